```python
import math
import jax, jax.numpy as jnp
from jax import lax
import numpy as np


D_MODEL = 1024
BATCH = 1
SEQ = 16384
DEPTH = 2

N_META = 16
SB_HEAD_DIM = 64
SB_WIDTH = D_MODEL // 2
SB_HEADS = SB_WIDTH // SB_HEAD_DIM
SB_BLOCK = 128
S5_WIDTH = D_MODEL - SB_WIDTH
S5_GROUP = 16
S5_GROUPS = S5_WIDTH // S5_GROUP
S5_STATE = 64
DN_HEAD_DIM = 128
DN_WIDTH = D_MODEL
DN_HEADS = DN_WIDTH // DN_HEAD_DIM
DN_CONV = 4
DN_CHUNK = 64
D_FF = 4 * D_MODEL
N_EVEN = (DEPTH + 1) // 2
N_ODD = DEPTH // 2
EPS = 1e-6

kernel_name = 'hybrid_stickbreak_s5_gated_deltanet'


def rms_norm(x, g):
    xf = x.astype(jnp.float32)
    y = xf * lax.rsqrt(jnp.mean(xf * xf, axis=-1, keepdims=True) + EPS)
    return (y * g.astype(jnp.float32)).astype(x.dtype)


def l2_norm(x):
    return x * lax.rsqrt(jnp.sum(x * x, axis=-1, keepdims=True) + EPS)


def stick_breaking_attention(q, k, v):
    b, l, h, d = q.shape
    pad = (-N_META) % SB_BLOCK
    lp = l + pad
    nb = lp // SB_BLOCK
    to_heads = lambda t: jnp.pad(t.transpose(0, 2, 1, 3), ((0, 0), (0, 0), (pad, 0), (0, 0)))
    qh, kh, vh = to_heads(q), to_heads(k), to_heads(v)
    q_blocks = qh.reshape(b, h, nb, SB_BLOCK, d).transpose(2, 0, 1, 3, 4)
    key_pos = jnp.arange(lp)
    scale = d ** -0.5

    def block(args):
        qb, bi = args
        q_pos = bi * SB_BLOCK + jnp.arange(SB_BLOCK)
        valid = (key_pos[None, :] < q_pos[:, None]) & (key_pos[None, :] >= pad)
        z = jnp.einsum('bhqd,bhkd->bhqk', qb, kh) * scale
        log_keep = jnp.where(valid, jax.nn.log_sigmoid(-z), 0.0)
        log_surv = lax.cumsum(log_keep, axis=3, reverse=True) - log_keep
        w = jnp.exp(jnp.where(valid, jax.nn.log_sigmoid(z) + log_surv, -jnp.inf))
        return jnp.einsum('bhqk,bhkd->bhqd', w, vh)

    out = lax.map(block, (q_blocks, jnp.arange(nb)))
    out = out.transpose(1, 0, 3, 2, 4).reshape(b, lp, h, d)
    return out[:, pad:]


def _complex_affine_combine(e1, e2):
    a1r, a1i, b1r, b1i = e1
    a2r, a2i, b2r, b2i = e2
    ar = a1r * a2r - a1i * a2i
    ai = a1r * a2i + a1i * a2r
    br = a2r * b1r - a2i * b1i + b2r
    bi = a2r * b1i + a2i * b1r + b2i
    return (ar, ai, br, bi)


def s5_glu(u, lam_re, lam_im, log_dt, b_re, b_im, c_re, c_im, d_skip, w_glu, b_glu):
    f32 = jnp.float32
    b, l, _ = u.shape
    lr = jnp.minimum(lam_re.astype(f32), -1e-4)
    li = lam_im.astype(f32)
    dt = jnp.exp(log_dt.astype(f32))[:, None]
    mag = jnp.exp(lr * dt)
    ang = li * dt
    abar_re, abar_im = mag * jnp.cos(ang), mag * jnp.sin(ang)
    den = lr * lr + li * li
    nr, ni = abar_re - 1.0, abar_im
    coef_re = (nr * lr + ni * li) / den
    coef_im = (ni * lr - nr * li) / den
    br, bim = b_re.astype(f32), b_im.astype(f32)
    bbar_re = coef_re[..., None] * br - coef_im[..., None] * bim
    bbar_im = coef_re[..., None] * bim + coef_im[..., None] * br
    ug = u.reshape(b, l, S5_GROUPS, S5_GROUP)
    bu_re = jnp.einsum('blgp,gnp->blgn', ug, bbar_re)
    bu_im = jnp.einsum('blgp,gnp->blgn', ug, bbar_im)
    a_re = jnp.broadcast_to(abar_re, bu_re.shape)
    a_im = jnp.broadcast_to(abar_im, bu_im.shape)
    _, _, x_re, x_im = lax.associative_scan(_complex_affine_combine, (a_re, a_im, bu_re, bu_im), axis=1)
    y = (jnp.einsum('blgn,gpn->blgp', x_re, c_re.astype(f32))
         - jnp.einsum('blgn,gpn->blgp', x_im, c_im.astype(f32)))
    y = y.reshape(b, l, S5_WIDTH) + d_skip.astype(f32) * u
    hact = jax.nn.gelu(y)
    return hact * jax.nn.sigmoid(hact @ w_glu.astype(f32) + b_glu.astype(f32))


def sb_s5_mixer(h, w_in, w_out, sb_norm_g, lam_re, lam_im, log_dt, b_re, b_im, c_re, c_im,
                d_skip, w_glu, b_glu, s5_norm_g):
    b, l, _ = h.shape
    proj = h @ w_in
    qkv = proj[..., :3 * SB_WIDTH].astype(jnp.float32).reshape(b, l, 3, SB_HEADS, SB_HEAD_DIM)
    o_sb = stick_breaking_attention(qkv[:, :, 0], qkv[:, :, 1], qkv[:, :, 2]).reshape(b, l, SB_WIDTH)
    o_s5 = s5_glu(proj[..., 3 * SB_WIDTH:].astype(jnp.float32), lam_re, lam_im, log_dt,
                  b_re, b_im, c_re, c_im, d_skip, w_glu, b_glu)
    merged = jnp.concatenate([rms_norm(o_sb, sb_norm_g), rms_norm(o_s5, s5_norm_g)], axis=-1)
    return merged.astype(h.dtype) @ w_out


def causal_conv(x, w):
    kw = w.shape[0]
    return lax.conv_general_dilated(x, w[:, None, :].astype(x.dtype), window_strides=(1,),
                                    padding=[(kw - 1, 0)], dimension_numbers=('NWC', 'WIO', 'NWC'),
                                    feature_group_count=x.shape[-1])


def gated_delta_rule(q, k, v, g, beta):
    f32 = jnp.float32
    b, l, h, dk = q.shape
    dv = v.shape[-1]
    cs = DN_CHUNK
    pad = (-N_META) % cs
    lp = l + pad
    nc = lp // cs

    def chunks(t):
        t = jnp.moveaxis(t, 2, 1)
        t = jnp.pad(t, [(0, 0), (0, 0), (pad, 0)] + [(0, 0)] * (t.ndim - 3))
        return t.reshape(t.shape[:2] + (nc, cs) + t.shape[3:])

    q, k, v, g, beta = chunks(q * dk ** -0.5), chunks(k), chunks(v), chunks(g), chunks(beta)
    gcum = jnp.cumsum(g, axis=-1)
    idx = jnp.arange(cs)
    incl = idx[:, None] >= idx[None, :]
    strict = idx[:, None] > idx[None, :]
    decay = jnp.exp(jnp.where(incl, gcum[..., :, None] - gcum[..., None, :], -jnp.inf))
    kk = jnp.einsum('bhncd,bhnsd->bhncs', k, k)
    lower = jnp.where(strict, beta[..., :, None] * kk * decay, 0.0)
    rhs = jnp.concatenate([v * beta[..., None], k * (beta * jnp.exp(gcum))[..., None]], axis=-1)
    sol = lax.linalg.triangular_solve(jnp.eye(cs, dtype=f32) + lower, rhs, left_side=True, lower=True)
    value, k_cumdecay = sol[..., :dv], sol[..., dv:]
    attn_intra = jnp.einsum('bhncd,bhnsd->bhncs', q, k) * decay
    q_g = q * jnp.exp(gcum)[..., None]
    g_last = gcum[..., -1]
    k_tail = k * jnp.exp(g_last[..., None] - gcum)[..., None]

    def step(state, inp):
        qg_c, kcd_c, val_c, intra_c, kt_c, gl_c = inp
        v_new = val_c - jnp.einsum('bhcd,bhde->bhce', kcd_c, state)
        o_c = jnp.einsum('bhcd,bhde->bhce', qg_c, state) + jnp.einsum('bhcs,bhse->bhce', intra_c, v_new)
        state = state * jnp.exp(gl_c)[..., None, None] + jnp.einsum('bhcd,bhce->bhde', kt_c, v_new)
        return state, o_c

    xs = tuple(jnp.moveaxis(t, 2, 0) for t in (q_g, k_cumdecay, value, attn_intra, k_tail, g_last))
    s0 = jnp.zeros((b, h, dk, dv), f32)
    _, out = lax.scan(step, s0, xs)
    out = out.transpose(1, 0, 3, 2, 4).reshape(b, lp, h, dv)
    return out[:, pad:]


def gated_deltanet_mixer(h, w_in, conv_w, a_log, dt_bias, norm_g, w_out):
    f32 = jnp.float32
    b, l, _ = h.shape
    wd, nh, hd = DN_WIDTH, DN_HEADS, DN_HEAD_DIM
    proj = h @ w_in
    qkv = jax.nn.silu(causal_conv(proj[..., :3 * wd], conv_w)).astype(f32)
    z = proj[..., 3 * wd:4 * wd].astype(f32).reshape(b, l, nh, hd)
    a = proj[..., 4 * wd:4 * wd + nh].astype(f32)
    bb = proj[..., 4 * wd + nh:].astype(f32)
    q, k, v = jnp.split(qkv, 3, axis=-1)
    q = l2_norm(q.reshape(b, l, nh, hd))
    k = l2_norm(k.reshape(b, l, nh, hd))
    v = v.reshape(b, l, nh, hd)
    beta = jax.nn.sigmoid(bb)
    g = -jnp.exp(a_log.astype(f32)) * jax.nn.softplus(a + dt_bias.astype(f32))
    o = gated_delta_rule(q, k, v, g, beta)
    o = rms_norm(o, norm_g) * jax.nn.silu(z)
    return o.reshape(b, l, wd).astype(h.dtype) @ w_out


def sq_relu_mlp(h, w1, w2):
    return jnp.square(jax.nn.relu(h @ w1)) @ w2


def setup_inputs(seed: int = 0) -> dict:
    key = jax.random.key(seed)
    ks = iter(jax.random.split(key, 40))
    f32 = jnp.float32
    nrm = lambda shape, s: jax.random.normal(next(ks), shape, f32) * s
    gain = lambda shape: 1.0 + nrm(shape, 0.02)
    log_uniform = lambda shape, lo, hi: jax.random.uniform(next(ks), shape, f32, math.log(lo), math.log(hi))
    dn_dt = jnp.exp(log_uniform((N_ODD, DN_HEADS), 1e-3, 1e-1))
    return {
        'x': nrm((BATCH, SEQ, D_MODEL), 1.0),
        'meta_tokens': nrm((N_META, D_MODEL), 1.0),
        'pre_mix_norm': gain((DEPTH, D_MODEL)),
        'post_mix_norm': gain((DEPTH, D_MODEL)),
        'pre_mlp_norm': gain((DEPTH, D_MODEL)),
        'post_mlp_norm': gain((DEPTH, D_MODEL)),
        'mlp_w1': nrm((DEPTH, D_MODEL, D_FF), D_MODEL ** -0.5),
        'mlp_w2': nrm((DEPTH, D_FF, D_MODEL), D_FF ** -0.5),
        'w_in_even': nrm((N_EVEN, D_MODEL, 3 * SB_WIDTH + S5_WIDTH), D_MODEL ** -0.5),
        'w_out_even': nrm((N_EVEN, SB_WIDTH + S5_WIDTH, D_MODEL), (SB_WIDTH + S5_WIDTH) ** -0.5),
        'sb_out_norm': gain((N_EVEN, SB_WIDTH)),
        's5_lambda_re': -0.5 + nrm((N_EVEN, S5_GROUPS, S5_STATE), 0.01),
        's5_lambda_im': jnp.pi * jnp.arange(S5_STATE, dtype=f32)[None, None, :] + nrm((N_EVEN, S5_GROUPS, S5_STATE), 0.01),
        's5_log_dt': log_uniform((N_EVEN, S5_GROUPS), 1e-3, 1e-1),
        's5_b_re': nrm((N_EVEN, S5_GROUPS, S5_STATE, S5_GROUP), (2 * S5_GROUP) ** -0.5),
        's5_b_im': nrm((N_EVEN, S5_GROUPS, S5_STATE, S5_GROUP), (2 * S5_GROUP) ** -0.5),
        's5_c_re': nrm((N_EVEN, S5_GROUPS, S5_GROUP, S5_STATE), S5_STATE ** -0.5),
        's5_c_im': nrm((N_EVEN, S5_GROUPS, S5_GROUP, S5_STATE), S5_STATE ** -0.5),
        's5_d': nrm((N_EVEN, S5_WIDTH), 0.5),
        's5_w_glu': nrm((N_EVEN, S5_WIDTH, S5_WIDTH), S5_WIDTH ** -0.5),
        's5_b_glu': nrm((N_EVEN, S5_WIDTH), 0.01),
        's5_out_norm': gain((N_EVEN, S5_WIDTH)),
        'w_in_odd': nrm((N_ODD, D_MODEL, 4 * DN_WIDTH + 2 * DN_HEADS), D_MODEL ** -0.5),
        'dn_conv_w': nrm((N_ODD, DN_CONV, 3 * DN_WIDTH), DN_CONV ** -0.5),
        'dn_a_log': jnp.log(jax.random.uniform(next(ks), (N_ODD, DN_HEADS), f32, 1.0, 16.0)),
        'dn_dt_bias': dn_dt + jnp.log(-jnp.expm1(-dn_dt)),
        'dn_out_norm': gain((N_ODD, DN_HEAD_DIM)),
        'w_out_odd': nrm((N_ODD, DN_WIDTH, D_MODEL), DN_WIDTH ** -0.5),
    }


def reference(x, meta_tokens, pre_mix_norm, post_mix_norm, pre_mlp_norm, post_mlp_norm, mlp_w1, mlp_w2,
              w_in_even, w_out_even, sb_out_norm, s5_lambda_re, s5_lambda_im, s5_log_dt, s5_b_re, s5_b_im,
              s5_c_re, s5_c_im, s5_d, s5_w_glu, s5_b_glu, s5_out_norm,
              w_in_odd, dn_conv_w, dn_a_log, dn_dt_bias, dn_out_norm, w_out_odd):
    b = x.shape[0]
    meta = jnp.broadcast_to(meta_tokens.astype(x.dtype)[None], (b, N_META, D_MODEL))
    hs = jnp.concatenate([meta, x], axis=1)
    for i in range(DEPTH):
        j = i // 2
        hn = rms_norm(hs, pre_mix_norm[i])
        if i % 2 == 0:
            mix = sb_s5_mixer(hn, w_in_even[j], w_out_even[j], sb_out_norm[j], s5_lambda_re[j], s5_lambda_im[j],
                              s5_log_dt[j], s5_b_re[j], s5_b_im[j], s5_c_re[j], s5_c_im[j], s5_d[j],
                              s5_w_glu[j], s5_b_glu[j], s5_out_norm[j])
        else:
            mix = gated_deltanet_mixer(hn, w_in_odd[j], dn_conv_w[j], dn_a_log[j], dn_dt_bias[j],
                                       dn_out_norm[j], w_out_odd[j])
        hs = hs + rms_norm(mix, post_mix_norm[i])
        hn = rms_norm(hs, pre_mlp_norm[i])
        hs = hs + rms_norm(sq_relu_mlp(hn, mlp_w1[i], mlp_w2[i]), post_mlp_norm[i])
    return hs[:, N_META:]
```

```python
import functools

import jax
import jax.numpy as jnp
from jax import lax
from jax.experimental import pallas as pl
from jax.experimental.pallas import tpu as pltpu

F32 = jnp.float32
BF16 = jnp.bfloat16

D_MODEL = 1024
N_META = 16
SB_HEAD_DIM = 64
SB_WIDTH = 512
S5_WIDTH = 512
S5_GROUP = 16
S5_GROUPS = 32
S5_STATE = 64
DN_HEAD_DIM = 128
DN_HEADS = 8
DN_CONV = 4
D_FF = 4096
EPS = 1e-6

LANES = 128
SUBLANES = 8
STREAMS = SUBLANES
OCTETS = S5_GROUPS // 8
OCT_STATES = 8 * S5_STATE
DN_CHUNK = 64
SB_TILE = 256
UNDERFLOW_LOG = -104.0
VMEM_LIMIT = 56 * 1024 * 1024


def _largest_div(n, cap, mult):
    best = None
    for d in range(mult, cap + 1, mult):
        if n % d == 0:
            best = d
    assert best is not None, (n, cap, mult)
    return best


def _cparams(*sem):
    return pltpu.CompilerParams(dimension_semantics=sem, vmem_limit_bytes=VMEM_LIMIT)


def _rms(x, g):
    ms = jnp.mean(x * x, axis=-1, keepdims=True)
    return x * lax.rsqrt(ms + EPS) * g


def _softplus(x):
    return jnp.maximum(x, 0.0) + jnp.log1p(jnp.exp(-jnp.abs(x)))


def _sigmoid(x):
    return 1.0 / (1.0 + jnp.exp(-x))


def _silu(x):
    return x * _sigmoid(x)


def _split3(x):
    hi = x.astype(BF16)
    r1 = x - hi.astype(F32)
    mid = r1.astype(BF16)
    lo = (r1 - mid.astype(F32)).astype(BF16)
    return hi, mid, lo


def _dot(a, b):
    return jnp.dot(a, b, preferred_element_type=F32)


def _dot_nt(a, b):
    return lax.dot_general(a, b, (((1,), (1,)), ((), ())), preferred_element_type=F32)


def _dot_tn(a, b):
    return lax.dot_general(a, b, (((0,), (0,)), ((), ())), preferred_element_type=F32)


def _const_spec(shape):
    nd = len(shape)
    return pl.BlockSpec(shape, lambda *_: (0,) * nd)


def _even_in_kernel(hs_ref, g_ref, w_ref, qkv_ref, u_ref):
    hn = _rms(hs_ref[...], g_ref[...]).astype(BF16)
    p = _dot(hn, w_ref[...])
    qkv_ref[:, :SB_WIDTH] = (p[:, :SB_WIDTH] * (SB_HEAD_DIM ** -0.5)).astype(BF16)
    qkv_ref[:, SB_WIDTH:] = p[:, SB_WIDTH:3 * SB_WIDTH].astype(BF16)
    u_ref[...] = p[:, 3 * SB_WIDTH:]


def _even_in(hs, g, w, tm):
    lp = hs.shape[0]
    ls = lp // STREAMS
    nt = ls // tm
    return pl.pallas_call(
        _even_in_kernel,
        grid=(lp // tm,),
        in_specs=[pl.BlockSpec((tm, D_MODEL), lambda i: (i, 0)),
                  _const_spec((1, D_MODEL)),
                  _const_spec(w.shape)],
        out_specs=[pl.BlockSpec((tm, 3 * SB_WIDTH), lambda i: (i, 0)),
                   pl.BlockSpec((tm, S5_WIDTH), lambda i: (i % nt, i // nt))],
        out_shape=[jax.ShapeDtypeStruct((lp, 3 * SB_WIDTH), BF16),
                   jax.ShapeDtypeStruct((ls, STREAMS * S5_WIDTH), F32)],
        compiler_params=_cparams("parallel"),
        name="even_in",
    )(hs, g, w)


def _sb_kernel(q_ref, k_ref, v_ref, o_ref, acc_ref, car_ref, *, tq):
    i = pl.program_id(1)
    q2 = q_ref[...]
    lane = lax.broadcasted_iota(jnp.int32, (1, LANES), 1)
    zq = jnp.zeros_like(q2)
    qs = (jnp.where(lane < SB_HEAD_DIM, q2, zq), jnp.where(lane >= SB_HEAD_DIM, q2, zq))
    row = i * tq + lax.broadcasted_iota(jnp.int32, (tq, 1), 0)
    jr = lax.broadcasted_iota(jnp.int32, (tq, tq), 0)
    jc = lax.broadcasted_iota(jnp.int32, (tq, tq), 1)
    later = jnp.where(jr > jc, 1.0, 0.0).astype(BF16)
    acc_ref[...] = jnp.zeros_like(acc_ref)
    car_ref[...] = jnp.zeros_like(car_ref)

    def cond(c):
        j, go = c
        return jnp.logical_and(j >= 0, go > 0)

    def body(c):
        j, _ = c
        off = pl.multiple_of(j * tq, tq)
        kb = k_ref[pl.ds(off, tq), :]
        vb = v_ref[pl.ds(off, tq), :]
        col = j * tq + lax.broadcasted_iota(jnp.int32, (1, tq), 1)
        valid = col < row
        top = None
        for h in range(2):
            z = _dot_nt(qs[h], kb)
            sp = _softplus(z)
            lk = jnp.where(valid, -sp, 0.0)
            hi = lk.astype(BF16)
            lo = (lk - hi.astype(F32)).astype(BF16)
            rc = _dot(hi, later) + _dot(lo, later)
            car = car_ref[h]
            w = jnp.where(valid, jnp.exp(z - sp + car + rc), 0.0)
            acc_ref[h] += _dot(w.astype(BF16), vb)
            ncar = car + jnp.sum(lk, axis=1, keepdims=True)
            car_ref[h] = ncar
            mh = jnp.max(ncar)
            top = mh if top is None else jnp.maximum(top, mh)
        return j - 1, (top > UNDERFLOW_LOG).astype(jnp.int32)

    lax.while_loop(cond, body, (i, jnp.int32(1)))
    o_ref[...] = jnp.where(lane < SB_HEAD_DIM, acc_ref[0], acc_ref[1])


def _sb_attention(qkv, tq):
    lp = qkv.shape[0]
    npair = SB_WIDTH // LANES
    return pl.pallas_call(
        functools.partial(_sb_kernel, tq=tq),
        grid=(npair, lp // tq),
        in_specs=[pl.BlockSpec((tq, LANES), lambda p, i: (i, p)),
                  pl.BlockSpec((lp, LANES), lambda p, i: (0, npair + p)),
                  pl.BlockSpec((lp, LANES), lambda p, i: (0, 2 * npair + p))],
        out_specs=pl.BlockSpec((tq, LANES), lambda p, i: (i, p)),
        out_shape=jax.ShapeDtypeStruct((lp, SB_WIDTH), F32),
        scratch_shapes=[pltpu.VMEM((2, tq, LANES), F32), pltpu.VMEM((2, tq, 1), F32)],
        compiler_params=_cparams("parallel", "parallel"),
        name="sb_attn",
    )(qkv, qkv, qkv)


def _cmul(ar, ai, br, bi):
    return ar * br - ai * bi, ar * bi + ai * br


def _s5_prep_kernel(lre_ref, lim_ref, ldt_ref, tab_ref, *, ls):
    lr = jnp.minimum(lre_ref[...], -1e-4)
    li = lim_ref[...]
    dt = jnp.exp(ldt_ref[...])
    mag = jnp.exp(lr * dt)
    ang = li * dt
    ar, ai = mag * jnp.cos(ang), mag * jnp.sin(ang)
    den = lr * lr + li * li
    nr, ni = ar - 1.0, ai
    cr = (nr * lr + ni * li) / den
    ci = (ni * lr - nr * li) / den
    pr, pi = jnp.ones_like(ar), jnp.zeros_like(ar)
    br, bi = ar, ai
    n = ls
    while n:
        if n & 1:
            pr, pi = _cmul(pr, pi, br, bi)
        n >>= 1
        if n:
            br, bi = _cmul(br, bi, br, bi)
    tab_ref[...] = jnp.concatenate(
        [ar, ai, cr, ci, pr, pi, jnp.zeros_like(ar), jnp.zeros_like(ar)], axis=0)


def _s5_prep(lam_re, lam_im, log_dt, ls):
    n = S5_GROUPS * S5_STATE
    row = lambda a: a.reshape(1, n)
    ldt = jnp.broadcast_to(log_dt[:, None], (S5_GROUPS, S5_STATE))
    return pl.pallas_call(
        functools.partial(_s5_prep_kernel, ls=ls),
        out_shape=jax.ShapeDtypeStruct((SUBLANES, n), F32),
        name="s5_prep",
    )(row(lam_re), row(lam_im), row(ldt))


def _s5_bbar(tab_ref, bre_ref, bim_ref, o):
    sl = slice(o * OCT_STATES, (o + 1) * OCT_STATES)
    cr, ci = tab_ref[2:3, sl], tab_ref[3:4, sl]
    bre, bim = bre_ref[o], bim_ref[o]
    return (cr * bre - ci * bim).astype(BF16), (cr * bim + ci * bre).astype(BF16)


def _s5_scan_octet(tab_ref, o, bur_ref, bui_ref, xr, xi, t_steps, hist=None):
    sl = slice(o * OCT_STATES, (o + 1) * OCT_STATES)
    ar = jnp.broadcast_to(tab_ref[0:1, sl], (STREAMS, OCT_STATES))
    ai = jnp.broadcast_to(tab_ref[1:2, sl], (STREAMS, OCT_STATES))

    def step(t, c):
        xr, xi = c
        rows = pl.ds(pl.multiple_of(t * STREAMS, STREAMS), STREAMS)
        nxr = ar * xr - ai * xi + bur_ref[rows, :]
        nxi = ar * xi + ai * xr + bui_ref[rows, :]
        if hist is not None:
            hist[0][rows, :] = nxr
            hist[1][rows, :] = nxi
        return nxr, nxi

    return lax.fori_loop(0, t_steps, step, (xr, xi), unroll=4)


def _s5_pass_a_kernel(u_ref, tab_ref, bre_ref, bim_ref, x0_ref, xr_ref, xi_ref, bur_ref, bui_ref, *, t_steps):
    i = pl.program_id(0)

    @pl.when(i == 0)
    def _():
        xr_ref[...] = jnp.zeros_like(xr_ref)
        xi_ref[...] = jnp.zeros_like(xi_ref)

    for o in range(OCTETS):
        ub = u_ref[:, o * LANES:(o + 1) * LANES].astype(BF16)
        bbr, bbi = _s5_bbar(tab_ref, bre_ref, bim_ref, o)
        bur_ref[...] = _dot(ub, bbr)
        bui_ref[...] = _dot(ub, bbi)
        xr, xi = _s5_scan_octet(tab_ref, o, bur_ref, bui_ref, xr_ref[o], xi_ref[o], t_steps)
        xr_ref[o] = xr
        xi_ref[o] = xi

    @pl.when(i == pl.num_programs(0) - 1)
    def _():
        sub = lax.broadcasted_iota(jnp.int32, (STREAMS, OCT_STATES), 0)
        for o in range(OCTETS):
            sl = slice(o * OCT_STATES, (o + 1) * OCT_STATES)
            pr, pi = tab_ref[4:5, sl], tab_ref[5:6, sl]
            er, ei = xr_ref[o], xi_ref[o]
            x0r = jnp.zeros((STREAMS, OCT_STATES), F32)
            x0i = jnp.zeros((STREAMS, OCT_STATES), F32)
            cr = jnp.zeros((1, OCT_STATES), F32)
            ci = jnp.zeros((1, OCT_STATES), F32)
            for r in range(1, STREAMS):
                mr, mi = _cmul(pr, pi, cr, ci)
                cr, ci = er[r - 1:r] + mr, ei[r - 1:r] + mi
                x0r = jnp.where(sub == r, cr, x0r)
                x0i = jnp.where(sub == r, ci, x0i)
            x0_ref[:, 2 * o * OCT_STATES:(2 * o + 1) * OCT_STATES] = x0r
            x0_ref[:, (2 * o + 1) * OCT_STATES:(2 * o + 2) * OCT_STATES] = x0i


def _gelu_tanh(x):
    return 0.5 * x * (1.0 + jnp.tanh(0.7978845608028654 * (x + 0.044715 * (x * x * x))))


def _s5_pass_b_kernel(u_ref, tab_ref, bre_ref, bim_ref, x0_ref, cre_ref, cim_ref, d_ref, wglu_ref, bglu_ref,
                      g_ref, o_ref, xr_ref, xi_ref, bur_ref, bui_ref, hr_ref, hi_ref, y_ref, *, t_steps):
    i = pl.program_id(0)

    @pl.when(i == 0)
    def _():
        for o in range(OCTETS):
            xr_ref[o] = x0_ref[:, 2 * o * OCT_STATES:(2 * o + 1) * OCT_STATES]
            xi_ref[o] = x0_ref[:, (2 * o + 1) * OCT_STATES:(2 * o + 2) * OCT_STATES]

    for o in range(OCTETS):
        ub = u_ref[:, o * LANES:(o + 1) * LANES].astype(BF16)
        bbr, bbi = _s5_bbar(tab_ref, bre_ref, bim_ref, o)
        bur_ref[...] = _dot(ub, bbr)
        bui_ref[...] = _dot(ub, bbi)
        xr, xi = _s5_scan_octet(tab_ref, o, bur_ref, bui_ref, xr_ref[o], xi_ref[o], t_steps,
                                hist=(hr_ref, hi_ref))
        xr_ref[o] = xr
        xi_ref[o] = xi
        y_ref[:, o * LANES:(o + 1) * LANES] = (_dot(hr_ref[...].astype(BF16), cre_ref[o])
                                                - _dot(hi_ref[...].astype(BF16), cim_ref[o]))

    y = y_ref[...] + d_ref[...] * u_ref[...]
    hact = _gelu_tanh(y)
    gate = _sigmoid(_dot(hact.astype(BF16), wglu_ref[...]) + bglu_ref[...])
    o_ref[...] = _rms(hact * gate, g_ref[...]).astype(BF16)


def _s5_block_diag_b(b):
    bt = b.reshape(OCTETS, 8, S5_STATE, S5_GROUP).transpose(0, 1, 3, 2)
    eye = jnp.eye(8, dtype=b.dtype)
    return jnp.einsum('ogpn,gh->ogphn', bt, eye).reshape(OCTETS, 8 * S5_GROUP, OCT_STATES)


def _s5_block_diag_c(c):
    ct = c.reshape(OCTETS, 8, S5_GROUP, S5_STATE).transpose(0, 1, 3, 2)
    eye = jnp.eye(8, dtype=c.dtype)
    return jnp.einsum('ognp,gh->ognhp', ct, eye).reshape(OCTETS, OCT_STATES, 8 * S5_GROUP)


def _s5(u_il, lam_re, lam_im, log_dt, b_re, b_im, c_re, c_im, d_skip, w_glu, b_glu, norm_g, t_steps):
    rows = u_il.shape[0]
    ls = rows // STREAMS
    tile = STREAMS * t_steps
    nsteps = ls // t_steps
    tab = _s5_prep(lam_re, lam_im, log_dt, ls)
    bre, bim = _s5_block_diag_b(b_re), _s5_block_diag_b(b_im)
    cre, cim = _s5_block_diag_c(c_re).astype(BF16), _s5_block_diag_c(c_im).astype(BF16)
    nstate = S5_GROUPS * S5_STATE
    state_scratch = [pltpu.VMEM((OCTETS, STREAMS, OCT_STATES), F32)] * 2
    bu_scratch = [pltpu.VMEM((tile, OCT_STATES), F32)] * 2
    u_spec = pl.BlockSpec((tile, S5_WIDTH), lambda i: (i, 0))

    x0 = pl.pallas_call(
        functools.partial(_s5_pass_a_kernel, t_steps=t_steps),
        grid=(nsteps,),
        in_specs=[u_spec, _const_spec(tab.shape), _const_spec(bre.shape), _const_spec(bim.shape)],
        out_specs=_const_spec((STREAMS, 2 * nstate)),
        out_shape=jax.ShapeDtypeStruct((STREAMS, 2 * nstate), F32),
        scratch_shapes=state_scratch + bu_scratch,
        compiler_params=_cparams("arbitrary"),
        name="s5_pass_a",
    )(u_il, tab, bre, bim)

    return pl.pallas_call(
        functools.partial(_s5_pass_b_kernel, t_steps=t_steps),
        grid=(nsteps,),
        in_specs=[u_spec, _const_spec(tab.shape), _const_spec(bre.shape), _const_spec(bim.shape),
                  _const_spec(x0.shape), _const_spec(cre.shape), _const_spec(cim.shape),
                  _const_spec((1, S5_WIDTH)), _const_spec(w_glu.shape), _const_spec((1, S5_WIDTH)),
                  _const_spec((1, S5_WIDTH))],
        out_specs=pl.BlockSpec((tile, S5_WIDTH), lambda i: (i, 0)),
        out_shape=jax.ShapeDtypeStruct((rows, S5_WIDTH), BF16),
        scratch_shapes=state_scratch + bu_scratch + bu_scratch + [pltpu.VMEM((tile, S5_WIDTH), F32)],
        compiler_params=_cparams("arbitrary"),
        name="s5_pass_b",
    )(u_il, tab, bre, bim, x0, cre, cim, d_skip.reshape(1, -1), w_glu.astype(BF16), b_glu.reshape(1, -1),
      norm_g.reshape(1, -1))


def _even_out_kernel(osb_ref, os5_ref, hs_ref, gsb_ref, wsb_ref, ws5_ref, gpost_ref, o_ref):
    sb = _rms(osb_ref[...], gsb_ref[...]).astype(BF16)
    mix = _dot(sb, wsb_ref[...]) + _dot(os5_ref[...], ws5_ref[...])
    o_ref[...] = hs_ref[...] + _rms(mix, gpost_ref[...])


def _even_out(o_sb, o_s5_il, hs, g_sb, w_out, g_post, tm):
    lp = hs.shape[0]
    ls = lp // STREAMS
    nt = ls // tm
    w_sb, w_s5 = w_out[:SB_WIDTH], w_out[SB_WIDTH:]
    return pl.pallas_call(
        _even_out_kernel,
        grid=(lp // tm,),
        in_specs=[pl.BlockSpec((tm, SB_WIDTH), lambda i: (i, 0)),
                  pl.BlockSpec((tm, S5_WIDTH), lambda i: (i % nt, i // nt)),
                  pl.BlockSpec((tm, D_MODEL), lambda i: (i, 0)),
                  _const_spec((1, SB_WIDTH)), _const_spec(w_sb.shape), _const_spec(w_s5.shape),
                  _const_spec((1, D_MODEL))],
        out_specs=pl.BlockSpec((tm, D_MODEL), lambda i: (i, 0)),
        out_shape=jax.ShapeDtypeStruct((lp, D_MODEL), F32),
        compiler_params=_cparams("parallel"),
        name="even_out",
    )(o_sb, o_s5_il, hs, g_sb, w_sb, w_s5, g_post)


def _mlp_kernel(hs_ref, gpre_ref, w1_ref, w2_ref, gpost_ref, o_ref, *, ff_tile):
    hs = hs_ref[...]
    hn = _rms(hs, gpre_ref[...]).astype(BF16)
    acc = jnp.zeros(hs.shape, F32)
    for f in range(D_FF // ff_tile):
        a = jnp.maximum(_dot(hn, w1_ref[:, f * ff_tile:(f + 1) * ff_tile]), 0.0)
        acc = acc + _dot((a * a).astype(BF16), w2_ref[f * ff_tile:(f + 1) * ff_tile, :])
    o_ref[...] = hs + _rms(acc, gpost_ref[...])


def _mlp(hs, g_pre, w1, w2, g_post, tm):
    lp = hs.shape[0]
    resident = lambda shape: pl.BlockSpec(shape, lambda i: (0, 0), pipeline_mode=pl.Buffered(1))
    return pl.pallas_call(
        functools.partial(_mlp_kernel, ff_tile=1024),
        grid=(lp // tm,),
        in_specs=[pl.BlockSpec((tm, D_MODEL), lambda i: (i, 0)),
                  _const_spec((1, D_MODEL)), resident(w1.shape), resident(w2.shape),
                  _const_spec((1, D_MODEL))],
        out_specs=pl.BlockSpec((tm, D_MODEL), lambda i: (i, 0)),
        out_shape=jax.ShapeDtypeStruct((lp, D_MODEL), F32),
        compiler_params=_cparams("parallel"),
        name="mlp",
    )(hs, g_pre, w1, w2, g_post)


def _chunk_tri(n, transposed):
    r = lax.broadcasted_iota(jnp.int32, (n, n), 0)
    c = lax.broadcasted_iota(jnp.int32, (n, n), 1)
    same = (r // DN_CHUNK) == (c // DN_CHUNK)
    order = (r <= c) if transposed else (r >= c)
    return jnp.where(jnp.logical_and(same, order), 1.0, 0.0).astype(BF16)


def _odd_in_kernel(hs_ref, g_ref, wqkv_ref, wz_ref, wabc_ref, wabr_ref, conv_ref, alc_ref, dtc_ref, alr_ref, dtr_ref,
                   q_ref, k_ref, v_ref, z_ref, gcol_ref, grow_ref, carry_ref, *, tm):
    i = pl.program_id(0)

    @pl.when(i == 0)
    def _():
        carry_ref[...] = jnp.zeros_like(carry_ref)

    hn = _rms(hs_ref[...], g_ref[...]).astype(BF16)
    z_ref[...] = _dot(hn, wz_ref[...]).astype(BF16)

    ab = _dot(hn, wabc_ref[...])
    lane = lax.broadcasted_iota(jnp.int32, (1, LANES), 1)
    gb = jnp.where(lane < DN_HEADS, -jnp.exp(alc_ref[...]) * _softplus(ab + dtc_ref[...]), _sigmoid(ab))
    tri = _chunk_tri(tm, transposed=False)
    hi, mid, lo = _split3(gb)
    gsum = _dot(tri, hi) + _dot(tri, mid) + _dot(tri, lo)
    gcol_ref[...] = jnp.where(lane < DN_HEADS, gsum, gb)[:, :2 * DN_HEADS]

    abr = _dot_nt(wabr_ref[...], hn)
    sub = lax.broadcasted_iota(jnp.int32, (2 * DN_HEADS, 1), 0)
    gbr = jnp.where(sub < DN_HEADS, -jnp.exp(alr_ref[...]) * _softplus(abr + dtr_ref[...]), _sigmoid(abr))
    trit = _chunk_tri(tm, transposed=True)
    hi, mid, lo = _split3(gbr)
    gsumr = _dot(hi, trit) + _dot(mid, trit) + _dot(lo, trit)
    grow_ref[...] = jnp.where(sub < DN_HEADS, gsumr, gbr)

    x = _dot(hn, wqkv_ref[...])
    prev = carry_ref[...]
    carry_ref[...] = x[tm - SUBLANES:, :]
    first = lax.broadcasted_iota(jnp.int32, (SUBLANES, 1), 0)
    y = x * conv_ref[DN_CONV - 1:DN_CONV, :]
    for s in range(1, DN_CONV):
        sh = pltpu.roll(x, s, 0)
        head = jnp.where(first < s, pltpu.roll(prev, s, 0), sh[:SUBLANES])
        sh = jnp.concatenate([head, sh[SUBLANES:]], axis=0)
        y = y + sh * conv_ref[DN_CONV - 1 - s:DN_CONV - s, :]
    y = _silu(y)
    w = DN_HEADS * DN_HEAD_DIM
    for h in range(DN_HEADS):
        sl = slice(h * DN_HEAD_DIM, (h + 1) * DN_HEAD_DIM)
        qh = y[:, sl]
        kh = y[:, w + h * DN_HEAD_DIM:w + (h + 1) * DN_HEAD_DIM]
        qn = qh * lax.rsqrt(jnp.sum(qh * qh, axis=-1, keepdims=True) + EPS)
        kn = kh * lax.rsqrt(jnp.sum(kh * kh, axis=-1, keepdims=True) + EPS)
        q_ref[:, sl] = (qn * (DN_HEAD_DIM ** -0.5)).astype(BF16)
        k_ref[:, sl] = kn.astype(BF16)
    v_ref[...] = y[:, 2 * w:].astype(BF16)


def _odd_in(hs, g, w_in, conv_w, a_log, dt_bias, tm):
    lp = hs.shape[0]
    w = DN_HEADS * DN_HEAD_DIM
    wqkv = w_in[:, :3 * w].astype(BF16)
    wz = w_in[:, 3 * w:4 * w].astype(BF16)
    wab = w_in[:, 4 * w:]
    wabc = jnp.pad(wab, ((0, 0), (0, LANES - 2 * DN_HEADS))).astype(BF16)
    wabr = wab.T.astype(BF16)
    pad_row = lambda a: jnp.pad(a.reshape(1, -1), ((0, 0), (0, LANES - DN_HEADS)))
    pad_col = lambda a: jnp.pad(a.reshape(-1, 1), ((0, DN_HEADS), (0, 0)))
    row_spec = lambda width: pl.BlockSpec((tm, width), lambda i: (i, 0))
    act = jax.ShapeDtypeStruct((lp, w), BF16)
    return pl.pallas_call(
        functools.partial(_odd_in_kernel, tm=tm),
        grid=(lp // tm,),
        in_specs=[row_spec(D_MODEL), _const_spec((1, D_MODEL)), _const_spec(wqkv.shape), _const_spec(wz.shape),
                  _const_spec(wabc.shape), _const_spec(wabr.shape), _const_spec(conv_w.shape),
                  _const_spec((1, LANES)), _const_spec((1, LANES)),
                  _const_spec((2 * DN_HEADS, 1)), _const_spec((2 * DN_HEADS, 1))],
        out_specs=[row_spec(w), row_spec(w), row_spec(w), row_spec(w), row_spec(2 * DN_HEADS),
                   pl.BlockSpec((2 * DN_HEADS, tm), lambda i: (0, i))],
        out_shape=[act, act, act, act,
                   jax.ShapeDtypeStruct((lp, 2 * DN_HEADS), F32),
                   jax.ShapeDtypeStruct((2 * DN_HEADS, lp), F32)],
        scratch_shapes=[pltpu.VMEM((SUBLANES, 3 * w), F32)],
        compiler_params=_cparams("arbitrary"),
        name="odd_in",
    )(hs, g, wqkv, wz, wabc, wabr, conv_w, pad_row(a_log), pad_row(dt_bias), pad_col(a_log), pad_col(dt_bias))


def _dn_kernel(q_ref, k_ref, v_ref, z_ref, gcol_ref, grow_ref, ng_ref, o_ref, s_ref, *, chunks):
    c64 = DN_CHUNK

    @pl.when(pl.program_id(0) == 0)
    def _():
        s_ref[...] = jnp.zeros_like(s_ref)

    ri = lax.broadcasted_iota(jnp.int32, (c64, c64), 0)
    ci = lax.broadcasted_iota(jnp.int32, (c64, c64), 1)
    incl = ri >= ci
    strict = ri > ci
    eye = jnp.where(ri == ci, 1.0, 0.0)
    pair_masks = []
    s = 1
    while s < c64:
        pair_masks.append(jnp.logical_and((ri // s) % 2 == 1, (ri // s) - 1 == ci // s))
        s *= 2
    ng = ng_ref[...]

    def chunk(c, carry):
        rows = pl.ds(pl.multiple_of(c * c64, c64), c64)
        gcb = gcol_ref[rows, :]
        grb = grow_ref[c]
        for h in range(DN_HEADS):
            sl = slice(h * DN_HEAD_DIM, (h + 1) * DN_HEAD_DIM)
            qb, kb, vb = q_ref[rows, sl], k_ref[rows, sl], v_ref[rows, sl]
            kf, vf = kb.astype(F32), vb.astype(F32)
            gc = gcb[:, h:h + 1]
            be = gcb[:, DN_HEADS + h:DN_HEADS + h + 1]
            gr = grb[h:h + 1, :]
            gl = gc[c64 - 1:c64, :]
            eg = jnp.exp(gc)
            decay = jnp.exp(jnp.where(incl, gc - gr, -jnp.inf))
            qkk = _dot_nt(jnp.concatenate([qb, kb], axis=0), kb)
            a_intra = qkk[:c64] * decay
            n = jnp.where(strict, be * qkk[c64:] * decay, 0.0)
            x = eye - jnp.where(pair_masks[0], n, 0.0)
            for m in pair_masks[1:]:
                xb = x.astype(BF16)
                x = x - _dot(xb, _dot(jnp.where(m, n, 0.0).astype(BF16), xb).astype(BF16))
            rhs = jnp.concatenate([(vf * be).astype(BF16), (kf * (be * eg)).astype(BF16)], axis=1)
            uw = _dot(x.astype(BF16), rhs)
            s = s_ref[h]
            sb = s.astype(BF16)
            qg = (qb.astype(F32) * eg).astype(BF16)
            ws = _dot(jnp.concatenate([uw[:, DN_HEAD_DIM:].astype(BF16), qg], axis=0), sb)
            v_new = uw[:, :DN_HEAD_DIM] - ws[:c64]
            vnb = v_new.astype(BF16)
            o = ws[c64:] + _dot(a_intra.astype(BF16), vnb)
            kt = (kf * jnp.exp(gl - gc)).astype(BF16)
            s_ref[h] = s * jnp.exp(gl) + _dot_tn(kt, vnb)
            on = o * lax.rsqrt(jnp.mean(o * o, axis=-1, keepdims=True) + EPS) * ng
            o_ref[rows, sl] = (on * _silu(z_ref[rows, sl].astype(F32))).astype(BF16)
        return carry

    lax.fori_loop(0, chunks, chunk, 0)


def _dn(q, k, v, z, gcol, grow, norm_g, chunks):
    lp = q.shape[0]
    w = DN_HEADS * DN_HEAD_DIM
    rows = chunks * DN_CHUNK
    grow3 = grow.reshape(2 * DN_HEADS, lp // DN_CHUNK, DN_CHUNK).transpose(1, 0, 2)
    act_spec = pl.BlockSpec((rows, w), lambda i: (i, 0))
    return pl.pallas_call(
        functools.partial(_dn_kernel, chunks=chunks),
        grid=(lp // rows,),
        in_specs=[act_spec, act_spec, act_spec, act_spec,
                  pl.BlockSpec((rows, 2 * DN_HEADS), lambda i: (i, 0)),
                  pl.BlockSpec((chunks, 2 * DN_HEADS, DN_CHUNK), lambda i: (i, 0, 0)),
                  _const_spec((1, DN_HEAD_DIM))],
        out_specs=act_spec,
        out_shape=jax.ShapeDtypeStruct((lp, w), BF16),
        scratch_shapes=[pltpu.VMEM((DN_HEADS, DN_HEAD_DIM, DN_HEAD_DIM), F32)],
        compiler_params=_cparams("arbitrary"),
        name="dn",
    )(q, k, v, z, gcol, grow3, norm_g.reshape(1, -1))


def _odd_out_kernel(og_ref, hs_ref, w_ref, gpost_ref, o_ref):
    o_ref[...] = hs_ref[...] + _rms(_dot(og_ref[...], w_ref[...]), gpost_ref[...])


def _odd_out(og, hs, w_out, g_post, tm):
    lp = hs.shape[0]
    return pl.pallas_call(
        _odd_out_kernel,
        grid=(lp // tm,),
        in_specs=[pl.BlockSpec((tm, og.shape[1]), lambda i: (i, 0)),
                  pl.BlockSpec((tm, D_MODEL), lambda i: (i, 0)),
                  _const_spec(w_out.shape), _const_spec((1, D_MODEL))],
        out_specs=pl.BlockSpec((tm, D_MODEL), lambda i: (i, 0)),
        out_shape=jax.ShapeDtypeStruct((lp, D_MODEL), F32),
        compiler_params=_cparams("parallel"),
        name="odd_out",
    )(og, hs, w_out, g_post)


def kernel(x, meta_tokens, pre_mix_norm, post_mix_norm, pre_mlp_norm, post_mlp_norm, mlp_w1, mlp_w2, w_in_even, w_out_even, sb_out_norm, s5_lambda_re, s5_lambda_im, s5_log_dt, s5_b_re, s5_b_im, s5_c_re, s5_c_im, s5_d, s5_w_glu, s5_b_glu, s5_out_norm, w_in_odd, dn_conv_w, dn_a_log, dn_dt_bias, dn_out_norm, w_out_odd):
    assert x.shape[0] == 1 and x.shape[2] == D_MODEL
    depth = pre_mix_norm.shape[0]
    l = N_META + x.shape[1]
    lp = -(-l // SB_TILE) * SB_TILE
    ls = lp // STREAMS
    tm_even = _largest_div(ls, 512, 16)
    t_s5 = _largest_div(ls, 128, SUBLANES)
    tm_mlp = _largest_div(lp, 640, 16)
    dn_chunks = _largest_div(lp // DN_CHUNK, 13, 1)
    row = lambda a: a.reshape(1, -1)

    hs = jnp.concatenate([meta_tokens.astype(x.dtype), x[0], jnp.zeros((lp - l, D_MODEL), x.dtype)], axis=0)
    for i in range(depth):
        j = i // 2
        if i % 2 == 0:
            qkv, u_il = _even_in(hs, row(pre_mix_norm[i]), w_in_even[j].astype(BF16), tm_even)
            o_sb = _sb_attention(qkv, SB_TILE)
            o_s5 = _s5(u_il.reshape(lp, S5_WIDTH), s5_lambda_re[j], s5_lambda_im[j], s5_log_dt[j],
                       s5_b_re[j], s5_b_im[j], s5_c_re[j], s5_c_im[j], s5_d[j], s5_w_glu[j], s5_b_glu[j],
                       s5_out_norm[j], t_s5)
            hs = _even_out(o_sb, o_s5.reshape(ls, STREAMS * S5_WIDTH), hs, row(sb_out_norm[j]),
                           w_out_even[j].astype(BF16), row(post_mix_norm[i]), tm_even)
        else:
            q, k, v, z, gcol, grow = _odd_in(hs, row(pre_mix_norm[i]), w_in_odd[j], dn_conv_w[j], dn_a_log[j],
                                             dn_dt_bias[j], SB_TILE)
            og = _dn(q, k, v, z, gcol, grow, dn_out_norm[j], dn_chunks)
            hs = _odd_out(og, hs, w_out_odd[j].astype(BF16), row(post_mix_norm[i]), tm_mlp)
        hs = _mlp(hs, row(pre_mlp_norm[i]), mlp_w1[i].astype(BF16), mlp_w2[i].astype(BF16),
                  row(post_mlp_norm[i]), tm_mlp)
    return hs[N_META:l][None]
```

```python
import functools

import jax
import jax.numpy as jnp
from jax import lax
from jax.experimental import pallas as pl
from jax.experimental.pallas import tpu as pltpu

F32 = jnp.float32
BF16 = jnp.bfloat16

D_MODEL = 1024
N_META = 16
SB_HEAD_DIM = 64
SB_WIDTH = 512
S5_WIDTH = 512
S5_GROUP = 16
S5_GROUPS = 32
S5_STATE = 64
DN_HEAD_DIM = 128
DN_HEADS = 8
DN_CONV = 4
D_FF = 4096
FF_TILE = 1024
EPS = 1e-6

LANES = 128
SUBLANES = 8
STREAMS = SUBLANES
OCTETS = S5_GROUPS // 8
OCT_STATES = 8 * S5_STATE
DN_CHUNK = 64
DN_GROUP = 2
SB_TILE = 256
SB_KEY_TILE = 128
UNDERFLOW_LOG = -104.0
VMEM_LIMIT = 56 * 1024 * 1024


def _largest_div(n, cap, mult):
    best = None
    for d in range(mult, cap + 1, mult):
        if n % d == 0:
            best = d
    assert best is not None, (n, cap, mult)
    return best


def _cparams(*sem):
    return pltpu.CompilerParams(dimension_semantics=sem, vmem_limit_bytes=VMEM_LIMIT)


def _rms(x, g):
    ms = jnp.mean(x * x, axis=-1, keepdims=True)
    return x * lax.rsqrt(ms + EPS) * g


def _softplus(x):
    return jnp.maximum(x, 0.0) + jnp.log1p(jnp.exp(-jnp.abs(x)))


def _sigmoid(x):
    return 1.0 / (1.0 + jnp.exp(-x))


def _silu(x):
    return x * _sigmoid(x)


def _split3(x):
    hi = x.astype(BF16)
    r1 = x - hi.astype(F32)
    mid = r1.astype(BF16)
    lo = (r1 - mid.astype(F32)).astype(BF16)
    return hi, mid, lo


def _dot(a, b):
    return jnp.dot(a, b, preferred_element_type=F32)


def _dot_nt(a, b):
    return lax.dot_general(a, b, (((1,), (1,)), ((), ())), preferred_element_type=F32)


def _dot_tn(a, b):
    return lax.dot_general(a, b, (((0,), (0,)), ((), ())), preferred_element_type=F32)


def _const_spec(shape):
    nd = len(shape)
    return pl.BlockSpec(shape, lambda *_: (0,) * nd)


def _even_in_kernel(hs_ref, g_ref, w_ref, qkv_ref, u_ref):
    hn = _rms(hs_ref[...], g_ref[...]).astype(BF16)
    p = _dot(hn, w_ref[...])
    qkv_ref[:, :SB_WIDTH] = (p[:, :SB_WIDTH] * (SB_HEAD_DIM ** -0.5)).astype(BF16)
    qkv_ref[:, SB_WIDTH:] = p[:, SB_WIDTH:3 * SB_WIDTH].astype(BF16)
    u_ref[...] = p[:, 3 * SB_WIDTH:]


def _even_in(hs, g, w, tm):
    lp = hs.shape[0]
    ls = lp // STREAMS
    nt = ls // tm
    return pl.pallas_call(
        _even_in_kernel,
        grid=(lp // tm,),
        in_specs=[pl.BlockSpec((tm, D_MODEL), lambda i: (i, 0)),
                  _const_spec((1, D_MODEL)),
                  _const_spec(w.shape)],
        out_specs=[pl.BlockSpec((tm, 3 * SB_WIDTH), lambda i: (i, 0)),
                   pl.BlockSpec((tm, S5_WIDTH), lambda i: (i % nt, i // nt))],
        out_shape=[jax.ShapeDtypeStruct((lp, 3 * SB_WIDTH), BF16),
                   jax.ShapeDtypeStruct((ls, STREAMS * S5_WIDTH), F32)],
        compiler_params=_cparams("parallel"),
        name="even_in",
    )(hs, g, w)


def _sb_kernel(q_ref, k_ref, v_ref, o_ref, acc_ref, car_ref, *, tq):
    tk = SB_KEY_TILE
    npair = SB_WIDTH // LANES
    pairs = range(npair)
    i = pl.program_id(0)
    lane = lax.broadcasted_iota(jnp.int32, (1, LANES), 1)
    qs = []
    for p in pairs:
        q2 = q_ref[:, p * LANES:(p + 1) * LANES]
        zq = jnp.zeros_like(q2)
        qs.append((jnp.where(lane < SB_HEAD_DIM, q2, zq), jnp.where(lane >= SB_HEAD_DIM, q2, zq)))
    jr = lax.broadcasted_iota(jnp.int32, (tk, tk), 0)
    jc = lax.broadcasted_iota(jnp.int32, (tk, tk), 1)
    later = jnp.where(jr > jc, 1.0, 0.0).astype(BF16)
    acc_ref[...] = jnp.zeros_like(acc_ref)
    car_ref[...] = jnp.zeros_like(car_ref)

    def key_block(j, r0):
        rows = slice(r0, tq)
        nr = tq - r0
        off = pl.multiple_of(j * tk, tk)
        col = j * tk + lax.broadcasted_iota(jnp.int32, (1, tk), 1)
        row = i * tq + r0 + lax.broadcasted_iota(jnp.int32, (nr, 1), 0)
        valid = col < row
        valid = jnp.concatenate([valid, valid], axis=0)
        z = [_dot_nt(jnp.concatenate([qs[p][0][rows], qs[p][1][rows]], axis=0),
                     k_ref[pl.ds(off, tk), p * LANES:(p + 1) * LANES]) for p in pairs]
        sp = [jnp.maximum(z[p], 0.0) + jnp.log(1.0 + jnp.exp(-jnp.abs(z[p]))) for p in pairs]
        lk = [jnp.where(valid, -sp[p], 0.0) for p in pairs]
        hi = [lk[p].astype(BF16) for p in pairs]
        lo = [(lk[p] - hi[p].astype(F32)).astype(BF16) for p in pairs]
        rc = [_dot(jnp.concatenate([hi[p], lo[p]], axis=0), later) for p in pairs]
        car = [car_ref[p, :, rows].reshape(2 * nr, 1) for p in pairs]
        w = [jnp.where(valid, jnp.exp(z[p] - sp[p] + car[p] + rc[p][:2 * nr] + rc[p][2 * nr:]), 0.0).astype(BF16)
             for p in pairs]
        top = None
        for p in pairs:
            acc_ref[p, :, rows] += _dot(w[p], v_ref[pl.ds(off, tk), p * LANES:(p + 1) * LANES]).reshape(2, nr, LANES)
            ncar = car[p] + jnp.sum(lk[p], axis=1, keepdims=True)
            car_ref[p, :, rows] = ncar.reshape(2, nr, 1)
            mp = jnp.max(ncar)
            top = mp if top is None else jnp.maximum(top, mp)
        return top

    nd = tq // tk
    for d in range(nd - 1, 0, -1):
        key_block(i * nd + d, d * tk)

    def cond(c):
        j, go = c
        return jnp.logical_and(j >= 0, go > 0)

    def body(c):
        j, _ = c
        top = key_block(j, 0)
        return j - 1, (top > UNDERFLOW_LOG).astype(jnp.int32)

    lax.while_loop(cond, body, (i * nd, jnp.int32(1)))
    for p in pairs:
        o_ref[:, p * LANES:(p + 1) * LANES] = jnp.where(lane < SB_HEAD_DIM, acc_ref[p, 0], acc_ref[p, 1])


def _sb_attention(qkv, tq):
    lp = qkv.shape[0]
    npair = SB_WIDTH // LANES
    resident = lambda col: pl.BlockSpec((lp, SB_WIDTH), lambda i: (0, col), pipeline_mode=pl.Buffered(1))
    return pl.pallas_call(
        functools.partial(_sb_kernel, tq=tq),
        grid=(lp // tq,),
        in_specs=[pl.BlockSpec((tq, SB_WIDTH), lambda i: (i, 0)), resident(1), resident(2)],
        out_specs=pl.BlockSpec((tq, SB_WIDTH), lambda i: (i, 0)),
        out_shape=jax.ShapeDtypeStruct((lp, SB_WIDTH), F32),
        scratch_shapes=[pltpu.VMEM((npair, 2, tq, LANES), F32), pltpu.VMEM((npair, 2, tq, 1), F32)],
        compiler_params=_cparams("parallel"),
        name="sb_attn",
    )(qkv, qkv, qkv)


def _cmul(ar, ai, br, bi):
    return ar * br - ai * bi, ar * bi + ai * br


def _s5_prep_kernel(lre_ref, lim_ref, ldt_ref, tab_ref, *, ls):
    lr = jnp.minimum(lre_ref[...], -1e-4)
    li = lim_ref[...]
    dt = jnp.exp(ldt_ref[...])
    mag = jnp.exp(lr * dt)
    ang = li * dt
    ar, ai = mag * jnp.cos(ang), mag * jnp.sin(ang)
    den = lr * lr + li * li
    nr, ni = ar - 1.0, ai
    cr = (nr * lr + ni * li) / den
    ci = (ni * lr - nr * li) / den
    pr, pi = jnp.ones_like(ar), jnp.zeros_like(ar)
    br, bi = ar, ai
    n = ls
    while n:
        if n & 1:
            pr, pi = _cmul(pr, pi, br, bi)
        n >>= 1
        if n:
            br, bi = _cmul(br, bi, br, bi)
    tab_ref[...] = jnp.concatenate(
        [ar, ai, cr, ci, pr, pi, jnp.zeros_like(ar), jnp.zeros_like(ar)], axis=0)


def _s5_prep(lam_re, lam_im, log_dt, ls):
    n = S5_GROUPS * S5_STATE
    row = lambda a: a.reshape(1, n)
    ldt = jnp.broadcast_to(log_dt[:, None], (S5_GROUPS, S5_STATE))
    return pl.pallas_call(
        functools.partial(_s5_prep_kernel, ls=ls),
        out_shape=jax.ShapeDtypeStruct((SUBLANES, n), F32),
        name="s5_prep",
    )(row(lam_re), row(lam_im), row(ldt))


def _s5_bbar(tab_ref, bre_ref, bim_ref, o):
    sl = slice(o * OCT_STATES, (o + 1) * OCT_STATES)
    cr, ci = tab_ref[2:3, sl], tab_ref[3:4, sl]
    bre, bim = bre_ref[o], bim_ref[o]
    return (cr * bre - ci * bim).astype(BF16), (cr * bim + ci * bre).astype(BF16)


def _s5_scan_octet(tab_ref, o, bur_ref, bui_ref, xr, xi, t_steps, hist=None):
    sl = slice(o * OCT_STATES, (o + 1) * OCT_STATES)
    ar = jnp.broadcast_to(tab_ref[0:1, sl], (STREAMS, OCT_STATES))
    ai = jnp.broadcast_to(tab_ref[1:2, sl], (STREAMS, OCT_STATES))

    for t in range(t_steps):
        rows = slice(t * STREAMS, (t + 1) * STREAMS)
        xr, xi = ar * xr - ai * xi + bur_ref[o, rows, :], ar * xi + ai * xr + bui_ref[o, rows, :]
        if hist is not None:
            hist[0][o, rows, :] = xr
            hist[1][o, rows, :] = xi
    return xr, xi


def _s5_pass_a_kernel(u_ref, tab_ref, bre_ref, bim_ref, x0_ref, xr_ref, xi_ref, bur_ref, bui_ref, *, t_steps):
    i = pl.program_id(0)

    @pl.when(i == 0)
    def _():
        xr_ref[...] = jnp.zeros_like(xr_ref)
        xi_ref[...] = jnp.zeros_like(xi_ref)

    for o in range(OCTETS):
        ub = u_ref[:, o * LANES:(o + 1) * LANES].astype(BF16)
        bbr, bbi = _s5_bbar(tab_ref, bre_ref, bim_ref, o)
        bur_ref[o] = _dot(ub, bbr)
        bui_ref[o] = _dot(ub, bbi)
        xr, xi = _s5_scan_octet(tab_ref, o, bur_ref, bui_ref, xr_ref[o], xi_ref[o], t_steps)
        xr_ref[o] = xr
        xi_ref[o] = xi

    @pl.when(i == pl.num_programs(0) - 1)
    def _():
        sub = lax.broadcasted_iota(jnp.int32, (STREAMS, OCT_STATES), 0)
        for o in range(OCTETS):
            sl = slice(o * OCT_STATES, (o + 1) * OCT_STATES)
            pr, pi = tab_ref[4:5, sl], tab_ref[5:6, sl]
            er, ei = xr_ref[o], xi_ref[o]
            x0r = jnp.zeros((STREAMS, OCT_STATES), F32)
            x0i = jnp.zeros((STREAMS, OCT_STATES), F32)
            cr = jnp.zeros((1, OCT_STATES), F32)
            ci = jnp.zeros((1, OCT_STATES), F32)
            for r in range(1, STREAMS):
                mr, mi = _cmul(pr, pi, cr, ci)
                cr, ci = er[r - 1:r] + mr, ei[r - 1:r] + mi
                x0r = jnp.where(sub == r, cr, x0r)
                x0i = jnp.where(sub == r, ci, x0i)
            x0_ref[:, 2 * o * OCT_STATES:(2 * o + 1) * OCT_STATES] = x0r
            x0_ref[:, (2 * o + 1) * OCT_STATES:(2 * o + 2) * OCT_STATES] = x0i


def _gelu_tanh(x):
    return 0.5 * x * (1.0 + jnp.tanh(0.7978845608028654 * (x + 0.044715 * (x * x * x))))


def _s5_pass_b_kernel(u_ref, tab_ref, bre_ref, bim_ref, x0_ref, cre_ref, cim_ref, d_ref, wglu_ref, bglu_ref,
                      g_ref, o_ref, xr_ref, xi_ref, bur_ref, bui_ref, hr_ref, hi_ref, y_ref, *, t_steps):
    i = pl.program_id(0)

    @pl.when(i == 0)
    def _():
        for o in range(OCTETS):
            xr_ref[o] = x0_ref[:, 2 * o * OCT_STATES:(2 * o + 1) * OCT_STATES]
            xi_ref[o] = x0_ref[:, (2 * o + 1) * OCT_STATES:(2 * o + 2) * OCT_STATES]

    for o in range(OCTETS):
        ub = u_ref[:, o * LANES:(o + 1) * LANES].astype(BF16)
        bbr, bbi = _s5_bbar(tab_ref, bre_ref, bim_ref, o)
        bur_ref[o] = _dot(ub, bbr)
        bui_ref[o] = _dot(ub, bbi)
        xr, xi = _s5_scan_octet(tab_ref, o, bur_ref, bui_ref, xr_ref[o], xi_ref[o], t_steps,
                                hist=(hr_ref, hi_ref))
        xr_ref[o] = xr
        xi_ref[o] = xi
        y_ref[:, o * LANES:(o + 1) * LANES] = (_dot(hr_ref[o].astype(BF16), cre_ref[o])
                                                - _dot(hi_ref[o].astype(BF16), cim_ref[o]))

    y = y_ref[...] + d_ref[...] * u_ref[...]
    hact = _gelu_tanh(y)
    gate = _sigmoid(_dot(hact.astype(BF16), wglu_ref[...]) + bglu_ref[...])
    o_ref[...] = _rms(hact * gate, g_ref[...]).astype(BF16)


def _s5_block_diag_b(b):
    bt = b.reshape(OCTETS, 8, S5_STATE, S5_GROUP).transpose(0, 1, 3, 2)
    eye = jnp.eye(8, dtype=b.dtype)
    return jnp.einsum('ogpn,gh->ogphn', bt, eye).reshape(OCTETS, 8 * S5_GROUP, OCT_STATES)


def _s5_block_diag_c(c):
    ct = c.reshape(OCTETS, 8, S5_GROUP, S5_STATE).transpose(0, 1, 3, 2)
    eye = jnp.eye(8, dtype=c.dtype)
    return jnp.einsum('ognp,gh->ognhp', ct, eye).reshape(OCTETS, OCT_STATES, 8 * S5_GROUP)


def _s5(u_il, lam_re, lam_im, log_dt, b_re, b_im, c_re, c_im, d_skip, w_glu, b_glu, norm_g, t_steps):
    rows = u_il.shape[0]
    ls = rows // STREAMS
    tile = STREAMS * t_steps
    nsteps = ls // t_steps
    tab = _s5_prep(lam_re, lam_im, log_dt, ls)
    bre, bim = _s5_block_diag_b(b_re), _s5_block_diag_b(b_im)
    cre, cim = _s5_block_diag_c(c_re).astype(BF16), _s5_block_diag_c(c_im).astype(BF16)
    nstate = S5_GROUPS * S5_STATE
    state_scratch = [pltpu.VMEM((OCTETS, STREAMS, OCT_STATES), F32)] * 2
    bu_scratch = [pltpu.VMEM((OCTETS, tile, OCT_STATES), F32)] * 2
    u_spec = pl.BlockSpec((tile, S5_WIDTH), lambda i: (i, 0))

    x0 = pl.pallas_call(
        functools.partial(_s5_pass_a_kernel, t_steps=t_steps),
        grid=(nsteps,),
        in_specs=[u_spec, _const_spec(tab.shape), _const_spec(bre.shape), _const_spec(bim.shape)],
        out_specs=_const_spec((STREAMS, 2 * nstate)),
        out_shape=jax.ShapeDtypeStruct((STREAMS, 2 * nstate), F32),
        scratch_shapes=state_scratch + bu_scratch,
        compiler_params=_cparams("arbitrary"),
        name="s5_pass_a",
    )(u_il, tab, bre, bim)

    return pl.pallas_call(
        functools.partial(_s5_pass_b_kernel, t_steps=t_steps),
        grid=(nsteps,),
        in_specs=[u_spec, _const_spec(tab.shape), _const_spec(bre.shape), _const_spec(bim.shape),
                  _const_spec(x0.shape), _const_spec(cre.shape), _const_spec(cim.shape),
                  _const_spec((1, S5_WIDTH)), _const_spec(w_glu.shape), _const_spec((1, S5_WIDTH)),
                  _const_spec((1, S5_WIDTH))],
        out_specs=pl.BlockSpec((tile, S5_WIDTH), lambda i: (i, 0)),
        out_shape=jax.ShapeDtypeStruct((rows, S5_WIDTH), BF16),
        scratch_shapes=state_scratch + bu_scratch + bu_scratch + [pltpu.VMEM((tile, S5_WIDTH), F32)],
        compiler_params=_cparams("arbitrary"),
        name="s5_pass_b",
    )(u_il, tab, bre, bim, x0, cre, cim, d_skip.reshape(1, -1), w_glu.astype(BF16), b_glu.reshape(1, -1),
      norm_g.reshape(1, -1))


def _mlp_block(hs, gpre_ref, w1_ref, w2_ref, gpost_ref):
    hn = _rms(hs, gpre_ref[...]).astype(BF16)
    acc = jnp.zeros(hs.shape, F32)
    for f in range(D_FF // FF_TILE):
        a = jnp.maximum(_dot(hn, w1_ref[:, f * FF_TILE:(f + 1) * FF_TILE]), 0.0)
        acc = acc + _dot((a * a).astype(BF16), w2_ref[f * FF_TILE:(f + 1) * FF_TILE, :])
    return hs + _rms(acc, gpost_ref[...])


def _resident(shape):
    nd = len(shape)
    return pl.BlockSpec(shape, lambda *_: (0,) * nd, pipeline_mode=pl.Buffered(1))


def _even_out_kernel(osb_ref, os5_ref, hs_ref, gsb_ref, wsb_ref, ws5_ref, gpost_ref,
                     gpre_mlp_ref, w1_ref, w2_ref, gpost_mlp_ref, o_ref):
    sb = _rms(osb_ref[...], gsb_ref[...]).astype(BF16)
    mix = _dot(sb, wsb_ref[...]) + _dot(os5_ref[...], ws5_ref[...])
    hs = hs_ref[...] + _rms(mix, gpost_ref[...])
    o_ref[...] = _mlp_block(hs, gpre_mlp_ref, w1_ref, w2_ref, gpost_mlp_ref)


def _even_out(o_sb, o_s5_il, hs, g_sb, w_out, g_post, g_pre_mlp, w1, w2, g_post_mlp, tm):
    lp = hs.shape[0]
    ls = lp // STREAMS
    nt = ls // tm
    w_sb, w_s5 = w_out[:SB_WIDTH], w_out[SB_WIDTH:]
    return pl.pallas_call(
        _even_out_kernel,
        grid=(lp // tm,),
        in_specs=[pl.BlockSpec((tm, SB_WIDTH), lambda i: (i, 0)),
                  pl.BlockSpec((tm, S5_WIDTH), lambda i: (i % nt, i // nt)),
                  pl.BlockSpec((tm, D_MODEL), lambda i: (i, 0)),
                  _const_spec((1, SB_WIDTH)), _resident(w_sb.shape), _resident(w_s5.shape),
                  _const_spec((1, D_MODEL)),
                  _const_spec((1, D_MODEL)), _resident(w1.shape), _resident(w2.shape), _const_spec((1, D_MODEL))],
        out_specs=pl.BlockSpec((tm, D_MODEL), lambda i: (i, 0)),
        out_shape=jax.ShapeDtypeStruct((lp, D_MODEL), F32),
        compiler_params=_cparams("parallel"),
        name="even_out_mlp",
    )(o_sb, o_s5_il, hs, g_sb, w_sb, w_s5, g_post, g_pre_mlp, w1, w2, g_post_mlp)


def _chunk_tri(n, transposed):
    r = lax.broadcasted_iota(jnp.int32, (n, n), 0)
    c = lax.broadcasted_iota(jnp.int32, (n, n), 1)
    same = (r // DN_CHUNK) == (c // DN_CHUNK)
    order = (r <= c) if transposed else (r >= c)
    return jnp.where(jnp.logical_and(same, order), 1.0, 0.0).astype(BF16)


def _odd_in_kernel(hs_ref, g_ref, wqkv_ref, wz_ref, wabc_ref, wabr_ref, conv_ref, alc_ref, dtc_ref, alr_ref, dtr_ref,
                   q_ref, k_ref, v_ref, z_ref, gcol_ref, grow_ref, carry_ref, *, tm):
    i = pl.program_id(0)

    @pl.when(i == 0)
    def _():
        carry_ref[...] = jnp.zeros_like(carry_ref)

    hn = _rms(hs_ref[...], g_ref[...]).astype(BF16)
    zp = _dot(hn, wz_ref[...]).astype(BF16)

    ab = _dot(hn, wabc_ref[...])
    lane = lax.broadcasted_iota(jnp.int32, (1, LANES), 1)
    gb = jnp.where(lane < DN_HEADS, -jnp.exp(alc_ref[...]) * _softplus(ab + dtc_ref[...]), _sigmoid(ab))
    tri = _chunk_tri(tm, transposed=False)
    hi, mid, lo = _split3(gb)
    gsum = _dot(tri, hi) + _dot(tri, mid) + _dot(tri, lo)
    gcol_ref[...] = jnp.where(lane < DN_HEADS, gsum, gb)[:, :2 * DN_HEADS]

    abr = _dot_nt(wabr_ref[...], hn)
    sub = lax.broadcasted_iota(jnp.int32, (2 * DN_HEADS, 1), 0)
    gbr = jnp.where(sub < DN_HEADS, -jnp.exp(alr_ref[...]) * _softplus(abr + dtr_ref[...]), _sigmoid(abr))
    trit = _chunk_tri(tm, transposed=True)
    hi, mid, lo = _split3(gbr)
    gsumr = _dot(hi, trit) + _dot(mid, trit) + _dot(lo, trit)
    grow_ref[...] = jnp.where(sub < DN_HEADS, gsumr, gbr)

    x = _dot(hn, wqkv_ref[...])
    prev = carry_ref[...]
    carry_ref[...] = x[tm - SUBLANES:, :]
    first = lax.broadcasted_iota(jnp.int32, (SUBLANES, 1), 0)
    y = x * conv_ref[DN_CONV - 1:DN_CONV, :]
    for s in range(1, DN_CONV):
        sh = pltpu.roll(x, s, 0)
        head = jnp.where(first < s, pltpu.roll(prev, s, 0), sh[:SUBLANES])
        sh = jnp.concatenate([head, sh[SUBLANES:]], axis=0)
        y = y + sh * conv_ref[DN_CONV - 1 - s:DN_CONV - s, :]
    y = _silu(y)
    w = DN_HEADS * DN_HEAD_DIM
    for h in range(DN_HEADS):
        sl = slice(h * DN_HEAD_DIM, (h + 1) * DN_HEAD_DIM)
        qh = y[:, sl]
        kh = y[:, w + h * DN_HEAD_DIM:w + (h + 1) * DN_HEAD_DIM]
        qn = qh * lax.rsqrt(jnp.sum(qh * qh, axis=-1, keepdims=True) + EPS)
        kn = kh * lax.rsqrt(jnp.sum(kh * kh, axis=-1, keepdims=True) + EPS)
        q_ref[h] = (qn * (DN_HEAD_DIM ** -0.5)).astype(BF16)
        k_ref[h] = kn.astype(BF16)
        v_ref[h] = y[:, 2 * w + h * DN_HEAD_DIM:2 * w + (h + 1) * DN_HEAD_DIM].astype(BF16)
        z_ref[h] = zp[:, sl]


def _odd_in(hs, g, w_in, conv_w, a_log, dt_bias, tm):
    lp = hs.shape[0]
    w = DN_HEADS * DN_HEAD_DIM
    wqkv = w_in[:, :3 * w].astype(BF16)
    wz = w_in[:, 3 * w:4 * w].astype(BF16)
    wab = w_in[:, 4 * w:]
    wabc = jnp.pad(wab, ((0, 0), (0, LANES - 2 * DN_HEADS))).astype(BF16)
    wabr = wab.T.astype(BF16)
    pad_row = lambda a: jnp.pad(a.reshape(1, -1), ((0, 0), (0, LANES - DN_HEADS)))
    pad_col = lambda a: jnp.pad(a.reshape(-1, 1), ((0, DN_HEADS), (0, 0)))
    row_spec = lambda width: pl.BlockSpec((tm, width), lambda i: (i, 0))
    act = jax.ShapeDtypeStruct((DN_HEADS, lp, DN_HEAD_DIM), BF16)
    act_spec = pl.BlockSpec((DN_HEADS, tm, DN_HEAD_DIM), lambda i: (0, i, 0))
    return pl.pallas_call(
        functools.partial(_odd_in_kernel, tm=tm),
        grid=(lp // tm,),
        in_specs=[row_spec(D_MODEL), _const_spec((1, D_MODEL)), _const_spec(wqkv.shape), _const_spec(wz.shape),
                  _const_spec(wabc.shape), _const_spec(wabr.shape), _const_spec(conv_w.shape),
                  _const_spec((1, LANES)), _const_spec((1, LANES)),
                  _const_spec((2 * DN_HEADS, 1)), _const_spec((2 * DN_HEADS, 1))],
        out_specs=[act_spec, act_spec, act_spec, act_spec, row_spec(2 * DN_HEADS),
                   pl.BlockSpec((2 * DN_HEADS, tm), lambda i: (0, i))],
        out_shape=[act, act, act, act,
                   jax.ShapeDtypeStruct((lp, 2 * DN_HEADS), F32),
                   jax.ShapeDtypeStruct((2 * DN_HEADS, lp), F32)],
        scratch_shapes=[pltpu.VMEM((SUBLANES, 3 * w), F32)],
        compiler_params=_cparams("arbitrary"),
        name="odd_in",
    )(hs, g, wqkv, wz, wabc, wabr, conv_w, pad_row(a_log), pad_row(dt_bias), pad_col(a_log), pad_col(dt_bias))


def _bdot(a, b):
    return lax.dot_general(a, b, (((2,), (1,)), ((0,), (0,))), preferred_element_type=F32)


def _bdot_nt(a, b):
    return lax.dot_general(a, b, (((2,), (2,)), ((0,), (0,))), preferred_element_type=F32)


def _bdot_tn(a, b):
    return lax.dot_general(a, b, (((1,), (1,)), ((0,), (0,))), preferred_element_type=F32)


def _dn_kernel(q_ref, k_ref, v_ref, z_ref, gcol_ref, grow_ref, ng_ref, o_ref,
               s_ref, u_scr, wq_scr, a_scr, kt_scr, egl_scr, *, chunks):
    c64, nh, hd, grp = DN_CHUNK, DN_HEADS, DN_HEAD_DIM, DN_GROUP
    nb = nh * grp

    @pl.when(pl.program_id(0) == 0)
    def _():
        s_ref[...] = jnp.zeros_like(s_ref)

    ri = lax.broadcasted_iota(jnp.int32, (c64, c64), 0)
    ci = lax.broadcasted_iota(jnp.int32, (c64, c64), 1)
    incl = ri >= ci
    strict = ri > ci
    eye = jnp.where(ri == ci, 1.0, 0.0)
    pair_masks = []
    s = 1
    while s < c64:
        pair_masks.append(jnp.logical_and((ri // s) % 2 == 1, (ri // s) - 1 == ci // s))
        s *= 2
    ng = ng_ref[...]

    def prep(gi, carry):
        c0 = gi * grp
        rows = pl.ds(pl.multiple_of(c0 * c64, grp * c64), grp * c64)
        cs = pl.ds(c0, grp)
        qb = q_ref[:, rows, :].reshape(nb, c64, hd)
        kb = k_ref[:, rows, :].reshape(nb, c64, hd)
        vb = v_ref[:, rows, :].reshape(nb, c64, hd)
        kf, vf = kb.astype(F32), vb.astype(F32)
        gcb = gcol_ref[rows, :]
        gc = jnp.stack([gcb[:, h:h + 1] for h in range(nh)], axis=0).reshape(nb, c64, 1)
        be = jnp.stack([gcb[:, nh + h:nh + h + 1] for h in range(nh)], axis=0).reshape(nb, c64, 1)
        gr = grow_ref[0:nh, cs].reshape(nb, 1, c64)
        gl = gc[:, c64 - 1:c64, :]
        eg = jnp.exp(gc)
        decay = jnp.exp(jnp.where(incl, gc - gr, -jnp.inf))
        qkk = _bdot_nt(jnp.concatenate([qb, kb], axis=1), kb)
        n = jnp.where(strict, be * qkk[:, c64:] * decay, 0.0)
        x = eye - jnp.where(pair_masks[0], n, 0.0)
        for m in pair_masks[1:]:
            xb = x.astype(BF16)
            x = x - _bdot(xb, _bdot(jnp.where(m, n, 0.0).astype(BF16), xb).astype(BF16))
        rhs = jnp.concatenate([(vf * be).astype(BF16), (kf * (be * eg)).astype(BF16)], axis=2)
        uw = _bdot(x.astype(BF16), rhs)
        qg = (qb.astype(F32) * eg).astype(BF16)
        u_scr[:, cs] = uw[:, :, :hd].reshape(nh, grp, c64, hd)
        wq_scr[:, cs] = jnp.concatenate([uw[:, :, hd:].astype(BF16), qg], axis=1).reshape(nh, grp, 2 * c64, hd)
        a_scr[:, cs] = (qkk[:, :c64] * decay).astype(BF16).reshape(nh, grp, c64, c64)
        kt_scr[:, cs] = (kf * jnp.exp(gl - gc)).astype(BF16).reshape(nh, grp, c64, hd)
        egl_scr[:, cs] = jnp.broadcast_to(jnp.exp(gl), (nb, 1, hd)).reshape(nh, grp, 1, hd)
        return carry

    lax.fori_loop(0, chunks // grp, prep, 0)

    def step(c, carry):
        rows = pl.ds(pl.multiple_of(c * c64, c64), c64)
        s = s_ref[...]
        ws = _bdot(wq_scr[:, c], s.astype(BF16))
        vnb = (u_scr[:, c] - ws[:, :c64]).astype(BF16)
        o = ws[:, c64:] + _bdot(a_scr[:, c], vnb)
        s_ref[...] = s * egl_scr[:, c] + _bdot_tn(kt_scr[:, c], vnb)
        on = o * lax.rsqrt(jnp.mean(o * o, axis=-1, keepdims=True) + EPS) * ng
        o_ref[:, rows, :] = (on * _silu(z_ref[:, rows, :].astype(F32))).astype(BF16)
        return carry

    lax.fori_loop(0, chunks, step, 0)


def _dn(q, k, v, z, gcol, grow, norm_g, chunks):
    nh, lp, hd = q.shape
    rows = chunks * DN_CHUNK
    grow4 = grow.reshape(2 * nh, lp // DN_CHUNK, 1, DN_CHUNK)
    act_spec = pl.BlockSpec((nh, rows, hd), lambda i: (0, i, 0))
    return pl.pallas_call(
        functools.partial(_dn_kernel, chunks=chunks),
        grid=(lp // rows,),
        in_specs=[act_spec, act_spec, act_spec, act_spec,
                  pl.BlockSpec((rows, 2 * nh), lambda i: (i, 0)),
                  pl.BlockSpec((2 * nh, chunks, 1, DN_CHUNK), lambda i: (0, i, 0, 0)),
                  _const_spec((1, hd))],
        out_specs=act_spec,
        out_shape=jax.ShapeDtypeStruct((nh, lp, hd), BF16),
        scratch_shapes=[pltpu.VMEM((nh, hd, hd), F32),
                        pltpu.VMEM((nh, chunks, DN_CHUNK, hd), F32),
                        pltpu.VMEM((nh, chunks, 2 * DN_CHUNK, hd), BF16),
                        pltpu.VMEM((nh, chunks, DN_CHUNK, DN_CHUNK), BF16),
                        pltpu.VMEM((nh, chunks, DN_CHUNK, hd), BF16),
                        pltpu.VMEM((nh, chunks, 1, hd), F32)],
        compiler_params=_cparams("arbitrary"),
        name="dn",
    )(q, k, v, z, gcol, grow4, norm_g.reshape(1, -1))


def _odd_out_kernel(og_ref, hs_ref, w_ref, gpost_ref, gpre_mlp_ref, w1_ref, w2_ref, gpost_mlp_ref, o_ref):
    og = jnp.concatenate([og_ref[h] for h in range(DN_HEADS)], axis=1)
    hs = hs_ref[...] + _rms(_dot(og, w_ref[...]), gpost_ref[...])
    o_ref[...] = _mlp_block(hs, gpre_mlp_ref, w1_ref, w2_ref, gpost_mlp_ref)


def _odd_out(og, hs, w_out, g_post, g_pre_mlp, w1, w2, g_post_mlp, tm):
    lp = hs.shape[0]
    return pl.pallas_call(
        _odd_out_kernel,
        grid=(lp // tm,),
        in_specs=[pl.BlockSpec((DN_HEADS, tm, DN_HEAD_DIM), lambda i: (0, i, 0)),
                  pl.BlockSpec((tm, D_MODEL), lambda i: (i, 0)),
                  _resident(w_out.shape), _const_spec((1, D_MODEL)),
                  _const_spec((1, D_MODEL)), _resident(w1.shape), _resident(w2.shape), _const_spec((1, D_MODEL))],
        out_specs=pl.BlockSpec((tm, D_MODEL), lambda i: (i, 0)),
        out_shape=jax.ShapeDtypeStruct((lp, D_MODEL), F32),
        compiler_params=_cparams("parallel"),
        name="odd_out_mlp",
    )(og, hs, w_out, g_post, g_pre_mlp, w1, w2, g_post_mlp)


def kernel(x, meta_tokens, pre_mix_norm, post_mix_norm, pre_mlp_norm, post_mlp_norm, mlp_w1, mlp_w2, w_in_even, w_out_even, sb_out_norm, s5_lambda_re, s5_lambda_im, s5_log_dt, s5_b_re, s5_b_im, s5_c_re, s5_c_im, s5_d, s5_w_glu, s5_b_glu, s5_out_norm, w_in_odd, dn_conv_w, dn_a_log, dn_dt_bias, dn_out_norm, w_out_odd):
    assert x.shape[0] == 1 and x.shape[2] == D_MODEL
    depth = pre_mix_norm.shape[0]
    l = N_META + x.shape[1]
    lp = -(-l // SB_TILE) * SB_TILE
    ls = lp // STREAMS
    tm_even = _largest_div(ls, 512, 16)
    t_s5 = _largest_div(ls, 128, SUBLANES)
    tm_mlp = _largest_div(lp, 640, 16)
    dn_chunks = _largest_div(lp // DN_CHUNK, 10, DN_GROUP)
    row = lambda a: a.reshape(1, -1)

    hs = jnp.concatenate([meta_tokens.astype(x.dtype), x[0], jnp.zeros((lp - l, D_MODEL), x.dtype)], axis=0)
    for i in range(depth):
        j = i // 2
        mlp = (row(pre_mlp_norm[i]), mlp_w1[i].astype(BF16), mlp_w2[i].astype(BF16), row(post_mlp_norm[i]))
        if i % 2 == 0:
            qkv, u_il = _even_in(hs, row(pre_mix_norm[i]), w_in_even[j].astype(BF16), tm_even)
            o_sb = _sb_attention(qkv, SB_TILE)
            o_s5 = _s5(u_il.reshape(lp, S5_WIDTH), s5_lambda_re[j], s5_lambda_im[j], s5_log_dt[j],
                       s5_b_re[j], s5_b_im[j], s5_c_re[j], s5_c_im[j], s5_d[j], s5_w_glu[j], s5_b_glu[j],
                       s5_out_norm[j], t_s5)
            hs = _even_out(o_sb, o_s5.reshape(ls, STREAMS * S5_WIDTH), hs, row(sb_out_norm[j]),
                           w_out_even[j].astype(BF16), row(post_mix_norm[i]), *mlp, tm_even)
        else:
            q, k, v, z, gcol, grow = _odd_in(hs, row(pre_mix_norm[i]), w_in_odd[j], dn_conv_w[j], dn_a_log[j],
                                             dn_dt_bias[j], SB_TILE)
            og = _dn(q, k, v, z, gcol, grow, dn_out_norm[j], dn_chunks)
            hs = _odd_out(og, hs, w_out_odd[j].astype(BF16), row(post_mix_norm[i]), *mlp, tm_mlp)
    return hs[N_META:l][None]
```

```python
import functools

import jax
import jax.numpy as jnp
from jax import lax
from jax.experimental import pallas as pl
from jax.experimental.pallas import tpu as pltpu

F32 = jnp.float32
BF16 = jnp.bfloat16

D_MODEL = 1024
N_META = 16
SB_HEAD_DIM = 64
SB_WIDTH = 512
S5_WIDTH = 512
S5_GROUP = 16
S5_GROUPS = 32
S5_STATE = 64
DN_HEAD_DIM = 128
DN_HEADS = 8
DN_CONV = 4
D_FF = 4096
FF_TILE = 1024
EPS = 1e-6

LANES = 128
SUBLANES = 8
STREAMS = SUBLANES
OCTETS = S5_GROUPS // 8
OCT_STATES = 8 * S5_STATE
DN_CHUNK = 64
DN_GROUP = 4
SB_TILE = 256
SB_KEY_TILE = 128
UNDERFLOW_LOG = -87.5
VMEM_LIMIT = 56 * 1024 * 1024


def _largest_div(n, cap, mult):
    best = None
    for d in range(mult, cap + 1, mult):
        if n % d == 0:
            best = d
    assert best is not None, (n, cap, mult)
    return best


def _cparams(*sem):
    return pltpu.CompilerParams(dimension_semantics=sem, vmem_limit_bytes=VMEM_LIMIT)


def _rms(x, g):
    ms = jnp.mean(x * x, axis=-1, keepdims=True)
    return x * lax.rsqrt(ms + EPS) * g


def _softplus(x):
    return jnp.maximum(x, 0.0) + jnp.log1p(jnp.exp(-jnp.abs(x)))


def _sigmoid(x):
    return 1.0 / (1.0 + jnp.exp(-x))


def _silu(x):
    return x * _sigmoid(x)


def _split3(x):
    hi = x.astype(BF16)
    r1 = x - hi.astype(F32)
    mid = r1.astype(BF16)
    lo = (r1 - mid.astype(F32)).astype(BF16)
    return hi, mid, lo


def _dot(a, b):
    return jnp.dot(a, b, preferred_element_type=F32)


def _dot_nt(a, b):
    return lax.dot_general(a, b, (((1,), (1,)), ((), ())), preferred_element_type=F32)


def _dot_tn(a, b):
    return lax.dot_general(a, b, (((0,), (0,)), ((), ())), preferred_element_type=F32)


def _const_spec(shape):
    nd = len(shape)
    return pl.BlockSpec(shape, lambda *_: (0,) * nd)


def _even_in_kernel(hs_ref, g_ref, w_ref, qkv_ref, *u_refs, tr):
    x = hs_ref[...].reshape(STREAMS * tr, D_MODEL)
    hn = _rms(x, g_ref[...]).astype(BF16)
    p = _dot(hn, w_ref[...])
    q = p[:, :SB_WIDTH] * (SB_HEAD_DIM ** -0.5)
    qkv_ref[:, :, :SB_WIDTH] = q.reshape(STREAMS, tr, SB_WIDTH).astype(BF16)
    qkv_ref[:, :, SB_WIDTH:] = p[:, SB_WIDTH:3 * SB_WIDTH].reshape(STREAMS, tr, 2 * SB_WIDTH).astype(BF16)
    for o, u_ref in enumerate(u_refs):
        c0 = 3 * SB_WIDTH + o * LANES
        for r in range(STREAMS):
            u_ref[pl.ds(r, tr, stride=STREAMS), :] = p[r * tr:(r + 1) * tr, c0:c0 + LANES]


def _even_in(hs, g, w, tr):
    lp = hs.shape[0]
    ls = lp // STREAMS
    qkv, *us = pl.pallas_call(
        functools.partial(_even_in_kernel, tr=tr),
        grid=(ls // tr,),
        in_specs=[pl.BlockSpec((STREAMS, tr, D_MODEL), lambda i: (0, i, 0)),
                  _const_spec((1, D_MODEL)),
                  _const_spec(w.shape)],
        out_specs=[pl.BlockSpec((STREAMS, tr, 3 * SB_WIDTH), lambda i: (0, i, 0))]
                  + [pl.BlockSpec((STREAMS * tr, LANES), lambda i: (i, 0))] * OCTETS,
        out_shape=[jax.ShapeDtypeStruct((STREAMS, ls, 3 * SB_WIDTH), BF16)]
                  + [jax.ShapeDtypeStruct((lp, LANES), F32)] * OCTETS,
        compiler_params=_cparams("parallel"),
        name="even_in",
    )(hs.reshape(STREAMS, ls, D_MODEL), g, w)
    return qkv.reshape(lp, 3 * SB_WIDTH), us


def _sb_kernel(q_ref, k_ref, v_ref, o_ref, acc_ref, car_ref, *, tq):
    tk = SB_KEY_TILE
    npair = SB_WIDTH // LANES
    pairs = range(npair)
    i = pl.program_id(0)
    lane = lax.broadcasted_iota(jnp.int32, (1, LANES), 1)
    qs = []
    for p in pairs:
        q2 = q_ref[:, p * LANES:(p + 1) * LANES]
        zq = jnp.zeros_like(q2)
        qs.append((jnp.where(lane < SB_HEAD_DIM, q2, zq), jnp.where(lane >= SB_HEAD_DIM, q2, zq)))
    jr = lax.broadcasted_iota(jnp.int32, (tk, tk), 0)
    jc = lax.broadcasted_iota(jnp.int32, (tk, tk), 1)
    later = jnp.where(jr > jc, 1.0, 0.0).astype(BF16)
    acc_ref[...] = jnp.zeros_like(acc_ref)
    car_ref[...] = jnp.zeros_like(car_ref)

    def key_block(j, r0):
        rows = slice(r0, tq)
        nr = tq - r0
        off = pl.multiple_of(j * tk, tk)
        col = j * tk + lax.broadcasted_iota(jnp.int32, (1, tk), 1)
        row = i * tq + r0 + lax.broadcasted_iota(jnp.int32, (nr, 1), 0)
        valid = col < row
        valid = jnp.concatenate([valid, valid], axis=0)
        z = [_dot_nt(jnp.concatenate([qs[p][0][rows], qs[p][1][rows]], axis=0),
                     k_ref[pl.ds(off, tk), p * LANES:(p + 1) * LANES]) for p in pairs]
        sp = [jnp.maximum(z[p], 0.0) + jnp.log(1.0 + jnp.exp(-jnp.abs(z[p]))) for p in pairs]
        lk = [jnp.where(valid, -sp[p], 0.0) for p in pairs]
        hi = [lk[p].astype(BF16) for p in pairs]
        lo = [(lk[p] - hi[p].astype(F32)).astype(BF16) for p in pairs]
        rc = [_dot(jnp.concatenate([hi[p], lo[p]], axis=0), later) for p in pairs]
        car = [car_ref[p, :, rows].reshape(2 * nr, 1) for p in pairs]
        w = [jnp.where(valid, jnp.exp(z[p] - sp[p] + car[p] + rc[p][:2 * nr] + rc[p][2 * nr:]), 0.0).astype(BF16)
             for p in pairs]
        top = None
        for p in pairs:
            acc_ref[p, :, rows] += _dot(w[p], v_ref[pl.ds(off, tk), p * LANES:(p + 1) * LANES]).reshape(2, nr, LANES)
            ncar = car[p] + jnp.sum(lk[p], axis=1, keepdims=True)
            car_ref[p, :, rows] = ncar.reshape(2, nr, 1)
            mp = jnp.max(ncar)
            top = mp if top is None else jnp.maximum(top, mp)
        return top

    nd = tq // tk
    for d in range(nd - 1, 0, -1):
        key_block(i * nd + d, d * tk)

    def cond(c):
        j, go = c
        return jnp.logical_and(j >= 0, go > 0)

    def body(c):
        j, _ = c
        top = key_block(j, 0)
        return j - 1, (top > UNDERFLOW_LOG).astype(jnp.int32)

    lax.while_loop(cond, body, (i * nd, jnp.int32(1)))
    for p in pairs:
        o_ref[:, p * LANES:(p + 1) * LANES] = jnp.where(lane < SB_HEAD_DIM, acc_ref[p, 0], acc_ref[p, 1])


def _sb_attention(qkv, tq):
    lp = qkv.shape[0]
    npair = SB_WIDTH // LANES
    resident = lambda col: pl.BlockSpec((lp, SB_WIDTH), lambda i: (0, col), pipeline_mode=pl.Buffered(1))
    return pl.pallas_call(
        functools.partial(_sb_kernel, tq=tq),
        grid=(lp // tq,),
        in_specs=[pl.BlockSpec((tq, SB_WIDTH), lambda i: (i, 0)), resident(1), resident(2)],
        out_specs=pl.BlockSpec((tq, SB_WIDTH), lambda i: (i, 0)),
        out_shape=jax.ShapeDtypeStruct((lp, SB_WIDTH), F32),
        scratch_shapes=[pltpu.VMEM((npair, 2, tq, LANES), F32), pltpu.VMEM((npair, 2, tq, 1), F32)],
        compiler_params=_cparams("parallel"),
        name="sb_attn",
    )(qkv, qkv, qkv)


def _cmul(ar, ai, br, bi):
    return ar * br - ai * bi, ar * bi + ai * br


def _s5_prep_kernel(lre_ref, lim_ref, ldt_ref, tab_ref, *, ls):
    lr = jnp.minimum(lre_ref[...], -1e-4)
    li = lim_ref[...]
    dt = jnp.exp(ldt_ref[...])
    mag = jnp.exp(lr * dt)
    ang = li * dt
    ar, ai = mag * jnp.cos(ang), mag * jnp.sin(ang)
    den = lr * lr + li * li
    nr, ni = ar - 1.0, ai
    cr = (nr * lr + ni * li) / den
    ci = (ni * lr - nr * li) / den
    pr, pi = jnp.ones_like(ar), jnp.zeros_like(ar)
    br, bi = ar, ai
    n = ls
    while n:
        if n & 1:
            pr, pi = _cmul(pr, pi, br, bi)
        n >>= 1
        if n:
            br, bi = _cmul(br, bi, br, bi)
    tab_ref[...] = jnp.concatenate(
        [ar, ai, cr, ci, pr, pi, jnp.zeros_like(ar), jnp.zeros_like(ar)], axis=0)


def _s5_prep(lam_re, lam_im, log_dt, ls):
    n = S5_GROUPS * S5_STATE
    row = lambda a: a.reshape(1, n)
    ldt = jnp.broadcast_to(log_dt[:, None], (S5_GROUPS, S5_STATE))
    return pl.pallas_call(
        functools.partial(_s5_prep_kernel, ls=ls),
        out_shape=jax.ShapeDtypeStruct((SUBLANES, n), F32),
        name="s5_prep",
    )(row(lam_re), row(lam_im), row(ldt))


def _s5_bbar(tab_ref, bre_ref, bim_ref, o):
    sl = slice(o * OCT_STATES, (o + 1) * OCT_STATES)
    cr, ci = tab_ref[2:3, sl], tab_ref[3:4, sl]
    bre, bim = bre_ref[o], bim_ref[o]
    return (cr * bre - ci * bim).astype(BF16), (cr * bim + ci * bre).astype(BF16)


def _s5_scan_octet(tab_ref, o, bur_ref, bui_ref, xr, xi, t_steps, hist=None):
    sl = slice(o * OCT_STATES, (o + 1) * OCT_STATES)
    ar = jnp.broadcast_to(tab_ref[0:1, sl], (STREAMS, OCT_STATES))
    ai = jnp.broadcast_to(tab_ref[1:2, sl], (STREAMS, OCT_STATES))

    for t in range(t_steps):
        rows = slice(t * STREAMS, (t + 1) * STREAMS)
        xr, xi = ar * xr - ai * xi + bur_ref[o, rows, :], ar * xi + ai * xr + bui_ref[o, rows, :]
        if hist is not None:
            hist[0][o, rows, :] = xr
            hist[1][o, rows, :] = xi
    return xr, xi


def _s5_pass_a_kernel(u0_ref, u1_ref, u2_ref, u3_ref, tab_ref, bre_ref, bim_ref, x0_ref,
                      xr_ref, xi_ref, bur_ref, bui_ref, *, t_steps):
    u_refs = (u0_ref, u1_ref, u2_ref, u3_ref)
    i = pl.program_id(0)

    @pl.when(i == 0)
    def _():
        xr_ref[...] = jnp.zeros_like(xr_ref)
        xi_ref[...] = jnp.zeros_like(xi_ref)

    for o in range(OCTETS):
        ub = u_refs[o][...].astype(BF16)
        bbr, bbi = _s5_bbar(tab_ref, bre_ref, bim_ref, o)
        bur_ref[o] = _dot(ub, bbr)
        bui_ref[o] = _dot(ub, bbi)
        xr, xi = _s5_scan_octet(tab_ref, o, bur_ref, bui_ref, xr_ref[o], xi_ref[o], t_steps)
        xr_ref[o] = xr
        xi_ref[o] = xi

    @pl.when(i == pl.num_programs(0) - 1)
    def _():
        sub = lax.broadcasted_iota(jnp.int32, (STREAMS, OCT_STATES), 0)
        for o in range(OCTETS):
            sl = slice(o * OCT_STATES, (o + 1) * OCT_STATES)
            pr, pi = tab_ref[4:5, sl], tab_ref[5:6, sl]
            er, ei = xr_ref[o], xi_ref[o]
            x0r = jnp.zeros((STREAMS, OCT_STATES), F32)
            x0i = jnp.zeros((STREAMS, OCT_STATES), F32)
            cr = jnp.zeros((1, OCT_STATES), F32)
            ci = jnp.zeros((1, OCT_STATES), F32)
            for r in range(1, STREAMS):
                mr, mi = _cmul(pr, pi, cr, ci)
                cr, ci = er[r - 1:r] + mr, ei[r - 1:r] + mi
                x0r = jnp.where(sub == r, cr, x0r)
                x0i = jnp.where(sub == r, ci, x0i)
            x0_ref[:, 2 * o * OCT_STATES:(2 * o + 1) * OCT_STATES] = x0r
            x0_ref[:, (2 * o + 1) * OCT_STATES:(2 * o + 2) * OCT_STATES] = x0i


def _gelu_tanh(x):
    return 0.5 * x * (1.0 + jnp.tanh(0.7978845608028654 * (x + 0.044715 * (x * x * x))))


def _s5_pass_b_kernel(u0_ref, u1_ref, u2_ref, u3_ref, tab_ref, bre_ref, bim_ref, x0_ref, cre_ref, cim_ref, d_ref,
                      wglu_ref, bglu_ref, g_ref, o_ref, xr_ref, xi_ref, bur_ref, bui_ref, hr_ref, hi_ref,
                      y0_ref, y1_ref, y2_ref, y3_ref, *, t_steps):
    u_refs = (u0_ref, u1_ref, u2_ref, u3_ref)
    y_refs = (y0_ref, y1_ref, y2_ref, y3_ref)
    i = pl.program_id(0)

    @pl.when(i == 0)
    def _():
        for o in range(OCTETS):
            xr_ref[o] = x0_ref[:, 2 * o * OCT_STATES:(2 * o + 1) * OCT_STATES]
            xi_ref[o] = x0_ref[:, (2 * o + 1) * OCT_STATES:(2 * o + 2) * OCT_STATES]

    ys = []
    for o in range(OCTETS):
        u = u_refs[o][...]
        bbr, bbi = _s5_bbar(tab_ref, bre_ref, bim_ref, o)
        bur_ref[o] = _dot(u.astype(BF16), bbr)
        bui_ref[o] = _dot(u.astype(BF16), bbi)
        xr, xi = _s5_scan_octet(tab_ref, o, bur_ref, bui_ref, xr_ref[o], xi_ref[o], t_steps,
                                hist=(hr_ref, hi_ref))
        xr_ref[o] = xr
        xi_ref[o] = xi
        ys.append(_dot(hr_ref[o].astype(BF16), cre_ref[o]) - _dot(hi_ref[o].astype(BF16), cim_ref[o])
                  + d_ref[:, o * LANES:(o + 1) * LANES] * u)

    hact = _gelu_tanh(jnp.concatenate(ys, axis=1))
    gate = _sigmoid(_dot(hact.astype(BF16), wglu_ref[...]) + bglu_ref[...])
    res = _rms(hact * gate, g_ref[...])
    for o in range(OCTETS):
        y_refs[o][...] = res[:, o * LANES:(o + 1) * LANES]
        for r in range(STREAMS):
            o_ref[r, :, o * LANES:(o + 1) * LANES] = y_refs[o][pl.ds(r, t_steps, stride=STREAMS), :]


def _s5_block_diag_b(b):
    bt = b.reshape(OCTETS, 8, S5_STATE, S5_GROUP).transpose(0, 1, 3, 2)
    eye = jnp.eye(8, dtype=b.dtype)
    return jnp.einsum('ogpn,gh->ogphn', bt, eye).reshape(OCTETS, 8 * S5_GROUP, OCT_STATES)


def _s5_block_diag_c(c):
    ct = c.reshape(OCTETS, 8, S5_GROUP, S5_STATE).transpose(0, 1, 3, 2)
    eye = jnp.eye(8, dtype=c.dtype)
    return jnp.einsum('ognp,gh->ognhp', ct, eye).reshape(OCTETS, OCT_STATES, 8 * S5_GROUP)


def _s5(us, lam_re, lam_im, log_dt, b_re, b_im, c_re, c_im, d_skip, w_glu, b_glu, norm_g, t_steps):
    rows = us[0].shape[0]
    ls = rows // STREAMS
    tile = STREAMS * t_steps
    nsteps = ls // t_steps
    tab = _s5_prep(lam_re, lam_im, log_dt, ls)
    bre, bim = _s5_block_diag_b(b_re), _s5_block_diag_b(b_im)
    cre, cim = _s5_block_diag_c(c_re).astype(BF16), _s5_block_diag_c(c_im).astype(BF16)
    nstate = S5_GROUPS * S5_STATE
    state_scratch = [pltpu.VMEM((OCTETS, STREAMS, OCT_STATES), F32)] * 2
    bu_scratch = [pltpu.VMEM((OCTETS, tile, OCT_STATES), F32)] * 2
    u_specs = [pl.BlockSpec((tile, LANES), lambda i: (i, 0))] * OCTETS

    x0 = pl.pallas_call(
        functools.partial(_s5_pass_a_kernel, t_steps=t_steps),
        grid=(nsteps,),
        in_specs=u_specs + [_const_spec(tab.shape), _const_spec(bre.shape), _const_spec(bim.shape)],
        out_specs=_const_spec((STREAMS, 2 * nstate)),
        out_shape=jax.ShapeDtypeStruct((STREAMS, 2 * nstate), F32),
        scratch_shapes=state_scratch + bu_scratch,
        compiler_params=_cparams("arbitrary"),
        name="s5_pass_a",
    )(*us, tab, bre, bim)

    return pl.pallas_call(
        functools.partial(_s5_pass_b_kernel, t_steps=t_steps),
        grid=(nsteps,),
        in_specs=u_specs + [_const_spec(tab.shape), _const_spec(bre.shape), _const_spec(bim.shape),
                            _const_spec(x0.shape), _const_spec(cre.shape), _const_spec(cim.shape),
                            _const_spec((1, S5_WIDTH)), _const_spec(w_glu.shape), _const_spec((1, S5_WIDTH)),
                            _const_spec((1, S5_WIDTH))],
        out_specs=pl.BlockSpec((STREAMS, t_steps, S5_WIDTH), lambda i: (0, i, 0)),
        out_shape=jax.ShapeDtypeStruct((STREAMS, ls, S5_WIDTH), F32),
        scratch_shapes=state_scratch + bu_scratch + bu_scratch + [pltpu.VMEM((tile, LANES), F32)] * OCTETS,
        compiler_params=_cparams("arbitrary"),
        name="s5_pass_b",
    )(*us, tab, bre, bim, x0, cre, cim, d_skip.reshape(1, -1), w_glu.astype(BF16), b_glu.reshape(1, -1),
      norm_g.reshape(1, -1))


def _mlp_block(hs, gpre_ref, w1_ref, w2_ref, gpost_ref):
    hn = _rms(hs, gpre_ref[...]).astype(BF16)
    acc = jnp.zeros(hs.shape, F32)
    for f in range(D_FF // FF_TILE):
        a = jnp.maximum(_dot(hn, w1_ref[:, f * FF_TILE:(f + 1) * FF_TILE]), 0.0)
        acc = acc + _dot((a * a).astype(BF16), w2_ref[f * FF_TILE:(f + 1) * FF_TILE, :])
    return hs + _rms(acc, gpost_ref[...])


def _resident(shape):
    nd = len(shape)
    return pl.BlockSpec(shape, lambda *_: (0,) * nd, pipeline_mode=pl.Buffered(1))


def _even_out_kernel(osb_ref, os5_ref, hs_ref, gsb_ref, wsb_ref, ws5_ref, gpost_ref,
                     gpre_mlp_ref, w1_ref, w2_ref, gpost_mlp_ref, o_ref):
    sb = _rms(osb_ref[...], gsb_ref[...]).astype(BF16)
    mix = _dot(sb, wsb_ref[...]) + _dot(os5_ref[...].astype(BF16), ws5_ref[...])
    hs = hs_ref[...] + _rms(mix, gpost_ref[...])
    o_ref[...] = _mlp_block(hs, gpre_mlp_ref, w1_ref, w2_ref, gpost_mlp_ref)


def _even_out(o_sb, o_s5, hs, g_sb, w_out, g_post, g_pre_mlp, w1, w2, g_post_mlp, tm):
    lp = hs.shape[0]
    w_sb, w_s5 = w_out[:SB_WIDTH], w_out[SB_WIDTH:]
    return pl.pallas_call(
        _even_out_kernel,
        grid=(lp // tm,),
        in_specs=[pl.BlockSpec((tm, SB_WIDTH), lambda i: (i, 0)),
                  pl.BlockSpec((tm, S5_WIDTH), lambda i: (i, 0)),
                  pl.BlockSpec((tm, D_MODEL), lambda i: (i, 0)),
                  _const_spec((1, SB_WIDTH)), _resident(w_sb.shape), _resident(w_s5.shape),
                  _const_spec((1, D_MODEL)),
                  _const_spec((1, D_MODEL)), _resident(w1.shape), _resident(w2.shape), _const_spec((1, D_MODEL))],
        out_specs=pl.BlockSpec((tm, D_MODEL), lambda i: (i, 0)),
        out_shape=jax.ShapeDtypeStruct((lp, D_MODEL), F32),
        compiler_params=_cparams("parallel"),
        name="even_out_mlp",
    )(o_sb, o_s5, hs, g_sb, w_sb, w_s5, g_post, g_pre_mlp, w1, w2, g_post_mlp)


def _chunk_tri(n, transposed):
    r = lax.broadcasted_iota(jnp.int32, (n, n), 0)
    c = lax.broadcasted_iota(jnp.int32, (n, n), 1)
    same = (r // DN_CHUNK) == (c // DN_CHUNK)
    order = (r <= c) if transposed else (r >= c)
    return jnp.where(jnp.logical_and(same, order), 1.0, 0.0).astype(BF16)


def _odd_in_kernel(hs_ref, g_ref, wqkv_ref, wz_ref, wabc_ref, wabr_ref, conv_ref, alc_ref, dtc_ref, alr_ref, dtr_ref,
                   q_ref, k_ref, v_ref, z_ref, gcol_ref, grow_ref, carry_ref, *, tm):
    i = pl.program_id(0)

    @pl.when(i == 0)
    def _():
        carry_ref[...] = jnp.zeros_like(carry_ref)

    hn = _rms(hs_ref[...], g_ref[...]).astype(BF16)

    ab = _dot(hn, wabc_ref[...])
    lane = lax.broadcasted_iota(jnp.int32, (1, LANES), 1)
    gb = jnp.where(lane < DN_HEADS, -jnp.exp(alc_ref[...]) * _softplus(ab + dtc_ref[...]), _sigmoid(ab))
    tri = _chunk_tri(tm, transposed=False)
    hi, mid, lo = _split3(gb)
    gsum = _dot(tri, hi) + _dot(tri, mid) + _dot(tri, lo)
    gcol_ref[...] = jnp.where(lane < DN_HEADS, gsum, gb)[:, :2 * DN_HEADS]

    abr = _dot_nt(wabr_ref[...], hn)
    sub = lax.broadcasted_iota(jnp.int32, (2 * DN_HEADS, 1), 0)
    gbr = jnp.where(sub < DN_HEADS, -jnp.exp(alr_ref[...]) * _softplus(abr + dtr_ref[...]), _sigmoid(abr))
    trit = _chunk_tri(tm, transposed=True)
    hi, mid, lo = _split3(gbr)
    gsumr = _dot(hi, trit) + _dot(mid, trit) + _dot(lo, trit)
    grow_ref[...] = jnp.where(sub < DN_HEADS, gsumr, gbr)

    first = lax.broadcasted_iota(jnp.int32, (SUBLANES, 1), 0)
    w = DN_HEADS * DN_HEAD_DIM
    hd = DN_HEAD_DIM

    def conv_silu(c0):
        cols = slice(c0, c0 + 2 * hd)
        x = _dot(hn, wqkv_ref[:, cols])
        prev = carry_ref[:, cols]
        carry_ref[:, cols] = x[tm - SUBLANES:, :]
        y = x * conv_ref[DN_CONV - 1:DN_CONV, cols]
        for s in range(1, DN_CONV):
            sh = pltpu.roll(x, s, 0)
            head = jnp.where(first < s, pltpu.roll(prev, s, 0), sh[:SUBLANES])
            sh = jnp.concatenate([head, sh[SUBLANES:]], axis=0)
            y = y + sh * conv_ref[DN_CONV - 1 - s:DN_CONV - s, cols]
        return _silu(y)

    for h in range(0, DN_HEADS, 2):
        q2 = conv_silu(h * hd)
        k2 = conv_silu(w + h * hd)
        v2 = conv_silu(2 * w + h * hd)
        z2 = _dot(hn, wz_ref[:, h * hd:(h + 2) * hd])
        for d in range(2):
            sl = slice(d * hd, (d + 1) * hd)
            qh, kh = q2[:, sl], k2[:, sl]
            q_ref[h + d] = (qh * (lax.rsqrt(jnp.sum(qh * qh, axis=-1, keepdims=True) + EPS)
                                  * (hd ** -0.5))).astype(BF16)
            k_ref[h + d] = (kh * lax.rsqrt(jnp.sum(kh * kh, axis=-1, keepdims=True) + EPS)).astype(BF16)
            v_ref[h + d] = v2[:, sl].astype(BF16)
            z_ref[h + d] = z2[:, sl].astype(BF16)


def _odd_in(hs, g, w_in, conv_w, a_log, dt_bias, tm):
    lp = hs.shape[0]
    w = DN_HEADS * DN_HEAD_DIM
    wqkv = w_in[:, :3 * w].astype(BF16)
    wz = w_in[:, 3 * w:4 * w].astype(BF16)
    wab = w_in[:, 4 * w:]
    wabc = jnp.pad(wab, ((0, 0), (0, LANES - 2 * DN_HEADS))).astype(BF16)
    wabr = wab.T.astype(BF16)
    pad_row = lambda a: jnp.pad(a.reshape(1, -1), ((0, 0), (0, LANES - DN_HEADS)))
    pad_col = lambda a: jnp.pad(a.reshape(-1, 1), ((0, DN_HEADS), (0, 0)))
    row_spec = lambda width: pl.BlockSpec((tm, width), lambda i: (i, 0))
    act = jax.ShapeDtypeStruct((DN_HEADS, lp, DN_HEAD_DIM), BF16)
    act_spec = pl.BlockSpec((DN_HEADS, tm, DN_HEAD_DIM), lambda i: (0, i, 0))
    return pl.pallas_call(
        functools.partial(_odd_in_kernel, tm=tm),
        grid=(lp // tm,),
        in_specs=[row_spec(D_MODEL), _const_spec((1, D_MODEL)), _const_spec(wqkv.shape), _const_spec(wz.shape),
                  _const_spec(wabc.shape), _const_spec(wabr.shape), _const_spec(conv_w.shape),
                  _const_spec((1, LANES)), _const_spec((1, LANES)),
                  _const_spec((2 * DN_HEADS, 1)), _const_spec((2 * DN_HEADS, 1))],
        out_specs=[act_spec, act_spec, act_spec, act_spec, row_spec(2 * DN_HEADS),
                   pl.BlockSpec((2 * DN_HEADS, tm), lambda i: (0, i))],
        out_shape=[act, act, act, act,
                   jax.ShapeDtypeStruct((lp, 2 * DN_HEADS), F32),
                   jax.ShapeDtypeStruct((2 * DN_HEADS, lp), F32)],
        scratch_shapes=[pltpu.VMEM((SUBLANES, 3 * w), F32)],
        compiler_params=_cparams("arbitrary"),
        name="odd_in",
    )(hs, g, wqkv, wz, wabc, wabr, conv_w, pad_row(a_log), pad_row(dt_bias), pad_col(a_log), pad_col(dt_bias))


def _bdot(a, b):
    return lax.dot_general(a, b, (((2,), (1,)), ((0,), (0,))), preferred_element_type=F32)


def _bdot_nt(a, b):
    return lax.dot_general(a, b, (((2,), (2,)), ((0,), (0,))), preferred_element_type=F32)


def _bdot_tn(a, b):
    return lax.dot_general(a, b, (((1,), (1,)), ((0,), (0,))), preferred_element_type=F32)


def _dn_kernel(q_ref, k_ref, v_ref, z_ref, gcol_ref, grow_ref, ng_ref, o_ref,
               s_ref, u_scr, wq_scr, a_scr, kt_scr, egl_scr, *, chunks):
    c64, nh, hd, grp = DN_CHUNK, DN_HEADS, DN_HEAD_DIM, DN_GROUP
    nb = nh * grp

    @pl.when(pl.program_id(0) == 0)
    def _():
        s_ref[...] = jnp.zeros_like(s_ref)

    ri = lax.broadcasted_iota(jnp.int32, (c64, c64), 0)
    ci = lax.broadcasted_iota(jnp.int32, (c64, c64), 1)
    incl = ri >= ci
    strict = ri > ci
    eye = jnp.where(ri == ci, 1.0, 0.0)
    pair_masks = []
    s = 1
    while s < c64:
        pair_masks.append(jnp.logical_and((ri // s) % 2 == 1, (ri // s) - 1 == ci // s))
        s *= 2
    ng = ng_ref[...]

    def prep(gi, carry):
        c0 = gi * grp
        rows = pl.ds(pl.multiple_of(c0 * c64, grp * c64), grp * c64)
        cs = pl.ds(c0, grp)
        qb = q_ref[:, rows, :].reshape(nb, c64, hd)
        kb = k_ref[:, rows, :].reshape(nb, c64, hd)
        vb = v_ref[:, rows, :].reshape(nb, c64, hd)
        kf, vf = kb.astype(F32), vb.astype(F32)
        gcb = gcol_ref[rows, :]
        gc = jnp.stack([gcb[:, h:h + 1] for h in range(nh)], axis=0).reshape(nb, c64, 1)
        be = jnp.stack([gcb[:, nh + h:nh + h + 1] for h in range(nh)], axis=0).reshape(nb, c64, 1)
        gr = grow_ref[0:nh, cs].reshape(nb, 1, c64)
        gl = gc[:, c64 - 1:c64, :]
        eg = jnp.exp(gc)
        decay = jnp.exp(jnp.where(incl, gc - gr, -jnp.inf))
        qkk = _bdot_nt(jnp.concatenate([qb, kb], axis=1), kb)
        n = jnp.where(strict, be * qkk[:, c64:] * decay, 0.0)
        x = eye - jnp.where(pair_masks[0], n, 0.0)
        for m in pair_masks[1:]:
            xb = x.astype(BF16)
            x = x - _bdot(xb, _bdot(jnp.where(m, n, 0.0).astype(BF16), xb).astype(BF16))
        rhs = jnp.concatenate([(vf * be).astype(BF16), (kf * (be * eg)).astype(BF16)], axis=2)
        uw = _bdot(x.astype(BF16), rhs)
        qg = (qb.astype(F32) * eg).astype(BF16)
        u_scr[:, cs] = uw[:, :, :hd].reshape(nh, grp, c64, hd)
        wq_scr[:, cs] = jnp.concatenate([uw[:, :, hd:].astype(BF16), qg], axis=1).reshape(nh, grp, 2 * c64, hd)
        a_scr[:, cs] = (qkk[:, :c64] * decay).astype(BF16).reshape(nh, grp, c64, c64)
        kt_scr[:, cs] = (kf * jnp.exp(gl - gc)).astype(BF16).reshape(nh, grp, c64, hd)
        egl_scr[:, cs] = jnp.broadcast_to(jnp.exp(gl), (nb, 1, hd)).reshape(nh, grp, 1, hd)
        return carry

    lax.fori_loop(0, chunks // grp, prep, 0)

    def step(c, carry):
        rows = pl.ds(pl.multiple_of(c * c64, c64), c64)
        s = s_ref[...]
        ws = _bdot(wq_scr[:, c], s.astype(BF16))
        vnb = (u_scr[:, c] - ws[:, :c64]).astype(BF16)
        o = ws[:, c64:] + _bdot(a_scr[:, c], vnb)
        s_ref[...] = s * egl_scr[:, c] + _bdot_tn(kt_scr[:, c], vnb)
        on = o * lax.rsqrt(jnp.mean(o * o, axis=-1, keepdims=True) + EPS) * ng
        o_ref[:, rows, :] = (on * _silu(z_ref[:, rows, :].astype(F32))).astype(BF16)
        return carry

    lax.fori_loop(0, chunks, step, 0)


def _dn(q, k, v, z, gcol, grow, norm_g, chunks):
    nh, lp, hd = q.shape
    rows = chunks * DN_CHUNK
    grow4 = grow.reshape(2 * nh, lp // DN_CHUNK, 1, DN_CHUNK)
    act_spec = pl.BlockSpec((nh, rows, hd), lambda i: (0, i, 0))
    return pl.pallas_call(
        functools.partial(_dn_kernel, chunks=chunks),
        grid=(lp // rows,),
        in_specs=[act_spec, act_spec, act_spec, act_spec,
                  pl.BlockSpec((rows, 2 * nh), lambda i: (i, 0)),
                  pl.BlockSpec((2 * nh, chunks, 1, DN_CHUNK), lambda i: (0, i, 0, 0)),
                  _const_spec((1, hd))],
        out_specs=act_spec,
        out_shape=jax.ShapeDtypeStruct((nh, lp, hd), BF16),
        scratch_shapes=[pltpu.VMEM((nh, hd, hd), F32),
                        pltpu.VMEM((nh, chunks, DN_CHUNK, hd), F32),
                        pltpu.VMEM((nh, chunks, 2 * DN_CHUNK, hd), BF16),
                        pltpu.VMEM((nh, chunks, DN_CHUNK, DN_CHUNK), BF16),
                        pltpu.VMEM((nh, chunks, DN_CHUNK, hd), BF16),
                        pltpu.VMEM((nh, chunks, 1, hd), F32)],
        compiler_params=_cparams("arbitrary"),
        name="dn",
    )(q, k, v, z, gcol, grow4, norm_g.reshape(1, -1))


def _odd_out_kernel(og_ref, hs_ref, w_ref, gpost_ref, gpre_mlp_ref, w1_ref, w2_ref, gpost_mlp_ref, o_ref):
    og = jnp.concatenate([og_ref[h] for h in range(DN_HEADS)], axis=1)
    hs = hs_ref[...] + _rms(_dot(og, w_ref[...]), gpost_ref[...])
    o_ref[...] = _mlp_block(hs, gpre_mlp_ref, w1_ref, w2_ref, gpost_mlp_ref)


def _odd_out(og, hs, w_out, g_post, g_pre_mlp, w1, w2, g_post_mlp, tm):
    lp = hs.shape[0]
    return pl.pallas_call(
        _odd_out_kernel,
        grid=(lp // tm,),
        in_specs=[pl.BlockSpec((DN_HEADS, tm, DN_HEAD_DIM), lambda i: (0, i, 0)),
                  pl.BlockSpec((tm, D_MODEL), lambda i: (i, 0)),
                  _resident(w_out.shape), _const_spec((1, D_MODEL)),
                  _const_spec((1, D_MODEL)), _resident(w1.shape), _resident(w2.shape), _const_spec((1, D_MODEL))],
        out_specs=pl.BlockSpec((tm, D_MODEL), lambda i: (i, 0)),
        out_shape=jax.ShapeDtypeStruct((lp, D_MODEL), F32),
        compiler_params=_cparams("parallel"),
        name="odd_out_mlp",
    )(og, hs, w_out, g_post, g_pre_mlp, w1, w2, g_post_mlp)


def kernel(x, meta_tokens, pre_mix_norm, post_mix_norm, pre_mlp_norm, post_mlp_norm, mlp_w1, mlp_w2, w_in_even, w_out_even, sb_out_norm, s5_lambda_re, s5_lambda_im, s5_log_dt, s5_b_re, s5_b_im, s5_c_re, s5_c_im, s5_d, s5_w_glu, s5_b_glu, s5_out_norm, w_in_odd, dn_conv_w, dn_a_log, dn_dt_bias, dn_out_norm, w_out_odd):
    assert x.shape[0] == 1 and x.shape[2] == D_MODEL
    depth = pre_mix_norm.shape[0]
    l = N_META + x.shape[1]
    lp = -(-l // SB_TILE) * SB_TILE
    ls = lp // STREAMS
    tr_even = _largest_div(ls, 104, 16)
    t_s5 = _largest_div(ls, 128, SUBLANES)
    tm_mlp = _largest_div(lp, 640, 16)
    dn_chunks = _largest_div(lp // DN_CHUNK, 10, DN_GROUP)
    row = lambda a: a.reshape(1, -1)

    hs = jnp.concatenate([meta_tokens.astype(x.dtype), x[0], jnp.zeros((lp - l, D_MODEL), x.dtype)], axis=0)
    for i in range(depth):
        j = i // 2
        mlp = (row(pre_mlp_norm[i]), mlp_w1[i].astype(BF16), mlp_w2[i].astype(BF16), row(post_mlp_norm[i]))
        if i % 2 == 0:
            qkv, u_il = _even_in(hs, row(pre_mix_norm[i]), w_in_even[j].astype(BF16), tr_even)
            o_sb = _sb_attention(qkv, SB_TILE)
            o_s5 = _s5(u_il, s5_lambda_re[j], s5_lambda_im[j], s5_log_dt[j],
                       s5_b_re[j], s5_b_im[j], s5_c_re[j], s5_c_im[j], s5_d[j], s5_w_glu[j], s5_b_glu[j],
                       s5_out_norm[j], t_s5)
            hs = _even_out(o_sb, o_s5.reshape(lp, S5_WIDTH), hs, row(sb_out_norm[j]),
                           w_out_even[j].astype(BF16), row(post_mix_norm[i]), *mlp, tm_mlp)
        else:
            q, k, v, z, gcol, grow = _odd_in(hs, row(pre_mix_norm[i]), w_in_odd[j], dn_conv_w[j], dn_a_log[j],
                                             dn_dt_bias[j], SB_TILE)
            og = _dn(q, k, v, z, gcol, grow, dn_out_norm[j], dn_chunks)
            hs = _odd_out(og, hs, w_out_odd[j].astype(BF16), row(post_mix_norm[i]), *mlp, tm_mlp)
    return hs[N_META:l][None]
```

```python
import functools

import jax
import jax.numpy as jnp
from jax import lax
from jax.experimental import pallas as pl
from jax.experimental.pallas import tpu as pltpu

F32 = jnp.float32
BF16 = jnp.bfloat16

D_MODEL = 1024
N_META = 16
SB_HEAD_DIM = 64
SB_WIDTH = 512
S5_WIDTH = 512
S5_GROUP = 16
S5_GROUPS = 32
S5_STATE = 64
DN_HEAD_DIM = 128
DN_HEADS = 8
DN_CONV = 4
D_FF = 4096
FF_TILE = 1024
EPS = 1e-6

LANES = 128
SUBLANES = 8
STREAMS = SUBLANES
OCTETS = S5_GROUPS // 8
OCT_STATES = 8 * S5_STATE
DN_CHUNK = 64
DN_GROUP = 4
SB_TILE = 256
SB_KEY_TILE = 128
UNDERFLOW_LOG = -87.5
VMEM_LIMIT = 56 * 1024 * 1024


def _largest_div(n, cap, mult):
    best = None
    for d in range(mult, cap + 1, mult):
        if n % d == 0:
            best = d
    assert best is not None, (n, cap, mult)
    return best


def _cparams(*sem):
    return pltpu.CompilerParams(dimension_semantics=sem, vmem_limit_bytes=VMEM_LIMIT)


def _rms(x, g):
    ms = jnp.mean(x * x, axis=-1, keepdims=True)
    return x * lax.rsqrt(ms + EPS) * g


def _softplus(x):
    return jnp.maximum(x, 0.0) + jnp.log1p(jnp.exp(-jnp.abs(x)))


def _sigmoid(x):
    return 1.0 / (1.0 + jnp.exp(-x))


def _silu(x):
    return x * _sigmoid(x)


def _split3(x):
    hi = x.astype(BF16)
    r1 = x - hi.astype(F32)
    mid = r1.astype(BF16)
    lo = (r1 - mid.astype(F32)).astype(BF16)
    return hi, mid, lo


def _dot(a, b):
    return jnp.dot(a, b, preferred_element_type=F32)


def _dot_nt(a, b):
    return lax.dot_general(a, b, (((1,), (1,)), ((), ())), preferred_element_type=F32)


def _dot_tn(a, b):
    return lax.dot_general(a, b, (((0,), (0,)), ((), ())), preferred_element_type=F32)


def _const_spec(shape):
    nd = len(shape)
    return pl.BlockSpec(shape, lambda *_: (0,) * nd)


def _even_in_kernel(hs_ref, g_ref, w_ref, qkv_ref, *u_refs, tr):
    x = hs_ref[...].reshape(STREAMS * tr, D_MODEL)
    hn = _rms(x, g_ref[...]).astype(BF16)
    p = _dot(hn, w_ref[...])
    q = p[:, :SB_WIDTH] * (SB_HEAD_DIM ** -0.5)
    qkv_ref[:, :, :SB_WIDTH] = q.reshape(STREAMS, tr, SB_WIDTH).astype(BF16)
    qkv_ref[:, :, SB_WIDTH:] = p[:, SB_WIDTH:3 * SB_WIDTH].reshape(STREAMS, tr, 2 * SB_WIDTH).astype(BF16)
    for o, u_ref in enumerate(u_refs):
        c0 = 3 * SB_WIDTH + o * LANES
        for r in range(STREAMS):
            u_ref[pl.ds(r, tr, stride=STREAMS), :] = p[r * tr:(r + 1) * tr, c0:c0 + LANES]


def _even_in(hs, g, w, tr):
    lp = hs.shape[0]
    ls = lp // STREAMS
    qkv, *us = pl.pallas_call(
        functools.partial(_even_in_kernel, tr=tr),
        grid=(ls // tr,),
        in_specs=[pl.BlockSpec((STREAMS, tr, D_MODEL), lambda i: (0, i, 0)),
                  _const_spec((1, D_MODEL)),
                  _const_spec(w.shape)],
        out_specs=[pl.BlockSpec((STREAMS, tr, 3 * SB_WIDTH), lambda i: (0, i, 0))]
                  + [pl.BlockSpec((STREAMS * tr, LANES), lambda i: (i, 0))] * OCTETS,
        out_shape=[jax.ShapeDtypeStruct((STREAMS, ls, 3 * SB_WIDTH), BF16)]
                  + [jax.ShapeDtypeStruct((lp, LANES), F32)] * OCTETS,
        compiler_params=_cparams("parallel"),
        name="even_in",
    )(hs.reshape(STREAMS, ls, D_MODEL), g, w)
    return qkv.reshape(lp, 3 * SB_WIDTH), us


def _sb_kernel(q_ref, k_ref, v_ref, o_ref, acc_ref, car_ref, *, tq):
    tk = SB_KEY_TILE
    npair = SB_WIDTH // LANES
    pairs = range(npair)
    i = pl.program_id(0)
    lane = lax.broadcasted_iota(jnp.int32, (1, LANES), 1)
    qs = []
    for p in pairs:
        q2 = q_ref[:, p * LANES:(p + 1) * LANES]
        zq = jnp.zeros_like(q2)
        qs.append((jnp.where(lane < SB_HEAD_DIM, q2, zq), jnp.where(lane >= SB_HEAD_DIM, q2, zq)))
    jr = lax.broadcasted_iota(jnp.int32, (tk, tk), 0)
    jc = lax.broadcasted_iota(jnp.int32, (tk, tk), 1)
    later = jnp.where(jr > jc, 1.0, 0.0).astype(BF16)
    acc_ref[...] = jnp.zeros_like(acc_ref)
    car_ref[...] = jnp.zeros_like(car_ref)

    def key_block(j, r0):
        rows = slice(r0, tq)
        nr = tq - r0
        off = pl.multiple_of(j * tk, tk)
        col = j * tk + lax.broadcasted_iota(jnp.int32, (1, tk), 1)
        row = i * tq + r0 + lax.broadcasted_iota(jnp.int32, (nr, 1), 0)
        valid = col < row
        valid = jnp.concatenate([valid, valid], axis=0)
        z = [_dot_nt(jnp.concatenate([qs[p][0][rows], qs[p][1][rows]], axis=0),
                     k_ref[pl.ds(off, tk), p * LANES:(p + 1) * LANES]) for p in pairs]
        sp = [jnp.maximum(z[p], 0.0) + jnp.log(1.0 + jnp.exp(-jnp.abs(z[p]))) for p in pairs]
        lk = [jnp.where(valid, -sp[p], 0.0) for p in pairs]
        hi = [lk[p].astype(BF16) for p in pairs]
        lo = [(lk[p] - hi[p].astype(F32)).astype(BF16) for p in pairs]
        rc = [_dot(jnp.concatenate([hi[p], lo[p]], axis=0), later) for p in pairs]
        car = [car_ref[p, :, rows].reshape(2 * nr, 1) for p in pairs]
        w = [jnp.where(valid, jnp.exp(z[p] - sp[p] + car[p] + rc[p][:2 * nr] + rc[p][2 * nr:]), 0.0).astype(BF16)
             for p in pairs]
        top = None
        for p in pairs:
            acc_ref[p, :, rows] += _dot(w[p], v_ref[pl.ds(off, tk), p * LANES:(p + 1) * LANES]).reshape(2, nr, LANES)
            ncar = car[p] + jnp.sum(lk[p], axis=1, keepdims=True)
            car_ref[p, :, rows] = ncar.reshape(2, nr, 1)
            mp = jnp.max(ncar)
            top = mp if top is None else jnp.maximum(top, mp)
        return top

    nd = tq // tk
    for d in range(nd - 1, 0, -1):
        key_block(i * nd + d, d * tk)

    def cond(c):
        j, go = c
        return jnp.logical_and(j >= 0, go > 0)

    def body(c):
        j, _ = c
        top = key_block(j, 0)
        return j - 1, (top > UNDERFLOW_LOG).astype(jnp.int32)

    lax.while_loop(cond, body, (i * nd, jnp.int32(1)))
    for p in pairs:
        o_ref[:, p * LANES:(p + 1) * LANES] = jnp.where(lane < SB_HEAD_DIM, acc_ref[p, 0], acc_ref[p, 1])


def _sb_attention(qkv, tq):
    lp = qkv.shape[0]
    npair = SB_WIDTH // LANES
    resident = lambda col: pl.BlockSpec((lp, SB_WIDTH), lambda i: (0, col), pipeline_mode=pl.Buffered(1))
    return pl.pallas_call(
        functools.partial(_sb_kernel, tq=tq),
        grid=(lp // tq,),
        in_specs=[pl.BlockSpec((tq, SB_WIDTH), lambda i: (i, 0)), resident(1), resident(2)],
        out_specs=pl.BlockSpec((tq, SB_WIDTH), lambda i: (i, 0)),
        out_shape=jax.ShapeDtypeStruct((lp, SB_WIDTH), F32),
        scratch_shapes=[pltpu.VMEM((npair, 2, tq, LANES), F32), pltpu.VMEM((npair, 2, tq, 1), F32)],
        compiler_params=_cparams("parallel"),
        name="sb_attn",
    )(qkv, qkv, qkv)


def _cmul(ar, ai, br, bi):
    return ar * br - ai * bi, ar * bi + ai * br


def _s5_prep_kernel(lre_ref, lim_ref, ldt_ref, tab_ref, *, ls):
    lr = jnp.minimum(lre_ref[...], -1e-4)
    li = lim_ref[...]
    dt = jnp.exp(ldt_ref[...])
    mag = jnp.exp(lr * dt)
    ang = li * dt
    ar, ai = mag * jnp.cos(ang), mag * jnp.sin(ang)
    den = lr * lr + li * li
    nr, ni = ar - 1.0, ai
    cr = (nr * lr + ni * li) / den
    ci = (ni * lr - nr * li) / den
    pr, pi = jnp.ones_like(ar), jnp.zeros_like(ar)
    br, bi = ar, ai
    n = ls
    while n:
        if n & 1:
            pr, pi = _cmul(pr, pi, br, bi)
        n >>= 1
        if n:
            br, bi = _cmul(br, bi, br, bi)
    tab_ref[...] = jnp.concatenate(
        [ar, ai, cr, ci, pr, pi, jnp.zeros_like(ar), jnp.zeros_like(ar)], axis=0)


def _s5_prep(lam_re, lam_im, log_dt, ls):
    n = S5_GROUPS * S5_STATE
    row = lambda a: a.reshape(1, n)
    ldt = jnp.broadcast_to(log_dt[:, None], (S5_GROUPS, S5_STATE))
    return pl.pallas_call(
        functools.partial(_s5_prep_kernel, ls=ls),
        out_shape=jax.ShapeDtypeStruct((SUBLANES, n), F32),
        name="s5_prep",
    )(row(lam_re), row(lam_im), row(ldt))


def _s5_bbar(tab_ref, bre_ref, bim_ref, o):
    sl = slice(o * OCT_STATES, (o + 1) * OCT_STATES)
    cr, ci = tab_ref[2:3, sl], tab_ref[3:4, sl]
    bre, bim = bre_ref[o], bim_ref[o]
    return (cr * bre - ci * bim).astype(BF16), (cr * bim + ci * bre).astype(BF16)


def _s5_scan_octet(tab_ref, o, bur_ref, bui_ref, xr, xi, t_steps, hist=None):
    sl = slice(o * OCT_STATES, (o + 1) * OCT_STATES)
    ar = jnp.broadcast_to(tab_ref[0:1, sl], (STREAMS, OCT_STATES))
    ai = jnp.broadcast_to(tab_ref[1:2, sl], (STREAMS, OCT_STATES))

    for t in range(t_steps):
        rows = slice(t * STREAMS, (t + 1) * STREAMS)
        xr, xi = ar * xr - ai * xi + bur_ref[o, rows, :], ar * xi + ai * xr + bui_ref[o, rows, :]
        if hist is not None:
            hist[0][o, rows, :] = xr
            hist[1][o, rows, :] = xi
    return xr, xi


def _s5_pass_a_kernel(u0_ref, u1_ref, u2_ref, u3_ref, tab_ref, bre_ref, bim_ref, x0_ref,
                      xr_ref, xi_ref, bur_ref, bui_ref, *, t_steps):
    u_refs = (u0_ref, u1_ref, u2_ref, u3_ref)
    i = pl.program_id(0)

    @pl.when(i == 0)
    def _():
        xr_ref[...] = jnp.zeros_like(xr_ref)
        xi_ref[...] = jnp.zeros_like(xi_ref)

    for o in range(OCTETS):
        ub = u_refs[o][...].astype(BF16)
        bbr, bbi = _s5_bbar(tab_ref, bre_ref, bim_ref, o)
        bur_ref[o] = _dot(ub, bbr)
        bui_ref[o] = _dot(ub, bbi)
        xr, xi = _s5_scan_octet(tab_ref, o, bur_ref, bui_ref, xr_ref[o], xi_ref[o], t_steps)
        xr_ref[o] = xr
        xi_ref[o] = xi

    @pl.when(i == pl.num_programs(0) - 1)
    def _():
        sub = lax.broadcasted_iota(jnp.int32, (STREAMS, OCT_STATES), 0)
        for o in range(OCTETS):
            sl = slice(o * OCT_STATES, (o + 1) * OCT_STATES)
            pr, pi = tab_ref[4:5, sl], tab_ref[5:6, sl]
            er, ei = xr_ref[o], xi_ref[o]
            x0r = jnp.zeros((STREAMS, OCT_STATES), F32)
            x0i = jnp.zeros((STREAMS, OCT_STATES), F32)
            cr = jnp.zeros((1, OCT_STATES), F32)
            ci = jnp.zeros((1, OCT_STATES), F32)
            for r in range(1, STREAMS):
                mr, mi = _cmul(pr, pi, cr, ci)
                cr, ci = er[r - 1:r] + mr, ei[r - 1:r] + mi
                x0r = jnp.where(sub == r, cr, x0r)
                x0i = jnp.where(sub == r, ci, x0i)
            x0_ref[:, 2 * o * OCT_STATES:(2 * o + 1) * OCT_STATES] = x0r
            x0_ref[:, (2 * o + 1) * OCT_STATES:(2 * o + 2) * OCT_STATES] = x0i


def _gelu_tanh(x):
    return 0.5 * x * (1.0 + jnp.tanh(0.7978845608028654 * (x + 0.044715 * (x * x * x))))


def _s5_pass_b_kernel(u0_ref, u1_ref, u2_ref, u3_ref, tab_ref, bre_ref, bim_ref, x0_ref, cre_ref, cim_ref, d_ref,
                      wglu_ref, bglu_ref, g_ref, o_ref, xr_ref, xi_ref, bur_ref, bui_ref, hr_ref, hi_ref,
                      y0_ref, y1_ref, y2_ref, y3_ref, *, t_steps):
    u_refs = (u0_ref, u1_ref, u2_ref, u3_ref)
    y_refs = (y0_ref, y1_ref, y2_ref, y3_ref)
    i = pl.program_id(0)

    @pl.when(i == 0)
    def _():
        for o in range(OCTETS):
            xr_ref[o] = x0_ref[:, 2 * o * OCT_STATES:(2 * o + 1) * OCT_STATES]
            xi_ref[o] = x0_ref[:, (2 * o + 1) * OCT_STATES:(2 * o + 2) * OCT_STATES]

    ys = []
    for o in range(OCTETS):
        u = u_refs[o][...]
        bbr, bbi = _s5_bbar(tab_ref, bre_ref, bim_ref, o)
        bur_ref[o] = _dot(u.astype(BF16), bbr)
        bui_ref[o] = _dot(u.astype(BF16), bbi)
        xr, xi = _s5_scan_octet(tab_ref, o, bur_ref, bui_ref, xr_ref[o], xi_ref[o], t_steps,
                                hist=(hr_ref, hi_ref))
        xr_ref[o] = xr
        xi_ref[o] = xi
        ys.append(_dot(hr_ref[o].astype(BF16), cre_ref[o]) - _dot(hi_ref[o].astype(BF16), cim_ref[o])
                  + d_ref[:, o * LANES:(o + 1) * LANES] * u)

    hact = _gelu_tanh(jnp.concatenate(ys, axis=1))
    gate = _sigmoid(_dot(hact.astype(BF16), wglu_ref[...]) + bglu_ref[...])
    res = _rms(hact * gate, g_ref[...])
    for o in range(OCTETS):
        y_refs[o][...] = res[:, o * LANES:(o + 1) * LANES]
        for r in range(STREAMS):
            o_ref[r, :, o * LANES:(o + 1) * LANES] = y_refs[o][pl.ds(r, t_steps, stride=STREAMS), :]


def _s5_block_diag_b(b):
    bt = b.reshape(OCTETS, 8, S5_STATE, S5_GROUP).transpose(0, 1, 3, 2)
    eye = jnp.eye(8, dtype=b.dtype)
    return jnp.einsum('ogpn,gh->ogphn', bt, eye).reshape(OCTETS, 8 * S5_GROUP, OCT_STATES)


def _s5_block_diag_c(c):
    ct = c.reshape(OCTETS, 8, S5_GROUP, S5_STATE).transpose(0, 1, 3, 2)
    eye = jnp.eye(8, dtype=c.dtype)
    return jnp.einsum('ognp,gh->ognhp', ct, eye).reshape(OCTETS, OCT_STATES, 8 * S5_GROUP)


def _s5(us, lam_re, lam_im, log_dt, b_re, b_im, c_re, c_im, d_skip, w_glu, b_glu, norm_g, t_steps):
    rows = us[0].shape[0]
    ls = rows // STREAMS
    tile = STREAMS * t_steps
    nsteps = ls // t_steps
    tab = _s5_prep(lam_re, lam_im, log_dt, ls)
    bre, bim = _s5_block_diag_b(b_re), _s5_block_diag_b(b_im)
    cre, cim = _s5_block_diag_c(c_re).astype(BF16), _s5_block_diag_c(c_im).astype(BF16)
    nstate = S5_GROUPS * S5_STATE
    state_scratch = [pltpu.VMEM((OCTETS, STREAMS, OCT_STATES), F32)] * 2
    bu_scratch = [pltpu.VMEM((OCTETS, tile, OCT_STATES), F32)] * 2
    u_specs = [pl.BlockSpec((tile, LANES), lambda i: (i, 0))] * OCTETS

    x0 = pl.pallas_call(
        functools.partial(_s5_pass_a_kernel, t_steps=t_steps),
        grid=(nsteps,),
        in_specs=u_specs + [_const_spec(tab.shape), _const_spec(bre.shape), _const_spec(bim.shape)],
        out_specs=_const_spec((STREAMS, 2 * nstate)),
        out_shape=jax.ShapeDtypeStruct((STREAMS, 2 * nstate), F32),
        scratch_shapes=state_scratch + bu_scratch,
        compiler_params=_cparams("arbitrary"),
        name="s5_pass_a",
    )(*us, tab, bre, bim)

    return pl.pallas_call(
        functools.partial(_s5_pass_b_kernel, t_steps=t_steps),
        grid=(nsteps,),
        in_specs=u_specs + [_const_spec(tab.shape), _const_spec(bre.shape), _const_spec(bim.shape),
                            _const_spec(x0.shape), _const_spec(cre.shape), _const_spec(cim.shape),
                            _const_spec((1, S5_WIDTH)), _const_spec(w_glu.shape), _const_spec((1, S5_WIDTH)),
                            _const_spec((1, S5_WIDTH))],
        out_specs=pl.BlockSpec((STREAMS, t_steps, S5_WIDTH), lambda i: (0, i, 0)),
        out_shape=jax.ShapeDtypeStruct((STREAMS, ls, S5_WIDTH), F32),
        scratch_shapes=state_scratch + bu_scratch + bu_scratch + [pltpu.VMEM((tile, LANES), F32)] * OCTETS,
        compiler_params=_cparams("arbitrary"),
        name="s5_pass_b",
    )(*us, tab, bre, bim, x0, cre, cim, d_skip.reshape(1, -1), w_glu.astype(BF16), b_glu.reshape(1, -1),
      norm_g.reshape(1, -1))


def _mlp_block(hs, gpre_ref, w1_ref, w2_ref, gpost_ref):
    hn = _rms(hs, gpre_ref[...]).astype(BF16)
    acc = jnp.zeros(hs.shape, F32)
    for f in range(D_FF // FF_TILE):
        a = jnp.maximum(_dot(hn, w1_ref[:, f * FF_TILE:(f + 1) * FF_TILE]), 0.0)
        acc = acc + _dot((a * a).astype(BF16), w2_ref[f * FF_TILE:(f + 1) * FF_TILE, :])
    return hs + _rms(acc, gpost_ref[...])


def _resident(shape):
    nd = len(shape)
    return pl.BlockSpec(shape, lambda *_: (0,) * nd, pipeline_mode=pl.Buffered(1))


def _even_out_kernel(osb_ref, os5_ref, hs_ref, gsb_ref, wsb_ref, ws5_ref, gpost_ref,
                     gpre_mlp_ref, w1_ref, w2_ref, gpost_mlp_ref, o_ref):
    sb = _rms(osb_ref[...], gsb_ref[...]).astype(BF16)
    mix = _dot(sb, wsb_ref[...]) + _dot(os5_ref[...].astype(BF16), ws5_ref[...])
    hs = hs_ref[...] + _rms(mix, gpost_ref[...])
    o_ref[...] = _mlp_block(hs, gpre_mlp_ref, w1_ref, w2_ref, gpost_mlp_ref)


def _row_tiles(lp, keep):
    if keep is None:
        tm = _largest_div(lp, 640, 16)
        return tm, lp // tm, None
    first, count = keep
    tm = _largest_div(count, 640, 16)
    return tm, count // tm, first


def _rows_spec(shape, tm, first, row_axis=0):
    nd = len(shape)
    if first is None:
        block = tuple(tm if a == row_axis else s for a, s in enumerate(shape))
        return pl.BlockSpec(block, lambda i: tuple(i if a == row_axis else 0 for a in range(nd)))
    assert first % 16 == 0 and tm % 16 == 0
    block = tuple(pl.Element(tm if a == row_axis else s) for a, s in enumerate(shape))
    return pl.BlockSpec(block, lambda i: tuple(pl.multiple_of(first + i * tm, 16) if a == row_axis else 0
                                               for a in range(nd)))


def _even_out(o_sb, o_s5, hs, g_sb, w_out, g_post, g_pre_mlp, w1, w2, g_post_mlp, keep=None):
    lp = hs.shape[0]
    w_sb, w_s5 = w_out[:SB_WIDTH], w_out[SB_WIDTH:]
    tm, steps, first = _row_tiles(lp, keep)
    return pl.pallas_call(
        _even_out_kernel,
        grid=(steps,),
        in_specs=[_rows_spec(o_sb.shape, tm, first), _rows_spec(o_s5.shape, tm, first), _rows_spec(hs.shape, tm, first),
                  _const_spec((1, SB_WIDTH)), _resident(w_sb.shape), _resident(w_s5.shape),
                  _const_spec((1, D_MODEL)),
                  _const_spec((1, D_MODEL)), _resident(w1.shape), _resident(w2.shape), _const_spec((1, D_MODEL))],
        out_specs=pl.BlockSpec((tm, D_MODEL), lambda i: (i, 0)),
        out_shape=jax.ShapeDtypeStruct((steps * tm, D_MODEL), F32),
        compiler_params=_cparams("parallel"),
        name="even_out_mlp",
    )(o_sb, o_s5, hs, g_sb, w_sb, w_s5, g_post, g_pre_mlp, w1, w2, g_post_mlp)


def _chunk_tri(n, transposed):
    r = lax.broadcasted_iota(jnp.int32, (n, n), 0)
    c = lax.broadcasted_iota(jnp.int32, (n, n), 1)
    same = (r // DN_CHUNK) == (c // DN_CHUNK)
    order = (r <= c) if transposed else (r >= c)
    return jnp.where(jnp.logical_and(same, order), 1.0, 0.0).astype(BF16)


def _odd_in_kernel(hs_ref, g_ref, wqkv_ref, wz_ref, wabc_ref, wabr_ref, conv_ref, alc_ref, dtc_ref, alr_ref, dtr_ref,
                   q_ref, k_ref, v_ref, z_ref, gcol_ref, grow_ref, carry_ref, *, tm):
    @pl.when(pl.program_id(0) == 0)
    def _():
        carry_ref[...] = jnp.zeros_like(carry_ref)

    hn = _rms(hs_ref[...], g_ref[...]).astype(BF16)

    ab = _dot(hn, wabc_ref[...])
    lane = lax.broadcasted_iota(jnp.int32, (1, LANES), 1)
    gb = jnp.where(lane < DN_HEADS, -jnp.exp(alc_ref[...]) * _softplus(ab + dtc_ref[...]), _sigmoid(ab))
    tri = _chunk_tri(tm, transposed=False)
    hi, mid, lo = _split3(gb)
    gsum = _dot(tri, hi) + _dot(tri, mid) + _dot(tri, lo)
    gcol_ref[...] = jnp.where(lane < DN_HEADS, gsum, gb)[:, :2 * DN_HEADS]

    abr = _dot_nt(wabr_ref[...], hn)
    sub = lax.broadcasted_iota(jnp.int32, (2 * DN_HEADS, 1), 0)
    gbr = jnp.where(sub < DN_HEADS, -jnp.exp(alr_ref[...]) * _softplus(abr + dtr_ref[...]), _sigmoid(abr))
    trit = _chunk_tri(tm, transposed=True)
    hi, mid, lo = _split3(gbr)
    gsumr = _dot(hi, trit) + _dot(mid, trit) + _dot(lo, trit)
    grow_ref[...] = jnp.where(sub < DN_HEADS, gsumr, gbr)

    first = lax.broadcasted_iota(jnp.int32, (SUBLANES, 1), 0)
    w = DN_HEADS * DN_HEAD_DIM
    hd = DN_HEAD_DIM

    def conv_silu(c0):
        cols = slice(c0, c0 + 2 * hd)
        x = _dot(hn, wqkv_ref[:, cols])
        prev = carry_ref[:, cols]
        carry_ref[:, cols] = x[tm - SUBLANES:, :]
        y = x * conv_ref[DN_CONV - 1:DN_CONV, cols]
        for s in range(1, DN_CONV):
            sh = pltpu.roll(x, s, 0)
            head = jnp.where(first < s, pltpu.roll(prev, s, 0), sh[:SUBLANES])
            sh = jnp.concatenate([head, sh[SUBLANES:]], axis=0)
            y = y + sh * conv_ref[DN_CONV - 1 - s:DN_CONV - s, cols]
        return _silu(y)

    for h in range(0, DN_HEADS, 2):
        q2 = conv_silu(h * hd)
        k2 = conv_silu(w + h * hd)
        v2 = conv_silu(2 * w + h * hd)
        z2 = _dot(hn, wz_ref[:, h * hd:(h + 2) * hd])
        for d in range(2):
            sl = slice(d * hd, (d + 1) * hd)
            qh, kh = q2[:, sl], k2[:, sl]
            q_ref[h + d] = (qh * (lax.rsqrt(jnp.sum(qh * qh, axis=-1, keepdims=True) + EPS)
                                  * (hd ** -0.5))).astype(BF16)
            k_ref[h + d] = (kh * lax.rsqrt(jnp.sum(kh * kh, axis=-1, keepdims=True) + EPS)).astype(BF16)
            v_ref[h + d] = v2[:, sl].astype(BF16)
            z_ref[h + d] = z2[:, sl].astype(BF16)


def _odd_in(hs, g, w_in, conv_w, a_log, dt_bias, tm):
    lp = hs.shape[0]
    w = DN_HEADS * DN_HEAD_DIM
    wqkv = w_in[:, :3 * w].astype(BF16)
    wz = w_in[:, 3 * w:4 * w].astype(BF16)
    wab = w_in[:, 4 * w:]
    wabc = jnp.pad(wab, ((0, 0), (0, LANES - 2 * DN_HEADS))).astype(BF16)
    wabr = wab.T.astype(BF16)
    pad_row = lambda a: jnp.pad(a.reshape(1, -1), ((0, 0), (0, LANES - DN_HEADS)))
    pad_col = lambda a: jnp.pad(a.reshape(-1, 1), ((0, DN_HEADS), (0, 0)))
    row_spec = lambda width: pl.BlockSpec((tm, width), lambda i: (i, 0))
    act = jax.ShapeDtypeStruct((DN_HEADS, lp, DN_HEAD_DIM), BF16)
    act_spec = pl.BlockSpec((DN_HEADS, tm, DN_HEAD_DIM), lambda i: (0, i, 0))
    return pl.pallas_call(
        functools.partial(_odd_in_kernel, tm=tm),
        grid=(lp // tm,),
        in_specs=[row_spec(D_MODEL), _const_spec((1, D_MODEL)), _const_spec(wqkv.shape), _const_spec(wz.shape),
                  _const_spec(wabc.shape), _const_spec(wabr.shape), _const_spec(conv_w.shape),
                  _const_spec((1, LANES)), _const_spec((1, LANES)),
                  _const_spec((2 * DN_HEADS, 1)), _const_spec((2 * DN_HEADS, 1))],
        out_specs=[act_spec, act_spec, act_spec, act_spec, row_spec(2 * DN_HEADS),
                   pl.BlockSpec((2 * DN_HEADS, tm), lambda i: (0, i))],
        out_shape=[act, act, act, act,
                   jax.ShapeDtypeStruct((lp, 2 * DN_HEADS), F32),
                   jax.ShapeDtypeStruct((2 * DN_HEADS, lp), F32)],
        scratch_shapes=[pltpu.VMEM((SUBLANES, 3 * w), F32)],
        compiler_params=_cparams("arbitrary"),
        name="odd_in",
    )(hs, g, wqkv, wz, wabc, wabr, conv_w, pad_row(a_log), pad_row(dt_bias), pad_col(a_log), pad_col(dt_bias))


def _bdot(a, b):
    return lax.dot_general(a, b, (((2,), (1,)), ((0,), (0,))), preferred_element_type=F32)


def _bdot_nt(a, b):
    return lax.dot_general(a, b, (((2,), (2,)), ((0,), (0,))), preferred_element_type=F32)


def _bdot_tn(a, b):
    return lax.dot_general(a, b, (((1,), (1,)), ((0,), (0,))), preferred_element_type=F32)


def _dn_kernel(q_ref, k_ref, v_ref, z_ref, gcol_ref, grow_ref, ng_ref, o_ref,
               s_ref, u_scr, wq_scr, a_scr, kt_scr, egl_scr, *, chunks):
    c64, nh, hd, grp = DN_CHUNK, DN_HEADS, DN_HEAD_DIM, DN_GROUP
    nb = nh * grp

    @pl.when(pl.program_id(0) == 0)
    def _():
        s_ref[...] = jnp.zeros_like(s_ref)

    ri = lax.broadcasted_iota(jnp.int32, (c64, c64), 0)
    ci = lax.broadcasted_iota(jnp.int32, (c64, c64), 1)
    incl = ri >= ci
    strict = ri > ci
    eye = jnp.where(ri == ci, 1.0, 0.0)
    pair_masks = []
    s = 1
    while s < c64:
        pair_masks.append(jnp.logical_and((ri // s) % 2 == 1, (ri // s) - 1 == ci // s))
        s *= 2
    ng = ng_ref[...]

    def prep(gi, carry):
        c0 = gi * grp
        rows = pl.ds(pl.multiple_of(c0 * c64, grp * c64), grp * c64)
        cs = pl.ds(c0, grp)
        qb = q_ref[:, rows, :].reshape(nb, c64, hd)
        kb = k_ref[:, rows, :].reshape(nb, c64, hd)
        vb = v_ref[:, rows, :].reshape(nb, c64, hd)
        kf, vf = kb.astype(F32), vb.astype(F32)
        gcb = gcol_ref[rows, :]
        gc = jnp.stack([gcb[:, h:h + 1] for h in range(nh)], axis=0).reshape(nb, c64, 1)
        be = jnp.stack([gcb[:, nh + h:nh + h + 1] for h in range(nh)], axis=0).reshape(nb, c64, 1)
        gr = grow_ref[0:nh, cs].reshape(nb, 1, c64)
        gl = gc[:, c64 - 1:c64, :]
        eg = jnp.exp(gc)
        decay = jnp.exp(jnp.where(incl, gc - gr, -jnp.inf))
        qkk = _bdot_nt(jnp.concatenate([qb, kb], axis=1), kb)
        n = jnp.where(strict, be * qkk[:, c64:] * decay, 0.0)
        x = eye - jnp.where(pair_masks[0], n, 0.0)
        for m in pair_masks[1:]:
            xb = x.astype(BF16)
            x = x - _bdot(xb, _bdot(jnp.where(m, n, 0.0).astype(BF16), xb).astype(BF16))
        rhs = jnp.concatenate([(vf * be).astype(BF16), (kf * (be * eg)).astype(BF16)], axis=2)
        uw = _bdot(x.astype(BF16), rhs)
        qg = (qb.astype(F32) * eg).astype(BF16)
        u_scr[:, cs] = uw[:, :, :hd].reshape(nh, grp, c64, hd)
        wq_scr[:, cs] = jnp.concatenate([uw[:, :, hd:].astype(BF16), qg], axis=1).reshape(nh, grp, 2 * c64, hd)
        a_scr[:, cs] = (qkk[:, :c64] * decay).astype(BF16).reshape(nh, grp, c64, c64)
        kt_scr[:, cs] = (kf * jnp.exp(gl - gc)).astype(BF16).reshape(nh, grp, c64, hd)
        egl_scr[:, cs] = jnp.broadcast_to(jnp.exp(gl), (nb, 1, hd)).reshape(nh, grp, 1, hd)
        return carry

    lax.fori_loop(0, chunks // grp, prep, 0)

    def step(c, carry):
        rows = pl.ds(pl.multiple_of(c * c64, c64), c64)
        s = s_ref[...]
        ws = _bdot(wq_scr[:, c], s.astype(BF16))
        vnb = (u_scr[:, c] - ws[:, :c64]).astype(BF16)
        o = ws[:, c64:] + _bdot(a_scr[:, c], vnb)
        s_ref[...] = s * egl_scr[:, c] + _bdot_tn(kt_scr[:, c], vnb)
        on = o * lax.rsqrt(jnp.mean(o * o, axis=-1, keepdims=True) + EPS) * ng
        o_ref[:, rows, :] = (on * _silu(z_ref[:, rows, :].astype(F32))).astype(BF16)
        return carry

    lax.fori_loop(0, chunks, step, 0)


def _dn(q, k, v, z, gcol, grow, norm_g, chunks):
    nh, lp, hd = q.shape
    rows = chunks * DN_CHUNK
    grow4 = grow.reshape(2 * nh, lp // DN_CHUNK, 1, DN_CHUNK)
    act_spec = pl.BlockSpec((nh, rows, hd), lambda i: (0, i, 0))
    return pl.pallas_call(
        functools.partial(_dn_kernel, chunks=chunks),
        grid=(lp // rows,),
        in_specs=[act_spec, act_spec, act_spec, act_spec,
                  pl.BlockSpec((rows, 2 * nh), lambda i: (i, 0)),
                  pl.BlockSpec((2 * nh, chunks, 1, DN_CHUNK), lambda i: (0, i, 0, 0)),
                  _const_spec((1, hd))],
        out_specs=act_spec,
        out_shape=jax.ShapeDtypeStruct((nh, lp, hd), BF16),
        scratch_shapes=[pltpu.VMEM((nh, hd, hd), F32),
                        pltpu.VMEM((nh, chunks, DN_CHUNK, hd), F32),
                        pltpu.VMEM((nh, chunks, 2 * DN_CHUNK, hd), BF16),
                        pltpu.VMEM((nh, chunks, DN_CHUNK, DN_CHUNK), BF16),
                        pltpu.VMEM((nh, chunks, DN_CHUNK, hd), BF16),
                        pltpu.VMEM((nh, chunks, 1, hd), F32)],
        compiler_params=_cparams("arbitrary"),
        name="dn",
    )(q, k, v, z, gcol, grow4, norm_g.reshape(1, -1))


def _odd_out_kernel(og_ref, hs_ref, w_ref, gpost_ref, gpre_mlp_ref, w1_ref, w2_ref, gpost_mlp_ref, o_ref):
    og = jnp.concatenate([og_ref[h] for h in range(DN_HEADS)], axis=1)
    hs = hs_ref[...] + _rms(_dot(og, w_ref[...]), gpost_ref[...])
    o_ref[...] = _mlp_block(hs, gpre_mlp_ref, w1_ref, w2_ref, gpost_mlp_ref)


def _odd_out(og, hs, w_out, g_post, g_pre_mlp, w1, w2, g_post_mlp, keep=None):
    lp = hs.shape[0]
    tm, steps, first = _row_tiles(lp, keep)
    return pl.pallas_call(
        _odd_out_kernel,
        grid=(steps,),
        in_specs=[_rows_spec(og.shape, tm, first, row_axis=1), _rows_spec(hs.shape, tm, first),
                  _resident(w_out.shape), _const_spec((1, D_MODEL)),
                  _const_spec((1, D_MODEL)), _resident(w1.shape), _resident(w2.shape), _const_spec((1, D_MODEL))],
        out_specs=pl.BlockSpec((tm, D_MODEL), lambda i: (i, 0)),
        out_shape=jax.ShapeDtypeStruct((steps * tm, D_MODEL), F32),
        compiler_params=_cparams("parallel"),
        name="odd_out_mlp",
    )(og, hs, w_out, g_post, g_pre_mlp, w1, w2, g_post_mlp)


def kernel(x, meta_tokens, pre_mix_norm, post_mix_norm, pre_mlp_norm, post_mlp_norm, mlp_w1, mlp_w2, w_in_even, w_out_even, sb_out_norm, s5_lambda_re, s5_lambda_im, s5_log_dt, s5_b_re, s5_b_im, s5_c_re, s5_c_im, s5_d, s5_w_glu, s5_b_glu, s5_out_norm, w_in_odd, dn_conv_w, dn_a_log, dn_dt_bias, dn_out_norm, w_out_odd):
    assert x.shape[0] == 1 and x.shape[2] == D_MODEL
    depth = pre_mix_norm.shape[0]
    l = N_META + x.shape[1]
    lp = -(-l // SB_TILE) * SB_TILE
    ls = lp // STREAMS
    tr_even = _largest_div(ls, 104, 16)
    t_s5 = _largest_div(ls, 128, SUBLANES)
    dn_chunks = _largest_div(lp // DN_CHUNK, 10, DN_GROUP)
    row = lambda a: a.reshape(1, -1)

    hs = jnp.concatenate([meta_tokens.astype(x.dtype), x[0], jnp.zeros((lp - l, D_MODEL), x.dtype)], axis=0)
    for i in range(depth):
        j = i // 2
        mlp = (row(pre_mlp_norm[i]), mlp_w1[i].astype(BF16), mlp_w2[i].astype(BF16), row(post_mlp_norm[i]))
        keep = (N_META, l - N_META) if i == depth - 1 else None
        if i % 2 == 0:
            qkv, u_il = _even_in(hs, row(pre_mix_norm[i]), w_in_even[j].astype(BF16), tr_even)
            o_sb = _sb_attention(qkv, SB_TILE)
            o_s5 = _s5(u_il, s5_lambda_re[j], s5_lambda_im[j], s5_log_dt[j],
                       s5_b_re[j], s5_b_im[j], s5_c_re[j], s5_c_im[j], s5_d[j], s5_w_glu[j], s5_b_glu[j],
                       s5_out_norm[j], t_s5)
            hs = _even_out(o_sb, o_s5.reshape(lp, S5_WIDTH), hs, row(sb_out_norm[j]), w_out_even[j].astype(BF16),
                           row(post_mix_norm[i]), *mlp, keep)
        else:
            q, k, v, z, gcol, grow = _odd_in(hs, row(pre_mix_norm[i]), w_in_odd[j], dn_conv_w[j], dn_a_log[j],
                                             dn_dt_bias[j], SB_TILE)
            og = _dn(q, k, v, z, gcol, grow, dn_out_norm[j], dn_chunks)
            hs = _odd_out(og, hs, w_out_odd[j].astype(BF16), row(post_mix_norm[i]), *mlp, keep)
    return hs[None]
```

```python
import functools

import jax
import jax.numpy as jnp
from jax import lax
from jax.experimental import pallas as pl
from jax.experimental.pallas import tpu as pltpu

F32 = jnp.float32
BF16 = jnp.bfloat16

D_MODEL = 1024
N_META = 16
SB_HEAD_DIM = 64
SB_WIDTH = 512
S5_WIDTH = 512
S5_GROUP = 16
S5_GROUPS = 32
S5_STATE = 64
DN_HEAD_DIM = 128
DN_HEADS = 8
DN_CONV = 4
D_FF = 4096
FF_TILE = 1024
EPS = 1e-6

LANES = 128
SUBLANES = 8
STREAMS = SUBLANES
OCTETS = S5_GROUPS // 8
OCT_STATES = 8 * S5_STATE
DN_CHUNK = 128
DN_GROUP = 2
SB_TILE = 256
SB_KEY_TILE = 128
UNDERFLOW_LOG = -87.5
VMEM_LIMIT = 56 * 1024 * 1024


def _largest_div(n, cap, mult):
    best = None
    for d in range(mult, cap + 1, mult):
        if n % d == 0:
            best = d
    assert best is not None, (n, cap, mult)
    return best


def _cparams(*sem):
    return pltpu.CompilerParams(dimension_semantics=sem, vmem_limit_bytes=VMEM_LIMIT)


def _rms(x, g):
    ms = jnp.mean(x * x, axis=-1, keepdims=True)
    return x * lax.rsqrt(ms + EPS) * g


def _softplus(x):
    return jnp.maximum(x, 0.0) + jnp.log1p(jnp.exp(-jnp.abs(x)))


def _sigmoid(x):
    return 1.0 / (1.0 + jnp.exp(-x))


def _silu(x):
    return x * _sigmoid(x)


def _split3(x):
    hi = x.astype(BF16)
    r1 = x - hi.astype(F32)
    mid = r1.astype(BF16)
    lo = (r1 - mid.astype(F32)).astype(BF16)
    return hi, mid, lo


def _dot(a, b):
    return jnp.dot(a, b, preferred_element_type=F32)


def _dot_nt(a, b):
    return lax.dot_general(a, b, (((1,), (1,)), ((), ())), preferred_element_type=F32)


def _dot_tn(a, b):
    return lax.dot_general(a, b, (((0,), (0,)), ((), ())), preferred_element_type=F32)


def _const_spec(shape):
    nd = len(shape)
    return pl.BlockSpec(shape, lambda *_: (0,) * nd)


def _x_window_spec(seq, a0, tile):
    assert a0 % 16 == 0 and tile % 16 == 0 and seq % 16 == 0 and N_META % 16 == 0
    return pl.BlockSpec(
        (pl.Element(tile), pl.Element(D_MODEL)),
        lambda i: (pl.multiple_of(jnp.clip(a0 + i * tile - N_META, 0, seq - tile), 16), 0))


def _seq_tile(xb, meta_ref, i, a0, tile, steps, seq):
    l = N_META + seq
    out = xb
    if a0 < N_META:
        assert a0 == 0 and tile > N_META
        head = jnp.concatenate([meta_ref[...].astype(xb.dtype), xb[:tile - N_META]], axis=0)
        out = jnp.where(i == 0, head, out)
    n_full = min(steps, max(0, (l - a0) // tile))
    rem = min(tile, max(0, l - a0 - n_full * tile))
    pad_from = n_full
    if n_full < steps and rem:
        assert n_full > 0 or a0 >= N_META
        cut = jnp.concatenate([xb[tile - rem:], jnp.zeros((tile - rem, D_MODEL), xb.dtype)], axis=0)
        out = jnp.where(i == n_full, cut, out)
        pad_from += 1
    if pad_from < steps:
        out = jnp.where(i >= pad_from, jnp.zeros_like(out), out)
    return out


def _even_in_kernel(*refs, tr, ls, seq):
    if seq is None:
        hs_ref, g_ref, w_ref, qkv_ref, *u_refs = refs
        x = hs_ref[...].reshape(STREAMS * tr, D_MODEL)
    else:
        x_refs, (meta_ref, g_ref, w_ref, qkv_ref, *u_refs) = refs[:STREAMS], refs[STREAMS:]
        x = jnp.concatenate([_seq_tile(x_refs[r][...], meta_ref, pl.program_id(0), r * ls, tr, ls // tr, seq)
                             for r in range(STREAMS)], axis=0)
    hn = _rms(x, g_ref[...]).astype(BF16)
    p = _dot(hn, w_ref[...])
    q = p[:, :SB_WIDTH] * (SB_HEAD_DIM ** -0.5)
    qkv_ref[:, :, :SB_WIDTH] = q.reshape(STREAMS, tr, SB_WIDTH).astype(BF16)
    qkv_ref[:, :, SB_WIDTH:] = p[:, SB_WIDTH:3 * SB_WIDTH].reshape(STREAMS, tr, 2 * SB_WIDTH).astype(BF16)
    for o, u_ref in enumerate(u_refs):
        c0 = 3 * SB_WIDTH + o * LANES
        for r in range(STREAMS):
            u_ref[pl.ds(r, tr, stride=STREAMS), :] = p[r * tr:(r + 1) * tr, c0:c0 + LANES]


def _even_in(hs, g, w, tr, lp):
    ls = lp // STREAMS
    if isinstance(hs, tuple):
        x, meta = hs
        seq = x.shape[0]
        acts = [x] * STREAMS + [meta]
        act_specs = [_x_window_spec(seq, r * ls, tr) for r in range(STREAMS)] + [_const_spec(meta.shape)]
    else:
        seq = None
        acts = [hs.reshape(STREAMS, ls, D_MODEL)]
        act_specs = [pl.BlockSpec((STREAMS, tr, D_MODEL), lambda i: (0, i, 0))]
    qkv, *us = pl.pallas_call(
        functools.partial(_even_in_kernel, tr=tr, ls=ls, seq=seq),
        grid=(ls // tr,),
        in_specs=act_specs + [_const_spec((1, D_MODEL)), _const_spec(w.shape)],
        out_specs=[pl.BlockSpec((STREAMS, tr, 3 * SB_WIDTH), lambda i: (0, i, 0))]
                  + [pl.BlockSpec((STREAMS * tr, LANES), lambda i: (i, 0))] * OCTETS,
        out_shape=[jax.ShapeDtypeStruct((STREAMS, ls, 3 * SB_WIDTH), BF16)]
                  + [jax.ShapeDtypeStruct((lp, LANES), F32)] * OCTETS,
        compiler_params=_cparams("parallel"),
        name="even_in",
    )(*acts, g, w)
    return qkv.reshape(lp, 3 * SB_WIDTH), us


def _sb_kernel(q_ref, k_ref, v_ref, o_ref, acc_ref, car_ref, *, tq):
    tk = SB_KEY_TILE
    npair = SB_WIDTH // LANES
    pairs = range(npair)
    i = pl.program_id(0)
    lane = lax.broadcasted_iota(jnp.int32, (1, LANES), 1)
    qs = []
    for p in pairs:
        q2 = q_ref[:, p * LANES:(p + 1) * LANES]
        zq = jnp.zeros_like(q2)
        qs.append((jnp.where(lane < SB_HEAD_DIM, q2, zq), jnp.where(lane >= SB_HEAD_DIM, q2, zq)))
    jr = lax.broadcasted_iota(jnp.int32, (tk, tk), 0)
    jc = lax.broadcasted_iota(jnp.int32, (tk, tk), 1)
    later = jnp.where(jr > jc, 1.0, 0.0).astype(BF16)
    acc_ref[...] = jnp.zeros_like(acc_ref)
    car_ref[...] = jnp.zeros_like(car_ref)

    def key_block(j, r0, r1):
        rows = slice(r0, r1)
        nr = r1 - r0
        off = pl.multiple_of(j * tk, tk)
        col = j * tk + lax.broadcasted_iota(jnp.int32, (1, tk), 1)
        row = i * tq + r0 + lax.broadcasted_iota(jnp.int32, (nr, 1), 0)
        valid = col < row
        valid = jnp.concatenate([valid, valid], axis=0)
        z = [_dot_nt(jnp.concatenate([qs[p][0][rows], qs[p][1][rows]], axis=0),
                     k_ref[pl.ds(off, tk), p * LANES:(p + 1) * LANES]) for p in pairs]
        sp = [jnp.maximum(z[p], 0.0) + jnp.log(1.0 + jnp.exp(-jnp.abs(z[p]))) for p in pairs]
        lk = [jnp.where(valid, -sp[p], 0.0) for p in pairs]
        hi = [lk[p].astype(BF16) for p in pairs]
        lo = [(lk[p] - hi[p].astype(F32)).astype(BF16) for p in pairs]
        rc = [_dot(jnp.concatenate([hi[p], lo[p]], axis=0), later) for p in pairs]
        car = [car_ref[p, :, rows].reshape(2 * nr, 1) for p in pairs]
        w = [jnp.where(valid, jnp.exp(z[p] - sp[p] + car[p] + rc[p][:2 * nr] + rc[p][2 * nr:]), 0.0).astype(BF16)
             for p in pairs]
        for p in pairs:
            acc_ref[p, :, rows] += _dot(w[p], v_ref[pl.ds(off, tk), p * LANES:(p + 1) * LANES]).reshape(2, nr, LANES)
            car_ref[p, :, rows] = (car[p] + jnp.sum(lk[p], axis=1, keepdims=True)).reshape(2, nr, 1)

    def band_live(r1):
        return (jnp.max(car_ref[:, :, r1 - tk:r1]) > UNDERFLOW_LOG).astype(jnp.int32)

    nd = tq // tk
    for d in range(nd - 1, 0, -1):
        key_block(i * nd + d, d * tk, tq)

    j = i * nd
    for r1 in range(tq, 0, -tk):
        def body(c, r1=r1):
            key_block(c[0], 0, r1)
            return c[0] - 1, band_live(r1)

        j, _ = lax.while_loop(lambda c: jnp.logical_and(c[0] >= 0, c[1] > 0), body, (j, band_live(r1)))
    for p in pairs:
        o_ref[:, p * LANES:(p + 1) * LANES] = jnp.where(lane < SB_HEAD_DIM, acc_ref[p, 0], acc_ref[p, 1])


def _sb_attention(qkv, tq):
    lp = qkv.shape[0]
    npair = SB_WIDTH // LANES
    resident = lambda col: pl.BlockSpec((lp, SB_WIDTH), lambda i: (0, col), pipeline_mode=pl.Buffered(1))
    return pl.pallas_call(
        functools.partial(_sb_kernel, tq=tq),
        grid=(lp // tq,),
        in_specs=[pl.BlockSpec((tq, SB_WIDTH), lambda i: (i, 0)), resident(1), resident(2)],
        out_specs=pl.BlockSpec((tq, SB_WIDTH), lambda i: (i, 0)),
        out_shape=jax.ShapeDtypeStruct((lp, SB_WIDTH), F32),
        scratch_shapes=[pltpu.VMEM((npair, 2, tq, LANES), F32), pltpu.VMEM((npair, 2, tq, 1), F32)],
        compiler_params=_cparams("parallel"),
        name="sb_attn",
    )(qkv, qkv, qkv)


def _cmul(ar, ai, br, bi):
    return ar * br - ai * bi, ar * bi + ai * br


def _s5_prep_kernel(lre_ref, lim_ref, ldt_ref, tab_ref, *, ls):
    lr = jnp.minimum(lre_ref[...], -1e-4)
    li = lim_ref[...]
    dt = jnp.exp(ldt_ref[...])
    mag = jnp.exp(lr * dt)
    ang = li * dt
    ar, ai = mag * jnp.cos(ang), mag * jnp.sin(ang)
    den = lr * lr + li * li
    nr, ni = ar - 1.0, ai
    cr = (nr * lr + ni * li) / den
    ci = (ni * lr - nr * li) / den
    pr, pi = jnp.ones_like(ar), jnp.zeros_like(ar)
    br, bi = ar, ai
    n = ls
    while n:
        if n & 1:
            pr, pi = _cmul(pr, pi, br, bi)
        n >>= 1
        if n:
            br, bi = _cmul(br, bi, br, bi)
    tab_ref[...] = jnp.concatenate(
        [ar, ai, cr, ci, pr, pi, jnp.zeros_like(ar), jnp.zeros_like(ar)], axis=0)


def _s5_prep(lam_re, lam_im, log_dt, ls):
    n = S5_GROUPS * S5_STATE
    row = lambda a: a.reshape(1, n)
    ldt = jnp.broadcast_to(log_dt[:, None], (S5_GROUPS, S5_STATE))
    return pl.pallas_call(
        functools.partial(_s5_prep_kernel, ls=ls),
        out_shape=jax.ShapeDtypeStruct((SUBLANES, n), F32),
        name="s5_prep",
    )(row(lam_re), row(lam_im), row(ldt))


def _s5_bbar(tab_ref, bre_ref, bim_ref, o):
    sl = slice(o * OCT_STATES, (o + 1) * OCT_STATES)
    cr, ci = tab_ref[2:3, sl], tab_ref[3:4, sl]
    bre, bim = bre_ref[o], bim_ref[o]
    return (cr * bre - ci * bim).astype(BF16), (cr * bim + ci * bre).astype(BF16)


def _s5_scan_octet(tab_ref, o, bur_ref, bui_ref, xr, xi, t_steps, hist=None):
    sl = slice(o * OCT_STATES, (o + 1) * OCT_STATES)
    ar = jnp.broadcast_to(tab_ref[0:1, sl], (STREAMS, OCT_STATES))
    ai = jnp.broadcast_to(tab_ref[1:2, sl], (STREAMS, OCT_STATES))

    for t in range(t_steps):
        rows = slice(t * STREAMS, (t + 1) * STREAMS)
        xr, xi = ar * xr - ai * xi + bur_ref[o, rows, :], ar * xi + ai * xr + bui_ref[o, rows, :]
        if hist is not None:
            hist[0][o, rows, :] = xr
            hist[1][o, rows, :] = xi
    return xr, xi


def _s5_pass_a_kernel(u0_ref, u1_ref, u2_ref, u3_ref, tab_ref, bre_ref, bim_ref, x0_ref,
                      xr_ref, xi_ref, bur_ref, bui_ref, *, t_steps):
    u_refs = (u0_ref, u1_ref, u2_ref, u3_ref)
    i = pl.program_id(0)

    @pl.when(i == 0)
    def _():
        xr_ref[...] = jnp.zeros_like(xr_ref)
        xi_ref[...] = jnp.zeros_like(xi_ref)

    for o in range(OCTETS):
        ub = u_refs[o][...].astype(BF16)
        bbr, bbi = _s5_bbar(tab_ref, bre_ref, bim_ref, o)
        bur_ref[o] = _dot(ub, bbr)
        bui_ref[o] = _dot(ub, bbi)
        xr, xi = _s5_scan_octet(tab_ref, o, bur_ref, bui_ref, xr_ref[o], xi_ref[o], t_steps)
        xr_ref[o] = xr
        xi_ref[o] = xi

    @pl.when(i == pl.num_programs(0) - 1)
    def _():
        sub = lax.broadcasted_iota(jnp.int32, (STREAMS, OCT_STATES), 0)
        for o in range(OCTETS):
            sl = slice(o * OCT_STATES, (o + 1) * OCT_STATES)
            pr, pi = tab_ref[4:5, sl], tab_ref[5:6, sl]
            er, ei = xr_ref[o], xi_ref[o]
            x0r = jnp.zeros((STREAMS, OCT_STATES), F32)
            x0i = jnp.zeros((STREAMS, OCT_STATES), F32)
            cr = jnp.zeros((1, OCT_STATES), F32)
            ci = jnp.zeros((1, OCT_STATES), F32)
            for r in range(1, STREAMS):
                mr, mi = _cmul(pr, pi, cr, ci)
                cr, ci = er[r - 1:r] + mr, ei[r - 1:r] + mi
                x0r = jnp.where(sub == r, cr, x0r)
                x0i = jnp.where(sub == r, ci, x0i)
            x0_ref[:, 2 * o * OCT_STATES:(2 * o + 1) * OCT_STATES] = x0r
            x0_ref[:, (2 * o + 1) * OCT_STATES:(2 * o + 2) * OCT_STATES] = x0i


def _gelu_tanh(x):
    return 0.5 * x * (1.0 + jnp.tanh(0.7978845608028654 * (x + 0.044715 * (x * x * x))))


def _s5_pass_b_kernel(u0_ref, u1_ref, u2_ref, u3_ref, tab_ref, bre_ref, bim_ref, x0_ref, cre_ref, cim_ref, d_ref,
                      wglu_ref, bglu_ref, g_ref, o_ref, xr_ref, xi_ref, bur_ref, bui_ref, hr_ref, hi_ref,
                      y0_ref, y1_ref, y2_ref, y3_ref, *, t_steps):
    u_refs = (u0_ref, u1_ref, u2_ref, u3_ref)
    y_refs = (y0_ref, y1_ref, y2_ref, y3_ref)
    i = pl.program_id(0)

    @pl.when(i == 0)
    def _():
        for o in range(OCTETS):
            xr_ref[o] = x0_ref[:, 2 * o * OCT_STATES:(2 * o + 1) * OCT_STATES]
            xi_ref[o] = x0_ref[:, (2 * o + 1) * OCT_STATES:(2 * o + 2) * OCT_STATES]

    ys = []
    for o in range(OCTETS):
        u = u_refs[o][...]
        bbr, bbi = _s5_bbar(tab_ref, bre_ref, bim_ref, o)
        bur_ref[o] = _dot(u.astype(BF16), bbr)
        bui_ref[o] = _dot(u.astype(BF16), bbi)
        xr, xi = _s5_scan_octet(tab_ref, o, bur_ref, bui_ref, xr_ref[o], xi_ref[o], t_steps,
                                hist=(hr_ref, hi_ref))
        xr_ref[o] = xr
        xi_ref[o] = xi
        ys.append(_dot(hr_ref[o].astype(BF16), cre_ref[o]) - _dot(hi_ref[o].astype(BF16), cim_ref[o])
                  + d_ref[:, o * LANES:(o + 1) * LANES] * u)

    hact = _gelu_tanh(jnp.concatenate(ys, axis=1))
    gate = _sigmoid(_dot(hact.astype(BF16), wglu_ref[...]) + bglu_ref[...])
    res = _rms(hact * gate, g_ref[...])
    for o in range(OCTETS):
        y_refs[o][...] = res[:, o * LANES:(o + 1) * LANES]
        for r in range(STREAMS):
            o_ref[r, :, o * LANES:(o + 1) * LANES] = y_refs[o][pl.ds(r, t_steps, stride=STREAMS), :]


def _s5_block_diag_b(b):
    bt = b.reshape(OCTETS, 8, S5_STATE, S5_GROUP).transpose(0, 1, 3, 2)
    eye = jnp.eye(8, dtype=b.dtype)
    return jnp.einsum('ogpn,gh->ogphn', bt, eye).reshape(OCTETS, 8 * S5_GROUP, OCT_STATES)


def _s5_block_diag_c(c):
    ct = c.reshape(OCTETS, 8, S5_GROUP, S5_STATE).transpose(0, 1, 3, 2)
    eye = jnp.eye(8, dtype=c.dtype)
    return jnp.einsum('ognp,gh->ognhp', ct, eye).reshape(OCTETS, OCT_STATES, 8 * S5_GROUP)


def _s5(us, lam_re, lam_im, log_dt, b_re, b_im, c_re, c_im, d_skip, w_glu, b_glu, norm_g, t_steps):
    rows = us[0].shape[0]
    ls = rows // STREAMS
    tile = STREAMS * t_steps
    nsteps = ls // t_steps
    tab = _s5_prep(lam_re, lam_im, log_dt, ls)
    bre, bim = _s5_block_diag_b(b_re), _s5_block_diag_b(b_im)
    cre, cim = _s5_block_diag_c(c_re).astype(BF16), _s5_block_diag_c(c_im).astype(BF16)
    nstate = S5_GROUPS * S5_STATE
    state_scratch = [pltpu.VMEM((OCTETS, STREAMS, OCT_STATES), F32)] * 2
    bu_scratch = [pltpu.VMEM((OCTETS, tile, OCT_STATES), F32)] * 2
    u_specs = [pl.BlockSpec((tile, LANES), lambda i: (i, 0))] * OCTETS

    x0 = pl.pallas_call(
        functools.partial(_s5_pass_a_kernel, t_steps=t_steps),
        grid=(nsteps,),
        in_specs=u_specs + [_const_spec(tab.shape), _const_spec(bre.shape), _const_spec(bim.shape)],
        out_specs=_const_spec((STREAMS, 2 * nstate)),
        out_shape=jax.ShapeDtypeStruct((STREAMS, 2 * nstate), F32),
        scratch_shapes=state_scratch + bu_scratch,
        compiler_params=_cparams("arbitrary"),
        name="s5_pass_a",
    )(*us, tab, bre, bim)

    return pl.pallas_call(
        functools.partial(_s5_pass_b_kernel, t_steps=t_steps),
        grid=(nsteps,),
        in_specs=u_specs + [_const_spec(tab.shape), _const_spec(bre.shape), _const_spec(bim.shape),
                            _const_spec(x0.shape), _const_spec(cre.shape), _const_spec(cim.shape),
                            _const_spec((1, S5_WIDTH)), _const_spec(w_glu.shape), _const_spec((1, S5_WIDTH)),
                            _const_spec((1, S5_WIDTH))],
        out_specs=pl.BlockSpec((STREAMS, t_steps, S5_WIDTH), lambda i: (0, i, 0)),
        out_shape=jax.ShapeDtypeStruct((STREAMS, ls, S5_WIDTH), F32),
        scratch_shapes=state_scratch + bu_scratch + bu_scratch + [pltpu.VMEM((tile, LANES), F32)] * OCTETS,
        compiler_params=_cparams("arbitrary"),
        name="s5_pass_b",
    )(*us, tab, bre, bim, x0, cre, cim, d_skip.reshape(1, -1), w_glu.astype(BF16), b_glu.reshape(1, -1),
      norm_g.reshape(1, -1))


def _mlp_block(hs, gpre_ref, w1_ref, w2_ref, gpost_ref):
    hn = _rms(hs, gpre_ref[...]).astype(BF16)
    acc = jnp.zeros(hs.shape, F32)
    for f in range(D_FF // FF_TILE):
        a = jnp.maximum(_dot(hn, w1_ref[:, f * FF_TILE:(f + 1) * FF_TILE]), 0.0)
        acc = acc + _dot((a * a).astype(BF16), w2_ref[f * FF_TILE:(f + 1) * FF_TILE, :])
    return hs + _rms(acc, gpost_ref[...])


def _resident(shape):
    nd = len(shape)
    return pl.BlockSpec(shape, lambda *_: (0,) * nd, pipeline_mode=pl.Buffered(1))


def _even_out_kernel(osb_ref, os5_ref, *refs, seq_tile):
    if seq_tile is None:
        hs_ref, *refs = refs
        hs = hs_ref[...]
    else:
        x_ref, meta_ref, *refs = refs
        hs = _seq_tile(x_ref[...], meta_ref, pl.program_id(0), *seq_tile)
    gsb_ref, wsb_ref, ws5_ref, gpost_ref, gpre_mlp_ref, w1_ref, w2_ref, gpost_mlp_ref, o_ref = refs
    sb = _rms(osb_ref[...], gsb_ref[...]).astype(BF16)
    mix = _dot(sb, wsb_ref[...]) + _dot(os5_ref[...].astype(BF16), ws5_ref[...])
    hs = hs + _rms(mix, gpost_ref[...])
    o_ref[...] = _mlp_block(hs, gpre_mlp_ref, w1_ref, w2_ref, gpost_mlp_ref)


def _row_tiles(lp, keep):
    if keep is None:
        tm = _largest_div(lp, 640, 16)
        return tm, lp // tm, None
    first, count = keep
    tm = _largest_div(count, 640, 16)
    return tm, count // tm, first


def _rows_spec(shape, tm, first, row_axis=0):
    nd = len(shape)
    if first is None:
        block = tuple(tm if a == row_axis else s for a, s in enumerate(shape))
        return pl.BlockSpec(block, lambda i: tuple(i if a == row_axis else 0 for a in range(nd)))
    assert first % 16 == 0 and tm % 16 == 0
    block = tuple(pl.Element(tm if a == row_axis else s) for a, s in enumerate(shape))
    return pl.BlockSpec(block, lambda i: tuple(pl.multiple_of(first + i * tm, 16) if a == row_axis else 0
                                               for a in range(nd)))


def _even_out(o_sb, o_s5, hs, g_sb, w_out, g_post, g_pre_mlp, w1, w2, g_post_mlp, keep=None):
    lp = o_sb.shape[0]
    w_sb, w_s5 = w_out[:SB_WIDTH], w_out[SB_WIDTH:]
    tm, steps, first = _row_tiles(lp, keep)
    if isinstance(hs, tuple):
        x, meta = hs
        seq_tile = (first or 0, tm, steps, x.shape[0])
        res, res_specs = [x, meta], [_x_window_spec(x.shape[0], first or 0, tm), _const_spec(meta.shape)]
    else:
        seq_tile = None
        res, res_specs = [hs], [_rows_spec(hs.shape, tm, first)]
    return pl.pallas_call(
        functools.partial(_even_out_kernel, seq_tile=seq_tile),
        grid=(steps,),
        in_specs=[_rows_spec(o_sb.shape, tm, first), _rows_spec(o_s5.shape, tm, first), *res_specs,
                  _const_spec((1, SB_WIDTH)), _resident(w_sb.shape), _resident(w_s5.shape),
                  _const_spec((1, D_MODEL)),
                  _const_spec((1, D_MODEL)), _resident(w1.shape), _resident(w2.shape), _const_spec((1, D_MODEL))],
        out_specs=pl.BlockSpec((tm, D_MODEL), lambda i: (i, 0)),
        out_shape=jax.ShapeDtypeStruct((steps * tm, D_MODEL), F32),
        compiler_params=_cparams("parallel"),
        name="even_out_mlp",
    )(o_sb, o_s5, *res, g_sb, w_sb, w_s5, g_post, g_pre_mlp, w1, w2, g_post_mlp)


def _chunk_tri(n, transposed):
    r = lax.broadcasted_iota(jnp.int32, (n, n), 0)
    c = lax.broadcasted_iota(jnp.int32, (n, n), 1)
    same = (r // DN_CHUNK) == (c // DN_CHUNK)
    order = (r <= c) if transposed else (r >= c)
    return jnp.where(jnp.logical_and(same, order), 1.0, 0.0).astype(BF16)


def _odd_in_kernel(hs_ref, g_ref, wqkv_ref, wz_ref, wabc_ref, wabr_ref, conv_ref, alc_ref, dtc_ref, alr_ref, dtr_ref,
                   q_ref, k_ref, v_ref, z_ref, gcol_ref, grow_ref, carry_ref, *, tm):
    @pl.when(pl.program_id(0) == 0)
    def _():
        carry_ref[...] = jnp.zeros_like(carry_ref)

    hn = _rms(hs_ref[...], g_ref[...]).astype(BF16)

    ab = _dot(hn, wabc_ref[...])
    lane = lax.broadcasted_iota(jnp.int32, (1, LANES), 1)
    gb = jnp.where(lane < DN_HEADS, -jnp.exp(alc_ref[...]) * _softplus(ab + dtc_ref[...]), _sigmoid(ab))
    tri = _chunk_tri(tm, transposed=False)
    hi, mid, lo = _split3(gb)
    gsum = _dot(tri, hi) + _dot(tri, mid) + _dot(tri, lo)
    gcol_ref[...] = jnp.where(lane < DN_HEADS, gsum, gb)[:, :2 * DN_HEADS]

    abr = _dot_nt(wabr_ref[...], hn)
    sub = lax.broadcasted_iota(jnp.int32, (2 * DN_HEADS, 1), 0)
    gbr = jnp.where(sub < DN_HEADS, -jnp.exp(alr_ref[...]) * _softplus(abr + dtr_ref[...]), _sigmoid(abr))
    trit = _chunk_tri(tm, transposed=True)
    hi, mid, lo = _split3(gbr)
    gsumr = _dot(hi, trit) + _dot(mid, trit) + _dot(lo, trit)
    grow_ref[...] = jnp.where(sub < DN_HEADS, gsumr, gbr)

    first = lax.broadcasted_iota(jnp.int32, (SUBLANES, 1), 0)
    w = DN_HEADS * DN_HEAD_DIM
    hd = DN_HEAD_DIM

    def conv_silu(c0):
        cols = slice(c0, c0 + 2 * hd)
        x = _dot(hn, wqkv_ref[:, cols])
        prev = carry_ref[:, cols]
        carry_ref[:, cols] = x[tm - SUBLANES:, :]
        y = x * conv_ref[DN_CONV - 1:DN_CONV, cols]
        for s in range(1, DN_CONV):
            sh = pltpu.roll(x, s, 0)
            head = jnp.where(first < s, pltpu.roll(prev, s, 0), sh[:SUBLANES])
            sh = jnp.concatenate([head, sh[SUBLANES:]], axis=0)
            y = y + sh * conv_ref[DN_CONV - 1 - s:DN_CONV - s, cols]
        return _silu(y)

    for h in range(0, DN_HEADS, 2):
        q2 = conv_silu(h * hd)
        k2 = conv_silu(w + h * hd)
        v2 = conv_silu(2 * w + h * hd)
        z2 = _dot(hn, wz_ref[:, h * hd:(h + 2) * hd])
        for d in range(2):
            sl = slice(d * hd, (d + 1) * hd)
            qh, kh = q2[:, sl], k2[:, sl]
            q_ref[h + d] = (qh * (lax.rsqrt(jnp.sum(qh * qh, axis=-1, keepdims=True) + EPS)
                                  * (hd ** -0.5))).astype(BF16)
            k_ref[h + d] = (kh * lax.rsqrt(jnp.sum(kh * kh, axis=-1, keepdims=True) + EPS)).astype(BF16)
            v_ref[h + d] = v2[:, sl].astype(BF16)
            z_ref[h + d] = z2[:, sl].astype(BF16)


def _odd_in(hs, g, w_in, conv_w, a_log, dt_bias, tm):
    lp = hs.shape[0]
    w = DN_HEADS * DN_HEAD_DIM
    wqkv = w_in[:, :3 * w].astype(BF16)
    wz = w_in[:, 3 * w:4 * w].astype(BF16)
    wab = w_in[:, 4 * w:]
    wabc = jnp.pad(wab, ((0, 0), (0, LANES - 2 * DN_HEADS))).astype(BF16)
    wabr = wab.T.astype(BF16)
    pad_row = lambda a: jnp.pad(a.reshape(1, -1), ((0, 0), (0, LANES - DN_HEADS)))
    pad_col = lambda a: jnp.pad(a.reshape(-1, 1), ((0, DN_HEADS), (0, 0)))
    row_spec = lambda width: pl.BlockSpec((tm, width), lambda i: (i, 0))
    act = jax.ShapeDtypeStruct((DN_HEADS, lp, DN_HEAD_DIM), BF16)
    act_spec = pl.BlockSpec((DN_HEADS, tm, DN_HEAD_DIM), lambda i: (0, i, 0))
    return pl.pallas_call(
        functools.partial(_odd_in_kernel, tm=tm),
        grid=(lp // tm,),
        in_specs=[row_spec(D_MODEL), _const_spec((1, D_MODEL)), _const_spec(wqkv.shape), _const_spec(wz.shape),
                  _const_spec(wabc.shape), _const_spec(wabr.shape), _const_spec(conv_w.shape),
                  _const_spec((1, LANES)), _const_spec((1, LANES)),
                  _const_spec((2 * DN_HEADS, 1)), _const_spec((2 * DN_HEADS, 1))],
        out_specs=[act_spec, act_spec, act_spec, act_spec, row_spec(2 * DN_HEADS),
                   pl.BlockSpec((2 * DN_HEADS, tm), lambda i: (0, i))],
        out_shape=[act, act, act, act,
                   jax.ShapeDtypeStruct((lp, 2 * DN_HEADS), F32),
                   jax.ShapeDtypeStruct((2 * DN_HEADS, lp), F32)],
        scratch_shapes=[pltpu.VMEM((SUBLANES, 3 * w), F32)],
        compiler_params=_cparams("arbitrary"),
        name="odd_in",
    )(hs, g, wqkv, wz, wabc, wabr, conv_w, pad_row(a_log), pad_row(dt_bias), pad_col(a_log), pad_col(dt_bias))


def _bdot(a, b):
    return lax.dot_general(a, b, (((2,), (1,)), ((0,), (0,))), preferred_element_type=F32)


def _bdot_nt(a, b):
    return lax.dot_general(a, b, (((2,), (2,)), ((0,), (0,))), preferred_element_type=F32)


def _bdot_tn(a, b):
    return lax.dot_general(a, b, (((1,), (1,)), ((0,), (0,))), preferred_element_type=F32)


def _dn_kernel(q_ref, k_ref, v_ref, z_ref, gcol_ref, grow_ref, ng_ref, o_ref,
               s_ref, u_scr, wq_scr, a_scr, kt_scr, egl_scr, *, chunks):
    c64, nh, hd, grp = DN_CHUNK, DN_HEADS, DN_HEAD_DIM, DN_GROUP
    nb = nh * grp

    @pl.when(pl.program_id(0) == 0)
    def _():
        s_ref[...] = jnp.zeros_like(s_ref)

    ri = lax.broadcasted_iota(jnp.int32, (c64, c64), 0)
    ci = lax.broadcasted_iota(jnp.int32, (c64, c64), 1)
    incl = ri >= ci
    strict = ri > ci
    eye = jnp.where(ri == ci, 1.0, 0.0)
    pair_masks = []
    s = 1
    while s < c64:
        pair_masks.append(jnp.logical_and((ri // s) % 2 == 1, (ri // s) - 1 == ci // s))
        s *= 2
    ng = ng_ref[...]

    def prep(gi, carry):
        c0 = gi * grp
        rows = pl.ds(pl.multiple_of(c0 * c64, grp * c64), grp * c64)
        cs = pl.ds(c0, grp)
        qb = q_ref[:, rows, :].reshape(nb, c64, hd)
        kb = k_ref[:, rows, :].reshape(nb, c64, hd)
        vb = v_ref[:, rows, :].reshape(nb, c64, hd)
        kf, vf = kb.astype(F32), vb.astype(F32)
        gcb = gcol_ref[rows, :]
        gc = jnp.stack([gcb[:, h:h + 1] for h in range(nh)], axis=0).reshape(nb, c64, 1)
        be = jnp.stack([gcb[:, nh + h:nh + h + 1] for h in range(nh)], axis=0).reshape(nb, c64, 1)
        gr = grow_ref[0:nh, cs].reshape(nb, 1, c64)
        gl = gc[:, c64 - 1:c64, :]
        eg = jnp.exp(gc)
        decay = jnp.exp(jnp.where(incl, gc - gr, -jnp.inf))
        qkk = _bdot_nt(jnp.concatenate([qb, kb], axis=1), kb)
        n = jnp.where(strict, be * qkk[:, c64:] * decay, 0.0)
        x = eye - jnp.where(pair_masks[0], n, 0.0)
        for m in pair_masks[1:]:
            xb = x.astype(BF16)
            x = x - _bdot(xb, _bdot(jnp.where(m, n, 0.0).astype(BF16), xb).astype(BF16))
        rhs = jnp.concatenate([(vf * be).astype(BF16), (kf * (be * eg)).astype(BF16)], axis=2)
        uw = _bdot(x.astype(BF16), rhs)
        qg = (qb.astype(F32) * eg).astype(BF16)
        u_scr[:, cs] = uw[:, :, :hd].reshape(nh, grp, c64, hd)
        wq_scr[:, cs] = jnp.concatenate([uw[:, :, hd:].astype(BF16), qg], axis=1).reshape(nh, grp, 2 * c64, hd)
        a_scr[:, cs] = (qkk[:, :c64] * decay).astype(BF16).reshape(nh, grp, c64, c64)
        kt_scr[:, cs] = (kf * jnp.exp(gl - gc)).astype(BF16).reshape(nh, grp, c64, hd)
        egl_scr[:, cs] = jnp.broadcast_to(jnp.exp(gl), (nb, 1, hd)).reshape(nh, grp, 1, hd)
        return carry

    lax.fori_loop(0, chunks // grp, prep, 0)

    def step(c, carry):
        rows = pl.ds(pl.multiple_of(c * c64, c64), c64)
        s = s_ref[...]
        ws = _bdot(wq_scr[:, c], s.astype(BF16))
        vnb = (u_scr[:, c] - ws[:, :c64]).astype(BF16)
        o = ws[:, c64:] + _bdot(a_scr[:, c], vnb)
        s_ref[...] = s * egl_scr[:, c] + _bdot_tn(kt_scr[:, c], vnb)
        on = o * lax.rsqrt(jnp.mean(o * o, axis=-1, keepdims=True) + EPS) * ng
        o_ref[:, rows, :] = (on * _silu(z_ref[:, rows, :].astype(F32))).astype(BF16)
        return carry

    lax.fori_loop(0, chunks, step, 0)


def _dn(q, k, v, z, gcol, grow, norm_g, chunks):
    nh, lp, hd = q.shape
    rows = chunks * DN_CHUNK
    grow4 = grow.reshape(2 * nh, lp // DN_CHUNK, 1, DN_CHUNK)
    act_spec = pl.BlockSpec((nh, rows, hd), lambda i: (0, i, 0))
    return pl.pallas_call(
        functools.partial(_dn_kernel, chunks=chunks),
        grid=(lp // rows,),
        in_specs=[act_spec, act_spec, act_spec, act_spec,
                  pl.BlockSpec((rows, 2 * nh), lambda i: (i, 0)),
                  pl.BlockSpec((2 * nh, chunks, 1, DN_CHUNK), lambda i: (0, i, 0, 0)),
                  _const_spec((1, hd))],
        out_specs=act_spec,
        out_shape=jax.ShapeDtypeStruct((nh, lp, hd), BF16),
        scratch_shapes=[pltpu.VMEM((nh, hd, hd), F32),
                        pltpu.VMEM((nh, chunks, DN_CHUNK, hd), F32),
                        pltpu.VMEM((nh, chunks, 2 * DN_CHUNK, hd), BF16),
                        pltpu.VMEM((nh, chunks, DN_CHUNK, DN_CHUNK), BF16),
                        pltpu.VMEM((nh, chunks, DN_CHUNK, hd), BF16),
                        pltpu.VMEM((nh, chunks, 1, hd), F32)],
        compiler_params=_cparams("arbitrary"),
        name="dn",
    )(q, k, v, z, gcol, grow4, norm_g.reshape(1, -1))


def _odd_out_kernel(og_ref, hs_ref, w_ref, gpost_ref, gpre_mlp_ref, w1_ref, w2_ref, gpost_mlp_ref, o_ref):
    og = jnp.concatenate([og_ref[h] for h in range(DN_HEADS)], axis=1)
    hs = hs_ref[...] + _rms(_dot(og, w_ref[...]), gpost_ref[...])
    o_ref[...] = _mlp_block(hs, gpre_mlp_ref, w1_ref, w2_ref, gpost_mlp_ref)


def _odd_out(og, hs, w_out, g_post, g_pre_mlp, w1, w2, g_post_mlp, keep=None):
    lp = hs.shape[0]
    tm, steps, first = _row_tiles(lp, keep)
    return pl.pallas_call(
        _odd_out_kernel,
        grid=(steps,),
        in_specs=[_rows_spec(og.shape, tm, first, row_axis=1), _rows_spec(hs.shape, tm, first),
                  _resident(w_out.shape), _const_spec((1, D_MODEL)),
                  _const_spec((1, D_MODEL)), _resident(w1.shape), _resident(w2.shape), _const_spec((1, D_MODEL))],
        out_specs=pl.BlockSpec((tm, D_MODEL), lambda i: (i, 0)),
        out_shape=jax.ShapeDtypeStruct((steps * tm, D_MODEL), F32),
        compiler_params=_cparams("parallel"),
        name="odd_out_mlp",
    )(og, hs, w_out, g_post, g_pre_mlp, w1, w2, g_post_mlp)


def kernel(x, meta_tokens, pre_mix_norm, post_mix_norm, pre_mlp_norm, post_mlp_norm, mlp_w1, mlp_w2, w_in_even, w_out_even, sb_out_norm, s5_lambda_re, s5_lambda_im, s5_log_dt, s5_b_re, s5_b_im, s5_c_re, s5_c_im, s5_d, s5_w_glu, s5_b_glu, s5_out_norm, w_in_odd, dn_conv_w, dn_a_log, dn_dt_bias, dn_out_norm, w_out_odd):
    assert x.shape[0] == 1 and x.shape[2] == D_MODEL
    depth = pre_mix_norm.shape[0]
    l = N_META + x.shape[1]
    lp = -(-l // SB_TILE) * SB_TILE
    ls = lp // STREAMS
    tr_even = _largest_div(ls, 104, 16)
    t_s5 = _largest_div(ls, 128, SUBLANES)
    dn_chunks = _largest_div(lp // DN_CHUNK, 10, DN_GROUP)
    row = lambda a: a.reshape(1, -1)

    hs = (x[0], meta_tokens)
    for i in range(depth):
        j = i // 2
        mlp = (row(pre_mlp_norm[i]), mlp_w1[i].astype(BF16), mlp_w2[i].astype(BF16), row(post_mlp_norm[i]))
        keep = (N_META, l - N_META) if i == depth - 1 else None
        if i % 2 == 0:
            qkv, u_il = _even_in(hs, row(pre_mix_norm[i]), w_in_even[j].astype(BF16), tr_even, lp)
            o_sb = _sb_attention(qkv, SB_TILE)
            o_s5 = _s5(u_il, s5_lambda_re[j], s5_lambda_im[j], s5_log_dt[j],
                       s5_b_re[j], s5_b_im[j], s5_c_re[j], s5_c_im[j], s5_d[j], s5_w_glu[j], s5_b_glu[j],
                       s5_out_norm[j], t_s5)
            hs = _even_out(o_sb, o_s5.reshape(lp, S5_WIDTH), hs, row(sb_out_norm[j]), w_out_even[j].astype(BF16),
                           row(post_mix_norm[i]), *mlp, keep)
        else:
            q, k, v, z, gcol, grow = _odd_in(hs, row(pre_mix_norm[i]), w_in_odd[j], dn_conv_w[j], dn_a_log[j],
                                             dn_dt_bias[j], SB_TILE)
            og = _dn(q, k, v, z, gcol, grow, dn_out_norm[j], dn_chunks)
            hs = _odd_out(og, hs, w_out_odd[j].astype(BF16), row(post_mix_norm[i]), *mlp, keep)
    return hs[None]
```

```python
import functools

import jax
import jax.numpy as jnp
from jax import lax
from jax.experimental import pallas as pl
from jax.experimental.pallas import tpu as pltpu

F32 = jnp.float32
BF16 = jnp.bfloat16

D_MODEL = 1024
N_META = 16
SB_HEAD_DIM = 64
SB_WIDTH = 512
S5_WIDTH = 512
S5_GROUP = 16
S5_GROUPS = 32
S5_STATE = 64
DN_HEAD_DIM = 128
DN_HEADS = 8
DN_CONV = 4
D_FF = 4096
FF_TILE = 1024
EPS = 1e-6

LANES = 128
SUBLANES = 8
STREAMS = SUBLANES
OCTETS = S5_GROUPS // 8
OCT_STATES = 8 * S5_STATE
DN_CHUNK = 128
DN_GROUP = 2
SB_TILE = 256
SB_KEY_TILE = 128
UNDERFLOW_LOG = -87.5
VMEM_LIMIT = 56 * 1024 * 1024


def _largest_div(n, cap, mult):
    best = None
    for d in range(mult, cap + 1, mult):
        if n % d == 0:
            best = d
    assert best is not None, (n, cap, mult)
    return best


def _cparams(*sem):
    return pltpu.CompilerParams(dimension_semantics=sem, vmem_limit_bytes=VMEM_LIMIT)


def _rms(x, g):
    ms = jnp.mean(x * x, axis=-1, keepdims=True)
    return x * lax.rsqrt(ms + EPS) * g


def _softplus(x):
    return jnp.maximum(x, 0.0) + jnp.log1p(jnp.exp(-jnp.abs(x)))


def _sigmoid(x):
    return 1.0 / (1.0 + jnp.exp(-x))


def _silu(x):
    return x * _sigmoid(x)


def _split3(x):
    hi = x.astype(BF16)
    r1 = x - hi.astype(F32)
    mid = r1.astype(BF16)
    lo = (r1 - mid.astype(F32)).astype(BF16)
    return hi, mid, lo


def _dot(a, b):
    return jnp.dot(a, b, preferred_element_type=F32)


def _dot_nt(a, b):
    return lax.dot_general(a, b, (((1,), (1,)), ((), ())), preferred_element_type=F32)


def _dot_tn(a, b):
    return lax.dot_general(a, b, (((0,), (0,)), ((), ())), preferred_element_type=F32)


def _const_spec(shape):
    nd = len(shape)
    return pl.BlockSpec(shape, lambda *_: (0,) * nd)


def _x_window_spec(seq, a0, tile):
    assert a0 % 16 == 0 and tile % 16 == 0 and seq % 16 == 0 and N_META % 16 == 0
    return pl.BlockSpec(
        (pl.Element(tile), pl.Element(D_MODEL)),
        lambda i: (pl.multiple_of(jnp.clip(a0 + i * tile - N_META, 0, seq - tile), 16), 0))


def _seq_tile(xb, meta_ref, i, a0, tile, steps, seq):
    l = N_META + seq
    out = xb
    if a0 < N_META:
        assert a0 == 0 and tile > N_META
        head = jnp.concatenate([meta_ref[...].astype(xb.dtype), xb[:tile - N_META]], axis=0)
        out = jnp.where(i == 0, head, out)
    n_full = min(steps, max(0, (l - a0) // tile))
    rem = min(tile, max(0, l - a0 - n_full * tile))
    pad_from = n_full
    if n_full < steps and rem:
        assert n_full > 0 or a0 >= N_META
        cut = jnp.concatenate([xb[tile - rem:], jnp.zeros((tile - rem, D_MODEL), xb.dtype)], axis=0)
        out = jnp.where(i == n_full, cut, out)
        pad_from += 1
    if pad_from < steps:
        out = jnp.where(i >= pad_from, jnp.zeros_like(out), out)
    return out


def _even_in_kernel(*refs, tr, ls, seq):
    if seq is None:
        hs_ref, g_ref, w_ref, qkv_ref, *u_refs = refs
        x = hs_ref[...].reshape(STREAMS * tr, D_MODEL)
    else:
        x_refs, (meta_ref, g_ref, w_ref, qkv_ref, *u_refs) = refs[:STREAMS], refs[STREAMS:]
        x = jnp.concatenate([_seq_tile(x_refs[r][...], meta_ref, pl.program_id(0), r * ls, tr, ls // tr, seq)
                             for r in range(STREAMS)], axis=0)
    hn = _rms(x, g_ref[...]).astype(BF16)
    p = _dot(hn, w_ref[...])
    q = p[:, :SB_WIDTH] * (SB_HEAD_DIM ** -0.5)
    qkv_ref[:, :, :SB_WIDTH] = q.reshape(STREAMS, tr, SB_WIDTH).astype(BF16)
    qkv_ref[:, :, SB_WIDTH:] = p[:, SB_WIDTH:3 * SB_WIDTH].reshape(STREAMS, tr, 2 * SB_WIDTH).astype(BF16)
    for o, u_ref in enumerate(u_refs):
        c0 = 3 * SB_WIDTH + o * LANES
        for r in range(STREAMS):
            u_ref[pl.ds(r, tr, stride=STREAMS), :] = p[r * tr:(r + 1) * tr, c0:c0 + LANES]


def _even_in(hs, g, w, tr, lp):
    ls = lp // STREAMS
    if isinstance(hs, tuple):
        x, meta = hs
        seq = x.shape[0]
        acts = [x] * STREAMS + [meta]
        act_specs = [_x_window_spec(seq, r * ls, tr) for r in range(STREAMS)] + [_const_spec(meta.shape)]
    else:
        seq = None
        acts = [hs.reshape(STREAMS, ls, D_MODEL)]
        act_specs = [pl.BlockSpec((STREAMS, tr, D_MODEL), lambda i: (0, i, 0))]
    qkv, *us = pl.pallas_call(
        functools.partial(_even_in_kernel, tr=tr, ls=ls, seq=seq),
        grid=(ls // tr,),
        in_specs=act_specs + [_const_spec((1, D_MODEL)), _const_spec(w.shape)],
        out_specs=[pl.BlockSpec((STREAMS, tr, 3 * SB_WIDTH), lambda i: (0, i, 0))]
                  + [pl.BlockSpec((STREAMS * tr, LANES), lambda i: (i, 0))] * OCTETS,
        out_shape=[jax.ShapeDtypeStruct((STREAMS, ls, 3 * SB_WIDTH), BF16)]
                  + [jax.ShapeDtypeStruct((lp, LANES), F32)] * OCTETS,
        compiler_params=_cparams("parallel"),
        name="even_in",
    )(*acts, g, w)
    return qkv.reshape(lp, 3 * SB_WIDTH), us


def _sb_kernel(q_ref, k_ref, v_ref, o_ref, acc_ref, car_ref, *, tq):
    tk = SB_KEY_TILE
    npair = SB_WIDTH // LANES
    pairs = range(npair)
    i = pl.program_id(0)
    lane = lax.broadcasted_iota(jnp.int32, (1, LANES), 1)
    qs = []
    for p in pairs:
        q2 = q_ref[:, p * LANES:(p + 1) * LANES]
        zq = jnp.zeros_like(q2)
        qs.append((jnp.where(lane < SB_HEAD_DIM, q2, zq), jnp.where(lane >= SB_HEAD_DIM, q2, zq)))
    jr = lax.broadcasted_iota(jnp.int32, (tk, tk), 0)
    jc = lax.broadcasted_iota(jnp.int32, (tk, tk), 1)
    later = jnp.where(jr > jc, 1.0, 0.0).astype(BF16)
    acc_ref[...] = jnp.zeros_like(acc_ref)
    car_ref[...] = jnp.zeros_like(car_ref)

    def key_block(j, r0, r1, diagonal):
        rows = slice(r0, r1)
        nr = r1 - r0
        off = pl.multiple_of(j * tk, tk)
        if diagonal:
            col = j * tk + lax.broadcasted_iota(jnp.int32, (1, tk), 1)
            row = i * tq + r0 + lax.broadcasted_iota(jnp.int32, (nr, 1), 0)
            valid = col < row
            valid = jnp.concatenate([valid, valid], axis=0)
            keep = lambda a: jnp.where(valid, a, 0.0)
        else:
            keep = lambda a: a
        z = [_dot_nt(jnp.concatenate([qs[p][0][rows], qs[p][1][rows]], axis=0),
                     k_ref[pl.ds(off, tk), p * LANES:(p + 1) * LANES]) for p in pairs]
        lk = [keep(-(jnp.maximum(z[p], 0.0) + jnp.log(1.0 + jnp.exp(-jnp.abs(z[p]))))) for p in pairs]
        hi = [lk[p].astype(BF16) for p in pairs]
        lo = [(lk[p] - hi[p].astype(F32)).astype(BF16) for p in pairs]
        rc = [_dot(jnp.concatenate([hi[p], lo[p]], axis=0), later) for p in pairs]
        car = [car_ref[p, :, rows].reshape(2 * nr, 1) for p in pairs]
        w = [keep(jnp.exp(z[p] + lk[p] + car[p] + rc[p][:2 * nr] + rc[p][2 * nr:])).astype(BF16) for p in pairs]
        for p in pairs:
            acc_ref[p, :, rows] += _dot(w[p], v_ref[pl.ds(off, tk), p * LANES:(p + 1) * LANES]).reshape(2, nr, LANES)
            car_ref[p, :, rows] = (car[p] + jnp.sum(lk[p], axis=1, keepdims=True)).reshape(2, nr, 1)

    def band_live(r1):
        return (jnp.max(car_ref[:, :, r1 - tk:r1]) > UNDERFLOW_LOG).astype(jnp.int32)

    nd = tq // tk
    for d in range(nd - 1, -1, -1):
        key_block(i * nd + d, d * tk, tq, diagonal=True)

    j = i * nd - 1
    for r1 in range(tq, 0, -tk):
        def body(c, r1=r1):
            key_block(c[0], 0, r1, diagonal=False)
            return c[0] - 1, band_live(r1)

        j, _ = lax.while_loop(lambda c: jnp.logical_and(c[0] >= 0, c[1] > 0), body, (j, band_live(r1)))
    for p in pairs:
        o_ref[:, p * LANES:(p + 1) * LANES] = jnp.where(lane < SB_HEAD_DIM, acc_ref[p, 0], acc_ref[p, 1])


def _sb_attention(qkv, tq):
    lp = qkv.shape[0]
    npair = SB_WIDTH // LANES
    resident = lambda col: pl.BlockSpec((lp, SB_WIDTH), lambda i: (0, col), pipeline_mode=pl.Buffered(1))
    return pl.pallas_call(
        functools.partial(_sb_kernel, tq=tq),
        grid=(lp // tq,),
        in_specs=[pl.BlockSpec((tq, SB_WIDTH), lambda i: (i, 0)), resident(1), resident(2)],
        out_specs=pl.BlockSpec((tq, SB_WIDTH), lambda i: (i, 0)),
        out_shape=jax.ShapeDtypeStruct((lp, SB_WIDTH), F32),
        scratch_shapes=[pltpu.VMEM((npair, 2, tq, LANES), F32), pltpu.VMEM((npair, 2, tq, 1), F32)],
        compiler_params=_cparams("parallel"),
        name="sb_attn",
    )(qkv, qkv, qkv)


def _cmul(ar, ai, br, bi):
    return ar * br - ai * bi, ar * bi + ai * br


def _s5_prep_kernel(lre_ref, lim_ref, ldt_ref, bre_ref, bim_ref, tab_ref, bb_ref, *, ls):
    lr = jnp.minimum(lre_ref[...], -1e-4)
    li = lim_ref[...]
    dt = jnp.exp(ldt_ref[...])
    mag = jnp.exp(lr * dt)
    ang = li * dt
    ar, ai = mag * jnp.cos(ang), mag * jnp.sin(ang)
    den = lr * lr + li * li
    nr, ni = ar - 1.0, ai
    cr = (nr * lr + ni * li) / den
    ci = (ni * lr - nr * li) / den
    pr, pi = jnp.ones_like(ar), jnp.zeros_like(ar)
    br, bi = ar, ai
    n = ls
    while n:
        if n & 1:
            pr, pi = _cmul(pr, pi, br, bi)
        n >>= 1
        if n:
            br, bi = _cmul(br, bi, br, bi)
    tab_ref[...] = jnp.concatenate(
        [ar, ai, cr, ci, pr, pi, jnp.zeros_like(ar), jnp.zeros_like(ar)], axis=0)
    for o in range(OCTETS):
        sl = slice(o * OCT_STATES, (o + 1) * OCT_STATES)
        bre, bim = bre_ref[o], bim_ref[o]
        bb_ref[o, :, :OCT_STATES] = (cr[:, sl] * bre - ci[:, sl] * bim).astype(BF16)
        bb_ref[o, :, OCT_STATES:] = (cr[:, sl] * bim + ci[:, sl] * bre).astype(BF16)


def _s5_prep(lam_re, lam_im, log_dt, bre, bim, ls):
    n = S5_GROUPS * S5_STATE
    row = lambda a: a.reshape(1, n)
    ldt = jnp.broadcast_to(log_dt[:, None], (S5_GROUPS, S5_STATE))
    return pl.pallas_call(
        functools.partial(_s5_prep_kernel, ls=ls),
        out_shape=[jax.ShapeDtypeStruct((SUBLANES, n), F32),
                   jax.ShapeDtypeStruct((OCTETS, 8 * S5_GROUP, 2 * OCT_STATES), BF16)],
        name="s5_prep",
    )(row(lam_re), row(lam_im), row(ldt), bre, bim)


def _s5_scan_octet(tab_ref, o, bur_ref, bui_ref, xr, xi, t_steps, hist=None):
    sl = slice(o * OCT_STATES, (o + 1) * OCT_STATES)
    ar = jnp.broadcast_to(tab_ref[0:1, sl], (STREAMS, OCT_STATES))
    ai = jnp.broadcast_to(tab_ref[1:2, sl], (STREAMS, OCT_STATES))

    for t in range(t_steps):
        rows = slice(t * STREAMS, (t + 1) * STREAMS)
        xr, xi = ar * xr - ai * xi + bur_ref[o, rows, :], ar * xi + ai * xr + bui_ref[o, rows, :]
        if hist is not None:
            hist[0][o, rows, :] = xr
            hist[1][o, rows, :] = xi
    return xr, xi


def _s5_pass_a_kernel(u0_ref, u1_ref, u2_ref, u3_ref, tab_ref, bb_ref, x0_ref,
                      xr_ref, xi_ref, bur_ref, bui_ref, *, t_steps):
    u_refs = (u0_ref, u1_ref, u2_ref, u3_ref)
    i = pl.program_id(0)

    @pl.when(i == 0)
    def _():
        xr_ref[...] = jnp.zeros_like(xr_ref)
        xi_ref[...] = jnp.zeros_like(xi_ref)

    for o in range(OCTETS):
        ub = u_refs[o][...].astype(BF16)
        bur_ref[o] = _dot(ub, bb_ref[o, :, :OCT_STATES])
        bui_ref[o] = _dot(ub, bb_ref[o, :, OCT_STATES:])
        xr, xi = _s5_scan_octet(tab_ref, o, bur_ref, bui_ref, xr_ref[o], xi_ref[o], t_steps)
        xr_ref[o] = xr
        xi_ref[o] = xi

    @pl.when(i == pl.num_programs(0) - 1)
    def _():
        sub = lax.broadcasted_iota(jnp.int32, (STREAMS, OCT_STATES), 0)
        for o in range(OCTETS):
            sl = slice(o * OCT_STATES, (o + 1) * OCT_STATES)
            pr, pi = tab_ref[4:5, sl], tab_ref[5:6, sl]
            er, ei = xr_ref[o], xi_ref[o]
            x0r = jnp.zeros((STREAMS, OCT_STATES), F32)
            x0i = jnp.zeros((STREAMS, OCT_STATES), F32)
            cr = jnp.zeros((1, OCT_STATES), F32)
            ci = jnp.zeros((1, OCT_STATES), F32)
            for r in range(1, STREAMS):
                mr, mi = _cmul(pr, pi, cr, ci)
                cr, ci = er[r - 1:r] + mr, ei[r - 1:r] + mi
                x0r = jnp.where(sub == r, cr, x0r)
                x0i = jnp.where(sub == r, ci, x0i)
            x0_ref[:, 2 * o * OCT_STATES:(2 * o + 1) * OCT_STATES] = x0r
            x0_ref[:, (2 * o + 1) * OCT_STATES:(2 * o + 2) * OCT_STATES] = x0i


def _gelu_tanh(x):
    return 0.5 * x * (1.0 + jnp.tanh(0.7978845608028654 * (x + 0.044715 * (x * x * x))))


def _s5_pass_b_kernel(u0_ref, u1_ref, u2_ref, u3_ref, tab_ref, bb_ref, x0_ref, cre_ref, cim_ref, d_ref,
                      wglu_ref, bglu_ref, g_ref, o_ref, xr_ref, xi_ref, bur_ref, bui_ref, hr_ref, hi_ref,
                      y0_ref, y1_ref, y2_ref, y3_ref, *, t_steps):
    u_refs = (u0_ref, u1_ref, u2_ref, u3_ref)
    y_refs = (y0_ref, y1_ref, y2_ref, y3_ref)
    i = pl.program_id(0)

    @pl.when(i == 0)
    def _():
        for o in range(OCTETS):
            xr_ref[o] = x0_ref[:, 2 * o * OCT_STATES:(2 * o + 1) * OCT_STATES]
            xi_ref[o] = x0_ref[:, (2 * o + 1) * OCT_STATES:(2 * o + 2) * OCT_STATES]

    ys = []
    for o in range(OCTETS):
        u = u_refs[o][...]
        bur_ref[o] = _dot(u.astype(BF16), bb_ref[o, :, :OCT_STATES])
        bui_ref[o] = _dot(u.astype(BF16), bb_ref[o, :, OCT_STATES:])
        xr, xi = _s5_scan_octet(tab_ref, o, bur_ref, bui_ref, xr_ref[o], xi_ref[o], t_steps,
                                hist=(hr_ref, hi_ref))
        xr_ref[o] = xr
        xi_ref[o] = xi
        ys.append(_dot(hr_ref[o].astype(BF16), cre_ref[o]) - _dot(hi_ref[o].astype(BF16), cim_ref[o])
                  + d_ref[:, o * LANES:(o + 1) * LANES] * u)

    hact = _gelu_tanh(jnp.concatenate(ys, axis=1))
    gate = _sigmoid(_dot(hact.astype(BF16), wglu_ref[...]) + bglu_ref[...])
    res = _rms(hact * gate, g_ref[...])
    for o in range(OCTETS):
        y_refs[o][...] = res[:, o * LANES:(o + 1) * LANES]
        for r in range(STREAMS):
            o_ref[r, :, o * LANES:(o + 1) * LANES] = y_refs[o][pl.ds(r, t_steps, stride=STREAMS), :]


def _s5_block_diag_b(b):
    bt = b.reshape(OCTETS, 8, S5_STATE, S5_GROUP).transpose(0, 1, 3, 2)
    eye = jnp.eye(8, dtype=b.dtype)
    return jnp.einsum('ogpn,gh->ogphn', bt, eye).reshape(OCTETS, 8 * S5_GROUP, OCT_STATES)


def _s5_block_diag_c(c):
    ct = c.reshape(OCTETS, 8, S5_GROUP, S5_STATE).transpose(0, 1, 3, 2)
    eye = jnp.eye(8, dtype=c.dtype)
    return jnp.einsum('ognp,gh->ognhp', ct, eye).reshape(OCTETS, OCT_STATES, 8 * S5_GROUP)


def _s5(us, lam_re, lam_im, log_dt, b_re, b_im, c_re, c_im, d_skip, w_glu, b_glu, norm_g, t_steps):
    rows = us[0].shape[0]
    ls = rows // STREAMS
    tile = STREAMS * t_steps
    nsteps = ls // t_steps
    tab, bb = _s5_prep(lam_re, lam_im, log_dt, _s5_block_diag_b(b_re), _s5_block_diag_b(b_im), ls)
    cre, cim = _s5_block_diag_c(c_re).astype(BF16), _s5_block_diag_c(c_im).astype(BF16)
    nstate = S5_GROUPS * S5_STATE
    state_scratch = [pltpu.VMEM((OCTETS, STREAMS, OCT_STATES), F32)] * 2
    bu_scratch = [pltpu.VMEM((OCTETS, tile, OCT_STATES), F32)] * 2
    u_specs = [pl.BlockSpec((tile, LANES), lambda i: (i, 0))] * OCTETS

    x0 = pl.pallas_call(
        functools.partial(_s5_pass_a_kernel, t_steps=t_steps),
        grid=(nsteps,),
        in_specs=u_specs + [_const_spec(tab.shape), _const_spec(bb.shape)],
        out_specs=_const_spec((STREAMS, 2 * nstate)),
        out_shape=jax.ShapeDtypeStruct((STREAMS, 2 * nstate), F32),
        scratch_shapes=state_scratch + bu_scratch,
        compiler_params=_cparams("arbitrary"),
        name="s5_pass_a",
    )(*us, tab, bb)

    return pl.pallas_call(
        functools.partial(_s5_pass_b_kernel, t_steps=t_steps),
        grid=(nsteps,),
        in_specs=u_specs + [_const_spec(tab.shape), _const_spec(bb.shape),
                            _const_spec(x0.shape), _const_spec(cre.shape), _const_spec(cim.shape),
                            _const_spec((1, S5_WIDTH)), _const_spec(w_glu.shape), _const_spec((1, S5_WIDTH)),
                            _const_spec((1, S5_WIDTH))],
        out_specs=pl.BlockSpec((STREAMS, t_steps, S5_WIDTH), lambda i: (0, i, 0)),
        out_shape=jax.ShapeDtypeStruct((STREAMS, ls, S5_WIDTH), F32),
        scratch_shapes=state_scratch + bu_scratch + bu_scratch + [pltpu.VMEM((tile, LANES), F32)] * OCTETS,
        compiler_params=_cparams("arbitrary"),
        name="s5_pass_b",
    )(*us, tab, bb, x0, cre, cim, d_skip.reshape(1, -1), w_glu.astype(BF16), b_glu.reshape(1, -1),
      norm_g.reshape(1, -1))


def _mlp_block(hs, gpre_ref, w1_ref, w2_ref, gpost_ref):
    hn = _rms(hs, gpre_ref[...]).astype(BF16)
    acc = jnp.zeros(hs.shape, F32)
    for f in range(D_FF // FF_TILE):
        a = jnp.maximum(_dot(hn, w1_ref[:, f * FF_TILE:(f + 1) * FF_TILE]), 0.0)
        acc = acc + _dot((a * a).astype(BF16), w2_ref[f * FF_TILE:(f + 1) * FF_TILE, :])
    return hs + _rms(acc, gpost_ref[...])


def _resident(shape):
    nd = len(shape)
    return pl.BlockSpec(shape, lambda *_: (0,) * nd, pipeline_mode=pl.Buffered(1))


def _even_out_kernel(osb_ref, os5_ref, *refs, seq_tile):
    if seq_tile is None:
        hs_ref, *refs = refs
        hs = hs_ref[...]
    else:
        x_ref, meta_ref, *refs = refs
        hs = _seq_tile(x_ref[...], meta_ref, pl.program_id(0), *seq_tile)
    gsb_ref, wsb_ref, ws5_ref, gpost_ref, gpre_mlp_ref, w1_ref, w2_ref, gpost_mlp_ref, o_ref = refs
    sb = _rms(osb_ref[...], gsb_ref[...]).astype(BF16)
    mix = _dot(sb, wsb_ref[...]) + _dot(os5_ref[...].astype(BF16), ws5_ref[...])
    hs = hs + _rms(mix, gpost_ref[...])
    o_ref[...] = _mlp_block(hs, gpre_mlp_ref, w1_ref, w2_ref, gpost_mlp_ref)


def _row_tiles(lp, keep):
    if keep is None:
        tm = _largest_div(lp, 640, 16)
        return tm, lp // tm, None
    first, count = keep
    tm = _largest_div(count, 640, 16)
    return tm, count // tm, first


def _rows_spec(shape, tm, first, row_axis=0):
    nd = len(shape)
    if first is None:
        block = tuple(tm if a == row_axis else s for a, s in enumerate(shape))
        return pl.BlockSpec(block, lambda i: tuple(i if a == row_axis else 0 for a in range(nd)))
    assert first % 16 == 0 and tm % 16 == 0
    block = tuple(pl.Element(tm if a == row_axis else s) for a, s in enumerate(shape))
    return pl.BlockSpec(block, lambda i: tuple(pl.multiple_of(first + i * tm, 16) if a == row_axis else 0
                                               for a in range(nd)))


def _even_out(o_sb, o_s5, hs, g_sb, w_out, g_post, g_pre_mlp, w1, w2, g_post_mlp, keep=None):
    lp = o_sb.shape[0]
    w_sb, w_s5 = w_out[:SB_WIDTH], w_out[SB_WIDTH:]
    tm, steps, first = _row_tiles(lp, keep)
    if isinstance(hs, tuple):
        x, meta = hs
        seq_tile = (first or 0, tm, steps, x.shape[0])
        res, res_specs = [x, meta], [_x_window_spec(x.shape[0], first or 0, tm), _const_spec(meta.shape)]
    else:
        seq_tile = None
        res, res_specs = [hs], [_rows_spec(hs.shape, tm, first)]
    return pl.pallas_call(
        functools.partial(_even_out_kernel, seq_tile=seq_tile),
        grid=(steps,),
        in_specs=[_rows_spec(o_sb.shape, tm, first), _rows_spec(o_s5.shape, tm, first), *res_specs,
                  _const_spec((1, SB_WIDTH)), _resident(w_sb.shape), _resident(w_s5.shape),
                  _const_spec((1, D_MODEL)),
                  _const_spec((1, D_MODEL)), _resident(w1.shape), _resident(w2.shape), _const_spec((1, D_MODEL))],
        out_specs=pl.BlockSpec((tm, D_MODEL), lambda i: (i, 0)),
        out_shape=jax.ShapeDtypeStruct((steps * tm, D_MODEL), F32),
        compiler_params=_cparams("parallel"),
        name="even_out_mlp",
    )(o_sb, o_s5, *res, g_sb, w_sb, w_s5, g_post, g_pre_mlp, w1, w2, g_post_mlp)


def _chunk_tri(n, transposed):
    r = lax.broadcasted_iota(jnp.int32, (n, n), 0)
    c = lax.broadcasted_iota(jnp.int32, (n, n), 1)
    same = (r // DN_CHUNK) == (c // DN_CHUNK)
    order = (r <= c) if transposed else (r >= c)
    return jnp.where(jnp.logical_and(same, order), 1.0, 0.0).astype(BF16)


def _odd_in_kernel(hs_ref, g_ref, wqkv_ref, wz_ref, wabc_ref, wabr_ref, conv_ref, alc_ref, dtc_ref, alr_ref, dtr_ref,
                   q_ref, k_ref, v_ref, z_ref, gcol_ref, grow_ref, carry_ref, *, tm):
    @pl.when(pl.program_id(0) == 0)
    def _():
        carry_ref[...] = jnp.zeros_like(carry_ref)

    hn = _rms(hs_ref[...], g_ref[...]).astype(BF16)

    ab = _dot(hn, wabc_ref[...])
    lane = lax.broadcasted_iota(jnp.int32, (1, LANES), 1)
    gb = jnp.where(lane < DN_HEADS, -jnp.exp(alc_ref[...]) * _softplus(ab + dtc_ref[...]), _sigmoid(ab))
    tri = _chunk_tri(tm, transposed=False)
    hi, mid, lo = _split3(gb)
    gsum = _dot(tri, hi) + _dot(tri, mid) + _dot(tri, lo)
    gcol_ref[...] = jnp.where(lane < DN_HEADS, gsum, gb)[:, :2 * DN_HEADS]

    abr = _dot_nt(wabr_ref[...], hn)
    sub = lax.broadcasted_iota(jnp.int32, (2 * DN_HEADS, 1), 0)
    gbr = jnp.where(sub < DN_HEADS, -jnp.exp(alr_ref[...]) * _softplus(abr + dtr_ref[...]), _sigmoid(abr))
    trit = _chunk_tri(tm, transposed=True)
    hi, mid, lo = _split3(gbr)
    gsumr = _dot(hi, trit) + _dot(mid, trit) + _dot(lo, trit)
    grow_ref[...] = jnp.where(sub < DN_HEADS, gsumr, gbr)

    first = lax.broadcasted_iota(jnp.int32, (SUBLANES, 1), 0)
    w = DN_HEADS * DN_HEAD_DIM
    hd = DN_HEAD_DIM

    def conv_silu(c0):
        cols = slice(c0, c0 + 2 * hd)
        x = _dot(hn, wqkv_ref[:, cols])
        prev = carry_ref[:, cols]
        carry_ref[:, cols] = x[tm - SUBLANES:, :]
        y = x * conv_ref[DN_CONV - 1:DN_CONV, cols]
        for s in range(1, DN_CONV):
            sh = pltpu.roll(x, s, 0)
            head = jnp.where(first < s, pltpu.roll(prev, s, 0), sh[:SUBLANES])
            sh = jnp.concatenate([head, sh[SUBLANES:]], axis=0)
            y = y + sh * conv_ref[DN_CONV - 1 - s:DN_CONV - s, cols]
        return _silu(y)

    for h in range(0, DN_HEADS, 2):
        q2 = conv_silu(h * hd)
        k2 = conv_silu(w + h * hd)
        v2 = conv_silu(2 * w + h * hd)
        z2 = _dot(hn, wz_ref[:, h * hd:(h + 2) * hd])
        for d in range(2):
            sl = slice(d * hd, (d + 1) * hd)
            qh, kh = q2[:, sl], k2[:, sl]
            q_ref[h + d] = (qh * (lax.rsqrt(jnp.sum(qh * qh, axis=-1, keepdims=True) + EPS)
                                  * (hd ** -0.5))).astype(BF16)
            k_ref[h + d] = (kh * lax.rsqrt(jnp.sum(kh * kh, axis=-1, keepdims=True) + EPS)).astype(BF16)
            v_ref[h + d] = v2[:, sl].astype(BF16)
            z_ref[h + d] = z2[:, sl].astype(BF16)


def _odd_in(hs, g, w_in, conv_w, a_log, dt_bias, tm):
    lp = hs.shape[0]
    w = DN_HEADS * DN_HEAD_DIM
    wqkv = w_in[:, :3 * w].astype(BF16)
    wz = w_in[:, 3 * w:4 * w].astype(BF16)
    wab = w_in[:, 4 * w:]
    wabc = jnp.pad(wab, ((0, 0), (0, LANES - 2 * DN_HEADS))).astype(BF16)
    wabr = wab.T.astype(BF16)
    pad_row = lambda a: jnp.pad(a.reshape(1, -1), ((0, 0), (0, LANES - DN_HEADS)))
    pad_col = lambda a: jnp.pad(a.reshape(-1, 1), ((0, DN_HEADS), (0, 0)))
    row_spec = lambda width: pl.BlockSpec((tm, width), lambda i: (i, 0))
    act = jax.ShapeDtypeStruct((DN_HEADS, lp, DN_HEAD_DIM), BF16)
    act_spec = pl.BlockSpec((DN_HEADS, tm, DN_HEAD_DIM), lambda i: (0, i, 0))
    return pl.pallas_call(
        functools.partial(_odd_in_kernel, tm=tm),
        grid=(lp // tm,),
        in_specs=[row_spec(D_MODEL), _const_spec((1, D_MODEL)), _const_spec(wqkv.shape), _const_spec(wz.shape),
                  _const_spec(wabc.shape), _const_spec(wabr.shape), _const_spec(conv_w.shape),
                  _const_spec((1, LANES)), _const_spec((1, LANES)),
                  _const_spec((2 * DN_HEADS, 1)), _const_spec((2 * DN_HEADS, 1))],
        out_specs=[act_spec, act_spec, act_spec, act_spec, row_spec(2 * DN_HEADS),
                   pl.BlockSpec((2 * DN_HEADS, tm), lambda i: (0, i))],
        out_shape=[act, act, act, act,
                   jax.ShapeDtypeStruct((lp, 2 * DN_HEADS), F32),
                   jax.ShapeDtypeStruct((2 * DN_HEADS, lp), F32)],
        scratch_shapes=[pltpu.VMEM((SUBLANES, 3 * w), F32)],
        compiler_params=_cparams("arbitrary"),
        name="odd_in",
    )(hs, g, wqkv, wz, wabc, wabr, conv_w, pad_row(a_log), pad_row(dt_bias), pad_col(a_log), pad_col(dt_bias))


def _bdot(a, b):
    return lax.dot_general(a, b, (((2,), (1,)), ((0,), (0,))), preferred_element_type=F32)


def _bdot_nt(a, b):
    return lax.dot_general(a, b, (((2,), (2,)), ((0,), (0,))), preferred_element_type=F32)


def _bdot_tn(a, b):
    return lax.dot_general(a, b, (((1,), (1,)), ((0,), (0,))), preferred_element_type=F32)


def _dn_kernel(q_ref, k_ref, v_ref, z_ref, gcol_ref, grow_ref, ng_ref, o_ref,
               s_ref, u_scr, wq_scr, a_scr, kt_scr, egl_scr, *, chunks):
    c64, nh, hd, grp = DN_CHUNK, DN_HEADS, DN_HEAD_DIM, DN_GROUP
    nb = nh * grp

    @pl.when(pl.program_id(0) == 0)
    def _():
        s_ref[...] = jnp.zeros_like(s_ref)

    ri = lax.broadcasted_iota(jnp.int32, (c64, c64), 0)
    ci = lax.broadcasted_iota(jnp.int32, (c64, c64), 1)
    incl = ri >= ci
    strict = ri > ci
    eye = jnp.where(ri == ci, 1.0, 0.0)
    pair_masks = []
    s = 1
    while s < c64:
        pair_masks.append(jnp.logical_and((ri // s) % 2 == 1, (ri // s) - 1 == ci // s))
        s *= 2
    ng = ng_ref[...]

    def prep(gi, carry):
        c0 = gi * grp
        rows = pl.ds(pl.multiple_of(c0 * c64, grp * c64), grp * c64)
        cs = pl.ds(c0, grp)
        qb = q_ref[:, rows, :].reshape(nb, c64, hd)
        kb = k_ref[:, rows, :].reshape(nb, c64, hd)
        vb = v_ref[:, rows, :].reshape(nb, c64, hd)
        kf, vf = kb.astype(F32), vb.astype(F32)
        gcb = gcol_ref[rows, :]
        gc = jnp.stack([gcb[:, h:h + 1] for h in range(nh)], axis=0).reshape(nb, c64, 1)
        be = jnp.stack([gcb[:, nh + h:nh + h + 1] for h in range(nh)], axis=0).reshape(nb, c64, 1)
        gr = grow_ref[0:nh, cs].reshape(nb, 1, c64)
        gl = gc[:, c64 - 1:c64, :]
        eg = jnp.exp(gc)
        decay = jnp.exp(jnp.where(incl, gc - gr, -jnp.inf))
        qkk = _bdot_nt(jnp.concatenate([qb, kb], axis=1), kb)
        n = jnp.where(strict, be * qkk[:, c64:] * decay, 0.0)
        x = eye - jnp.where(pair_masks[0], n, 0.0)
        for m in pair_masks[1:]:
            xb = x.astype(BF16)
            x = x - _bdot(xb, _bdot(jnp.where(m, n, 0.0).astype(BF16), xb).astype(BF16))
        rhs = jnp.concatenate([(vf * be).astype(BF16), (kf * (be * eg)).astype(BF16)], axis=2)
        uw = _bdot(x.astype(BF16), rhs)
        qg = (qb.astype(F32) * eg).astype(BF16)
        u_scr[:, cs] = uw[:, :, :hd].reshape(nh, grp, c64, hd)
        wq_scr[:, cs] = jnp.concatenate([uw[:, :, hd:].astype(BF16), qg], axis=1).reshape(nh, grp, 2 * c64, hd)
        a_scr[:, cs] = (qkk[:, :c64] * decay).astype(BF16).reshape(nh, grp, c64, c64)
        kt_scr[:, cs] = (kf * jnp.exp(gl - gc)).astype(BF16).reshape(nh, grp, c64, hd)
        egl_scr[:, cs] = jnp.broadcast_to(jnp.exp(gl), (nb, 1, hd)).reshape(nh, grp, 1, hd)
        return carry

    lax.fori_loop(0, chunks // grp, prep, 0)

    def step(c, carry):
        rows = pl.ds(pl.multiple_of(c * c64, c64), c64)
        s = s_ref[...]
        ws = _bdot(wq_scr[:, c], s.astype(BF16))
        vnb = (u_scr[:, c] - ws[:, :c64]).astype(BF16)
        o = ws[:, c64:] + _bdot(a_scr[:, c], vnb)
        s_ref[...] = s * egl_scr[:, c] + _bdot_tn(kt_scr[:, c], vnb)
        on = o * lax.rsqrt(jnp.mean(o * o, axis=-1, keepdims=True) + EPS) * ng
        o_ref[:, rows, :] = (on * _silu(z_ref[:, rows, :].astype(F32))).astype(BF16)
        return carry

    lax.fori_loop(0, chunks, step, 0)


def _dn(q, k, v, z, gcol, grow, norm_g, chunks):
    nh, lp, hd = q.shape
    rows = chunks * DN_CHUNK
    grow4 = grow.reshape(2 * nh, lp // DN_CHUNK, 1, DN_CHUNK)
    act_spec = pl.BlockSpec((nh, rows, hd), lambda i: (0, i, 0))
    return pl.pallas_call(
        functools.partial(_dn_kernel, chunks=chunks),
        grid=(lp // rows,),
        in_specs=[act_spec, act_spec, act_spec, act_spec,
                  pl.BlockSpec((rows, 2 * nh), lambda i: (i, 0)),
                  pl.BlockSpec((2 * nh, chunks, 1, DN_CHUNK), lambda i: (0, i, 0, 0)),
                  _const_spec((1, hd))],
        out_specs=act_spec,
        out_shape=jax.ShapeDtypeStruct((nh, lp, hd), BF16),
        scratch_shapes=[pltpu.VMEM((nh, hd, hd), F32),
                        pltpu.VMEM((nh, chunks, DN_CHUNK, hd), F32),
                        pltpu.VMEM((nh, chunks, 2 * DN_CHUNK, hd), BF16),
                        pltpu.VMEM((nh, chunks, DN_CHUNK, DN_CHUNK), BF16),
                        pltpu.VMEM((nh, chunks, DN_CHUNK, hd), BF16),
                        pltpu.VMEM((nh, chunks, 1, hd), F32)],
        compiler_params=_cparams("arbitrary"),
        name="dn",
    )(q, k, v, z, gcol, grow4, norm_g.reshape(1, -1))


def _odd_out_kernel(og_ref, hs_ref, w_ref, gpost_ref, gpre_mlp_ref, w1_ref, w2_ref, gpost_mlp_ref, o_ref):
    og = jnp.concatenate([og_ref[h] for h in range(DN_HEADS)], axis=1)
    hs = hs_ref[...] + _rms(_dot(og, w_ref[...]), gpost_ref[...])
    o_ref[...] = _mlp_block(hs, gpre_mlp_ref, w1_ref, w2_ref, gpost_mlp_ref)


def _odd_out(og, hs, w_out, g_post, g_pre_mlp, w1, w2, g_post_mlp, keep=None):
    lp = hs.shape[0]
    tm, steps, first = _row_tiles(lp, keep)
    return pl.pallas_call(
        _odd_out_kernel,
        grid=(steps,),
        in_specs=[_rows_spec(og.shape, tm, first, row_axis=1), _rows_spec(hs.shape, tm, first),
                  _resident(w_out.shape), _const_spec((1, D_MODEL)),
                  _const_spec((1, D_MODEL)), _resident(w1.shape), _resident(w2.shape), _const_spec((1, D_MODEL))],
        out_specs=pl.BlockSpec((tm, D_MODEL), lambda i: (i, 0)),
        out_shape=jax.ShapeDtypeStruct((steps * tm, D_MODEL), F32),
        compiler_params=_cparams("parallel"),
        name="odd_out_mlp",
    )(og, hs, w_out, g_post, g_pre_mlp, w1, w2, g_post_mlp)


def kernel(x, meta_tokens, pre_mix_norm, post_mix_norm, pre_mlp_norm, post_mlp_norm, mlp_w1, mlp_w2, w_in_even, w_out_even, sb_out_norm, s5_lambda_re, s5_lambda_im, s5_log_dt, s5_b_re, s5_b_im, s5_c_re, s5_c_im, s5_d, s5_w_glu, s5_b_glu, s5_out_norm, w_in_odd, dn_conv_w, dn_a_log, dn_dt_bias, dn_out_norm, w_out_odd):
    assert x.shape[0] == 1 and x.shape[2] == D_MODEL
    depth = pre_mix_norm.shape[0]
    l = N_META + x.shape[1]
    lp = -(-l // SB_TILE) * SB_TILE
    ls = lp // STREAMS
    tr_even = _largest_div(ls, 104, 16)
    t_s5 = _largest_div(ls, 128, SUBLANES)
    dn_chunks = _largest_div(lp // DN_CHUNK, 10, DN_GROUP)
    row = lambda a: a.reshape(1, -1)

    hs = (x[0], meta_tokens)
    for i in range(depth):
        j = i // 2
        mlp = (row(pre_mlp_norm[i]), mlp_w1[i].astype(BF16), mlp_w2[i].astype(BF16), row(post_mlp_norm[i]))
        keep = (N_META, l - N_META) if i == depth - 1 else None
        if i % 2 == 0:
            qkv, u_il = _even_in(hs, row(pre_mix_norm[i]), w_in_even[j].astype(BF16), tr_even, lp)
            o_sb = _sb_attention(qkv, SB_TILE)
            o_s5 = _s5(u_il, s5_lambda_re[j], s5_lambda_im[j], s5_log_dt[j],
                       s5_b_re[j], s5_b_im[j], s5_c_re[j], s5_c_im[j], s5_d[j], s5_w_glu[j], s5_b_glu[j],
                       s5_out_norm[j], t_s5)
            hs = _even_out(o_sb, o_s5.reshape(lp, S5_WIDTH), hs, row(sb_out_norm[j]), w_out_even[j].astype(BF16),
                           row(post_mix_norm[i]), *mlp, keep)
        else:
            q, k, v, z, gcol, grow = _odd_in(hs, row(pre_mix_norm[i]), w_in_odd[j], dn_conv_w[j], dn_a_log[j],
                                             dn_dt_bias[j], SB_TILE)
            og = _dn(q, k, v, z, gcol, grow, dn_out_norm[j], dn_chunks)
            hs = _odd_out(og, hs, w_out_odd[j].astype(BF16), row(post_mix_norm[i]), *mlp, keep)
    return hs[None]
```

```python
import functools

import jax
import jax.numpy as jnp
from jax import lax
from jax.experimental import pallas as pl
from jax.experimental.pallas import tpu as pltpu

F32 = jnp.float32
BF16 = jnp.bfloat16

D_MODEL = 1024
N_META = 16
SB_HEAD_DIM = 64
SB_WIDTH = 512
S5_WIDTH = 512
S5_GROUP = 16
S5_GROUPS = 32
S5_STATE = 64
DN_HEAD_DIM = 128
DN_HEADS = 8
DN_CONV = 4
D_FF = 4096
FF_TILE = 1024
EPS = 1e-6

LANES = 128
SUBLANES = 8
STREAMS = SUBLANES
OCTETS = S5_GROUPS // 8
OCT_STATES = 8 * S5_STATE
DN_CHUNK = 128
DN_GROUP = 2
SB_TILE = 256
SB_KEY_TILE = 128
SB_PAIR_GROUP = 4
UNDERFLOW_LOG = -87.5
VMEM_LIMIT = 56 * 1024 * 1024


def _largest_div(n, cap, mult):
    best = None
    for d in range(mult, cap + 1, mult):
        if n % d == 0:
            best = d
    assert best is not None, (n, cap, mult)
    return best


def _cparams(*sem):
    return pltpu.CompilerParams(dimension_semantics=sem, vmem_limit_bytes=VMEM_LIMIT)


def _rms(x, g):
    ms = jnp.mean(x * x, axis=-1, keepdims=True)
    return x * lax.rsqrt(ms + EPS) * g


def _softplus(x):
    return jnp.maximum(x, 0.0) + jnp.log1p(jnp.exp(-jnp.abs(x)))


def _sigmoid(x):
    return 1.0 / (1.0 + jnp.exp(-x))


def _silu(x):
    return x * _sigmoid(x)


def _split3(x):
    hi = x.astype(BF16)
    r1 = x - hi.astype(F32)
    mid = r1.astype(BF16)
    lo = (r1 - mid.astype(F32)).astype(BF16)
    return hi, mid, lo


def _dot(a, b):
    return jnp.dot(a, b, preferred_element_type=F32)


def _dot_nt(a, b):
    return lax.dot_general(a, b, (((1,), (1,)), ((), ())), preferred_element_type=F32)


def _dot_tn(a, b):
    return lax.dot_general(a, b, (((0,), (0,)), ((), ())), preferred_element_type=F32)


def _const_spec(shape):
    nd = len(shape)
    return pl.BlockSpec(shape, lambda *_: (0,) * nd)


def _x_window_spec(seq, a0, tile):
    assert a0 % 16 == 0 and tile % 16 == 0 and seq % 16 == 0 and N_META % 16 == 0
    return pl.BlockSpec(
        (pl.Element(tile), pl.Element(D_MODEL)),
        lambda i: (pl.multiple_of(jnp.clip(a0 + i * tile - N_META, 0, seq - tile), 16), 0))


def _seq_tile(xb, meta_ref, i, a0, tile, steps, seq):
    l = N_META + seq
    out = xb
    if a0 < N_META:
        assert a0 == 0 and tile > N_META
        head = jnp.concatenate([meta_ref[...].astype(xb.dtype), xb[:tile - N_META]], axis=0)
        out = jnp.where(i == 0, head, out)
    n_full = min(steps, max(0, (l - a0) // tile))
    rem = min(tile, max(0, l - a0 - n_full * tile))
    pad_from = n_full
    if n_full < steps and rem:
        assert n_full > 0 or a0 >= N_META
        cut = jnp.concatenate([xb[tile - rem:], jnp.zeros((tile - rem, D_MODEL), xb.dtype)], axis=0)
        out = jnp.where(i == n_full, cut, out)
        pad_from += 1
    if pad_from < steps:
        out = jnp.where(i >= pad_from, jnp.zeros_like(out), out)
    return out


def _even_in_kernel(*refs, tr, ls, seq):
    if seq is None:
        hs_ref, g_ref, w_ref, qkv_ref, *u_refs = refs
        x = hs_ref[...].reshape(STREAMS * tr, D_MODEL)
    else:
        x_refs, (meta_ref, g_ref, w_ref, qkv_ref, *u_refs) = refs[:STREAMS], refs[STREAMS:]
        x = jnp.concatenate([_seq_tile(x_refs[r][...], meta_ref, pl.program_id(0), r * ls, tr, ls // tr, seq)
                             for r in range(STREAMS)], axis=0)
    hn = _rms(x, g_ref[...]).astype(BF16)
    p = _dot(hn, w_ref[...])
    q = p[:, :SB_WIDTH] * (SB_HEAD_DIM ** -0.5)
    qkv_ref[:, :, :SB_WIDTH] = q.reshape(STREAMS, tr, SB_WIDTH).astype(BF16)
    qkv_ref[:, :, SB_WIDTH:] = p[:, SB_WIDTH:3 * SB_WIDTH].reshape(STREAMS, tr, 2 * SB_WIDTH).astype(BF16)
    for o, u_ref in enumerate(u_refs):
        c0 = 3 * SB_WIDTH + o * LANES
        for r in range(STREAMS):
            u_ref[pl.ds(r, tr, stride=STREAMS), :] = p[r * tr:(r + 1) * tr, c0:c0 + LANES]


def _even_in(hs, g, w, tr, lp):
    ls = lp // STREAMS
    if isinstance(hs, tuple):
        x, meta = hs
        seq = x.shape[0]
        acts = [x] * STREAMS + [meta]
        act_specs = [_x_window_spec(seq, r * ls, tr) for r in range(STREAMS)] + [_const_spec(meta.shape)]
    else:
        seq = None
        acts = [hs.reshape(STREAMS, ls, D_MODEL)]
        act_specs = [pl.BlockSpec((STREAMS, tr, D_MODEL), lambda i: (0, i, 0))]
    qkv, *us = pl.pallas_call(
        functools.partial(_even_in_kernel, tr=tr, ls=ls, seq=seq),
        grid=(ls // tr,),
        in_specs=act_specs + [_const_spec((1, D_MODEL)), _const_spec(w.shape)],
        out_specs=[pl.BlockSpec((STREAMS, tr, 3 * SB_WIDTH), lambda i: (0, i, 0))]
                  + [pl.BlockSpec((STREAMS * tr, LANES), lambda i: (i, 0))] * OCTETS,
        out_shape=[jax.ShapeDtypeStruct((STREAMS, ls, 3 * SB_WIDTH), BF16)]
                  + [jax.ShapeDtypeStruct((lp, LANES), F32)] * OCTETS,
        compiler_params=_cparams("parallel"),
        name="even_in",
    )(*acts, g, w)
    return qkv.reshape(lp, 3 * SB_WIDTH), us


def _sb_kernel(q_ref, k_ref, v_ref, o_ref, acc_ref, car_ref, *, tq):
    tk = SB_KEY_TILE
    npair = SB_WIDTH // LANES
    pairs = range(npair)
    i = pl.program_id(0)
    lane = lax.broadcasted_iota(jnp.int32, (1, LANES), 1)
    qs = []
    for p in pairs:
        q2 = q_ref[:, p * LANES:(p + 1) * LANES]
        zq = jnp.zeros_like(q2)
        qs.append((jnp.where(lane < SB_HEAD_DIM, q2, zq), jnp.where(lane >= SB_HEAD_DIM, q2, zq)))
    jr = lax.broadcasted_iota(jnp.int32, (tk, tk), 0)
    jc = lax.broadcasted_iota(jnp.int32, (tk, tk), 1)
    later = jnp.where(jr > jc, 1.0, 0.0).astype(BF16)
    acc_ref[...] = jnp.zeros_like(acc_ref)
    car_ref[...] = jnp.zeros_like(car_ref)

    def key_block(j, r0, r1, diagonal):
        rows = slice(r0, r1)
        nr = r1 - r0
        off = pl.multiple_of(j * tk, tk)
        if diagonal:
            col = j * tk + lax.broadcasted_iota(jnp.int32, (1, tk), 1)
            row = i * tq + r0 + lax.broadcasted_iota(jnp.int32, (nr, 1), 0)
            valid = col < row
            valid = jnp.concatenate([valid, valid], axis=0)
            keep = lambda a: jnp.where(valid, a, 0.0)
        else:
            keep = lambda a: a
        for g0 in range(0, npair, SB_PAIR_GROUP):
            grp = range(g0, g0 + SB_PAIR_GROUP)
            z = {p: _dot_nt(jnp.concatenate([qs[p][0][rows], qs[p][1][rows]], axis=0),
                            k_ref[pl.ds(off, tk), p * LANES:(p + 1) * LANES]) for p in grp}
            lk = {p: keep(-(jnp.maximum(z[p], 0.0) + jnp.log(1.0 + jnp.exp(-jnp.abs(z[p]))))) for p in grp}
            hi = {p: lk[p].astype(BF16) for p in grp}
            lo = {p: (lk[p] - hi[p].astype(F32)).astype(BF16) for p in grp}
            rc = {p: _dot(jnp.concatenate([hi[p], lo[p]], axis=0), later) for p in grp}
            car = {p: car_ref[p, :, rows].reshape(2 * nr, 1) for p in grp}
            w = {p: keep(jnp.exp(z[p] + lk[p] + car[p] + rc[p][:2 * nr] + rc[p][2 * nr:])).astype(BF16) for p in grp}
            for p in grp:
                acc_ref[p, :, rows] += _dot(w[p], v_ref[pl.ds(off, tk), p * LANES:(p + 1) * LANES]).reshape(
                    2, nr, LANES)
                car_ref[p, :, rows] = (car[p] + jnp.sum(lk[p], axis=1, keepdims=True)).reshape(2, nr, 1)

    def band_live(r1):
        return (jnp.max(car_ref[:, :, r1 - tk:r1]) > UNDERFLOW_LOG).astype(jnp.int32)

    nd = tq // tk
    for d in range(nd - 1, -1, -1):
        key_block(i * nd + d, d * tk, tq, diagonal=True)

    j = i * nd - 1
    for r1 in range(tq, 0, -tk):
        def body(c, r1=r1):
            key_block(c[0], 0, r1, diagonal=False)
            return c[0] - 1, band_live(r1)

        j, _ = lax.while_loop(lambda c: jnp.logical_and(c[0] >= 0, c[1] > 0), body, (j, band_live(r1)))
    for p in pairs:
        o_ref[:, p * LANES:(p + 1) * LANES] = jnp.where(lane < SB_HEAD_DIM, acc_ref[p, 0], acc_ref[p, 1])


def _sb_attention(qkv, tq):
    lp = qkv.shape[0]
    npair = SB_WIDTH // LANES
    resident = lambda col: pl.BlockSpec((lp, SB_WIDTH), lambda i: (0, col), pipeline_mode=pl.Buffered(1))
    return pl.pallas_call(
        functools.partial(_sb_kernel, tq=tq),
        grid=(lp // tq,),
        in_specs=[pl.BlockSpec((tq, SB_WIDTH), lambda i: (i, 0)), resident(1), resident(2)],
        out_specs=pl.BlockSpec((tq, SB_WIDTH), lambda i: (i, 0)),
        out_shape=jax.ShapeDtypeStruct((lp, SB_WIDTH), F32),
        scratch_shapes=[pltpu.VMEM((npair, 2, tq, LANES), F32), pltpu.VMEM((npair, 2, tq, 1), F32)],
        compiler_params=_cparams("parallel"),
        name="sb_attn",
    )(qkv, qkv, qkv)


def _cmul(ar, ai, br, bi):
    return ar * br - ai * bi, ar * bi + ai * br


def _s5_prep_kernel(lre_ref, lim_ref, ldt_ref, bre_ref, bim_ref, tab_ref, bb_ref, *, ls):
    lr = jnp.minimum(lre_ref[...], -1e-4)
    li = lim_ref[...]
    dt = jnp.exp(ldt_ref[...])
    mag = jnp.exp(lr * dt)
    ang = li * dt
    ar, ai = mag * jnp.cos(ang), mag * jnp.sin(ang)
    den = lr * lr + li * li
    nr, ni = ar - 1.0, ai
    cr = (nr * lr + ni * li) / den
    ci = (ni * lr - nr * li) / den
    pr, pi = jnp.ones_like(ar), jnp.zeros_like(ar)
    br, bi = ar, ai
    n = ls
    while n:
        if n & 1:
            pr, pi = _cmul(pr, pi, br, bi)
        n >>= 1
        if n:
            br, bi = _cmul(br, bi, br, bi)
    tab_ref[...] = jnp.concatenate(
        [ar, ai, cr, ci, pr, pi, jnp.zeros_like(ar), jnp.zeros_like(ar)], axis=0)
    for o in range(OCTETS):
        sl = slice(o * OCT_STATES, (o + 1) * OCT_STATES)
        bre, bim = bre_ref[o], bim_ref[o]
        bb_ref[o, :, :OCT_STATES] = (cr[:, sl] * bre - ci[:, sl] * bim).astype(BF16)
        bb_ref[o, :, OCT_STATES:] = (cr[:, sl] * bim + ci[:, sl] * bre).astype(BF16)


def _s5_prep(lam_re, lam_im, log_dt, bre, bim, ls):
    n = S5_GROUPS * S5_STATE
    row = lambda a: a.reshape(1, n)
    ldt = jnp.broadcast_to(log_dt[:, None], (S5_GROUPS, S5_STATE))
    return pl.pallas_call(
        functools.partial(_s5_prep_kernel, ls=ls),
        out_shape=[jax.ShapeDtypeStruct((SUBLANES, n), F32),
                   jax.ShapeDtypeStruct((OCTETS, 8 * S5_GROUP, 2 * OCT_STATES), BF16)],
        name="s5_prep",
    )(row(lam_re), row(lam_im), row(ldt), bre, bim)


def _s5_scan_octet(tab_ref, o, bur_ref, bui_ref, xr, xi, t_steps, hist=None):
    sl = slice(o * OCT_STATES, (o + 1) * OCT_STATES)
    ar = jnp.broadcast_to(tab_ref[0:1, sl], (STREAMS, OCT_STATES))
    ai = jnp.broadcast_to(tab_ref[1:2, sl], (STREAMS, OCT_STATES))

    for t in range(t_steps):
        rows = slice(t * STREAMS, (t + 1) * STREAMS)
        xr, xi = ar * xr - ai * xi + bur_ref[o, rows, :], ar * xi + ai * xr + bui_ref[o, rows, :]
        if hist is not None:
            hist[0][o, rows, :] = xr
            hist[1][o, rows, :] = xi
    return xr, xi


def _s5_pass_a_kernel(u0_ref, u1_ref, u2_ref, u3_ref, tab_ref, bb_ref, x0_ref,
                      xr_ref, xi_ref, bur_ref, bui_ref, *, t_steps):
    u_refs = (u0_ref, u1_ref, u2_ref, u3_ref)
    i = pl.program_id(0)

    @pl.when(i == 0)
    def _():
        xr_ref[...] = jnp.zeros_like(xr_ref)
        xi_ref[...] = jnp.zeros_like(xi_ref)

    for o in range(OCTETS):
        ub = u_refs[o][...].astype(BF16)
        bur_ref[o] = _dot(ub, bb_ref[o, :, :OCT_STATES])
        bui_ref[o] = _dot(ub, bb_ref[o, :, OCT_STATES:])
        xr, xi = _s5_scan_octet(tab_ref, o, bur_ref, bui_ref, xr_ref[o], xi_ref[o], t_steps)
        xr_ref[o] = xr
        xi_ref[o] = xi

    @pl.when(i == pl.num_programs(0) - 1)
    def _():
        sub = lax.broadcasted_iota(jnp.int32, (STREAMS, OCT_STATES), 0)
        for o in range(OCTETS):
            sl = slice(o * OCT_STATES, (o + 1) * OCT_STATES)
            pr, pi = tab_ref[4:5, sl], tab_ref[5:6, sl]
            er, ei = xr_ref[o], xi_ref[o]
            x0r = jnp.zeros((STREAMS, OCT_STATES), F32)
            x0i = jnp.zeros((STREAMS, OCT_STATES), F32)
            cr = jnp.zeros((1, OCT_STATES), F32)
            ci = jnp.zeros((1, OCT_STATES), F32)
            for r in range(1, STREAMS):
                mr, mi = _cmul(pr, pi, cr, ci)
                cr, ci = er[r - 1:r] + mr, ei[r - 1:r] + mi
                x0r = jnp.where(sub == r, cr, x0r)
                x0i = jnp.where(sub == r, ci, x0i)
            x0_ref[:, 2 * o * OCT_STATES:(2 * o + 1) * OCT_STATES] = x0r
            x0_ref[:, (2 * o + 1) * OCT_STATES:(2 * o + 2) * OCT_STATES] = x0i


def _gelu_tanh(x):
    return 0.5 * x * (1.0 + jnp.tanh(0.7978845608028654 * (x + 0.044715 * (x * x * x))))


def _s5_pass_b_kernel(u0_ref, u1_ref, u2_ref, u3_ref, tab_ref, bb_ref, x0_ref, cre_ref, cim_ref, d_ref,
                      wglu_ref, bglu_ref, g_ref, o_ref, xr_ref, xi_ref, bur_ref, bui_ref, hr_ref, hi_ref,
                      y0_ref, y1_ref, y2_ref, y3_ref, *, t_steps):
    u_refs = (u0_ref, u1_ref, u2_ref, u3_ref)
    y_refs = (y0_ref, y1_ref, y2_ref, y3_ref)
    i = pl.program_id(0)

    @pl.when(i == 0)
    def _():
        for o in range(OCTETS):
            xr_ref[o] = x0_ref[:, 2 * o * OCT_STATES:(2 * o + 1) * OCT_STATES]
            xi_ref[o] = x0_ref[:, (2 * o + 1) * OCT_STATES:(2 * o + 2) * OCT_STATES]

    ys = []
    for o in range(OCTETS):
        u = u_refs[o][...]
        bur_ref[o] = _dot(u.astype(BF16), bb_ref[o, :, :OCT_STATES])
        bui_ref[o] = _dot(u.astype(BF16), bb_ref[o, :, OCT_STATES:])
        xr, xi = _s5_scan_octet(tab_ref, o, bur_ref, bui_ref, xr_ref[o], xi_ref[o], t_steps,
                                hist=(hr_ref, hi_ref))
        xr_ref[o] = xr
        xi_ref[o] = xi
        ys.append(_dot(hr_ref[o].astype(BF16), cre_ref[o]) - _dot(hi_ref[o].astype(BF16), cim_ref[o])
                  + d_ref[:, o * LANES:(o + 1) * LANES] * u)

    hact = _gelu_tanh(jnp.concatenate(ys, axis=1))
    gate = _sigmoid(_dot(hact.astype(BF16), wglu_ref[...]) + bglu_ref[...])
    res = _rms(hact * gate, g_ref[...])
    for o in range(OCTETS):
        y_refs[o][...] = res[:, o * LANES:(o + 1) * LANES]
        for r in range(STREAMS):
            o_ref[r, :, o * LANES:(o + 1) * LANES] = y_refs[o][pl.ds(r, t_steps, stride=STREAMS), :]


def _s5_block_diag_b(b):
    bt = b.reshape(OCTETS, 8, S5_STATE, S5_GROUP).transpose(0, 1, 3, 2)
    eye = jnp.eye(8, dtype=b.dtype)
    return jnp.einsum('ogpn,gh->ogphn', bt, eye).reshape(OCTETS, 8 * S5_GROUP, OCT_STATES)


def _s5_block_diag_c(c):
    ct = c.reshape(OCTETS, 8, S5_GROUP, S5_STATE).transpose(0, 1, 3, 2)
    eye = jnp.eye(8, dtype=c.dtype)
    return jnp.einsum('ognp,gh->ognhp', ct, eye).reshape(OCTETS, OCT_STATES, 8 * S5_GROUP)


def _s5(us, lam_re, lam_im, log_dt, b_re, b_im, c_re, c_im, d_skip, w_glu, b_glu, norm_g, t_steps):
    rows = us[0].shape[0]
    ls = rows // STREAMS
    tile = STREAMS * t_steps
    nsteps = ls // t_steps
    tab, bb = _s5_prep(lam_re, lam_im, log_dt, _s5_block_diag_b(b_re), _s5_block_diag_b(b_im), ls)
    cre, cim = _s5_block_diag_c(c_re).astype(BF16), _s5_block_diag_c(c_im).astype(BF16)
    nstate = S5_GROUPS * S5_STATE
    state_scratch = [pltpu.VMEM((OCTETS, STREAMS, OCT_STATES), F32)] * 2
    bu_scratch = [pltpu.VMEM((OCTETS, tile, OCT_STATES), F32)] * 2
    u_specs = [pl.BlockSpec((tile, LANES), lambda i: (i, 0))] * OCTETS

    x0 = pl.pallas_call(
        functools.partial(_s5_pass_a_kernel, t_steps=t_steps),
        grid=(nsteps,),
        in_specs=u_specs + [_const_spec(tab.shape), _const_spec(bb.shape)],
        out_specs=_const_spec((STREAMS, 2 * nstate)),
        out_shape=jax.ShapeDtypeStruct((STREAMS, 2 * nstate), F32),
        scratch_shapes=state_scratch + bu_scratch,
        compiler_params=_cparams("arbitrary"),
        name="s5_pass_a",
    )(*us, tab, bb)

    return pl.pallas_call(
        functools.partial(_s5_pass_b_kernel, t_steps=t_steps),
        grid=(nsteps,),
        in_specs=u_specs + [_const_spec(tab.shape), _const_spec(bb.shape),
                            _const_spec(x0.shape), _const_spec(cre.shape), _const_spec(cim.shape),
                            _const_spec((1, S5_WIDTH)), _const_spec(w_glu.shape), _const_spec((1, S5_WIDTH)),
                            _const_spec((1, S5_WIDTH))],
        out_specs=pl.BlockSpec((STREAMS, t_steps, S5_WIDTH), lambda i: (0, i, 0)),
        out_shape=jax.ShapeDtypeStruct((STREAMS, ls, S5_WIDTH), F32),
        scratch_shapes=state_scratch + bu_scratch + bu_scratch + [pltpu.VMEM((tile, LANES), F32)] * OCTETS,
        compiler_params=_cparams("arbitrary"),
        name="s5_pass_b",
    )(*us, tab, bb, x0, cre, cim, d_skip.reshape(1, -1), w_glu.astype(BF16), b_glu.reshape(1, -1),
      norm_g.reshape(1, -1))


def _mlp_block(hs, gpre_ref, w1_ref, w2_ref, gpost_ref):
    hn = _rms(hs, gpre_ref[...]).astype(BF16)
    acc = jnp.zeros(hs.shape, F32)
    for f in range(D_FF // FF_TILE):
        a = jnp.maximum(_dot(hn, w1_ref[:, f * FF_TILE:(f + 1) * FF_TILE]), 0.0)
        acc = acc + _dot((a * a).astype(BF16), w2_ref[f * FF_TILE:(f + 1) * FF_TILE, :])
    return hs + _rms(acc, gpost_ref[...])


def _resident(shape):
    nd = len(shape)
    return pl.BlockSpec(shape, lambda *_: (0,) * nd, pipeline_mode=pl.Buffered(1))


def _even_out_kernel(osb_ref, os5_ref, *refs, seq_tile):
    if seq_tile is None:
        hs_ref, *refs = refs
        hs = hs_ref[...]
    else:
        x_ref, meta_ref, *refs = refs
        hs = _seq_tile(x_ref[...], meta_ref, pl.program_id(0), *seq_tile)
    gsb_ref, wsb_ref, ws5_ref, gpost_ref, gpre_mlp_ref, w1_ref, w2_ref, gpost_mlp_ref, o_ref = refs
    sb = _rms(osb_ref[...], gsb_ref[...]).astype(BF16)
    mix = _dot(sb, wsb_ref[...]) + _dot(os5_ref[...].astype(BF16), ws5_ref[...])
    hs = hs + _rms(mix, gpost_ref[...])
    o_ref[...] = _mlp_block(hs, gpre_mlp_ref, w1_ref, w2_ref, gpost_mlp_ref)


def _row_tiles(lp, keep):
    if keep is None:
        tm = _largest_div(lp, 640, 16)
        return tm, lp // tm, None
    first, count = keep
    tm = _largest_div(count, 640, 16)
    return tm, count // tm, first


def _rows_spec(shape, tm, first, row_axis=0):
    nd = len(shape)
    if first is None:
        block = tuple(tm if a == row_axis else s for a, s in enumerate(shape))
        return pl.BlockSpec(block, lambda i: tuple(i if a == row_axis else 0 for a in range(nd)))
    assert first % 16 == 0 and tm % 16 == 0
    block = tuple(pl.Element(tm if a == row_axis else s) for a, s in enumerate(shape))
    return pl.BlockSpec(block, lambda i: tuple(pl.multiple_of(first + i * tm, 16) if a == row_axis else 0
                                               for a in range(nd)))


def _even_out(o_sb, o_s5, hs, g_sb, w_out, g_post, g_pre_mlp, w1, w2, g_post_mlp, keep=None):
    lp = o_sb.shape[0]
    w_sb, w_s5 = w_out[:SB_WIDTH], w_out[SB_WIDTH:]
    tm, steps, first = _row_tiles(lp, keep)
    if isinstance(hs, tuple):
        x, meta = hs
        seq_tile = (first or 0, tm, steps, x.shape[0])
        res, res_specs = [x, meta], [_x_window_spec(x.shape[0], first or 0, tm), _const_spec(meta.shape)]
    else:
        seq_tile = None
        res, res_specs = [hs], [_rows_spec(hs.shape, tm, first)]
    return pl.pallas_call(
        functools.partial(_even_out_kernel, seq_tile=seq_tile),
        grid=(steps,),
        in_specs=[_rows_spec(o_sb.shape, tm, first), _rows_spec(o_s5.shape, tm, first), *res_specs,
                  _const_spec((1, SB_WIDTH)), _resident(w_sb.shape), _resident(w_s5.shape),
                  _const_spec((1, D_MODEL)),
                  _const_spec((1, D_MODEL)), _resident(w1.shape), _resident(w2.shape), _const_spec((1, D_MODEL))],
        out_specs=pl.BlockSpec((tm, D_MODEL), lambda i: (i, 0)),
        out_shape=jax.ShapeDtypeStruct((steps * tm, D_MODEL), F32),
        compiler_params=_cparams("parallel"),
        name="even_out_mlp",
    )(o_sb, o_s5, *res, g_sb, w_sb, w_s5, g_post, g_pre_mlp, w1, w2, g_post_mlp)


def _chunk_tri(n):
    r = lax.broadcasted_iota(jnp.int32, (n, n), 0)
    c = lax.broadcasted_iota(jnp.int32, (n, n), 1)
    same = (r // DN_CHUNK) == (c // DN_CHUNK)
    return jnp.where(jnp.logical_and(same, r >= c), 1.0, 0.0).astype(BF16)


def _odd_in_kernel(hs_ref, g_ref, wqkv_ref, wz_ref, wabc_ref, conv_ref, alc_ref, dtc_ref,
                   q_ref, k_ref, v_ref, z_ref, gcol_ref, grow_ref, carry_ref, *, tm):
    @pl.when(pl.program_id(0) == 0)
    def _():
        carry_ref[...] = jnp.zeros_like(carry_ref)

    hn = _rms(hs_ref[...], g_ref[...]).astype(BF16)

    ab = _dot(hn, wabc_ref[...])
    lane = lax.broadcasted_iota(jnp.int32, (1, LANES), 1)
    gb = jnp.where(lane < DN_HEADS, -jnp.exp(alc_ref[...]) * _softplus(ab + dtc_ref[...]), _sigmoid(ab))
    tri = _chunk_tri(tm)
    hi, mid, lo = _split3(gb)
    gsum = _dot(tri, hi) + _dot(tri, mid) + _dot(tri, lo)
    gcum = jnp.where(lane < DN_HEADS, gsum, gb)
    gcol_ref[...] = gcum[:, :2 * DN_HEADS]
    grow_ref[...] = gcum.T[:2 * DN_HEADS]

    first = lax.broadcasted_iota(jnp.int32, (SUBLANES, 1), 0)
    w = DN_HEADS * DN_HEAD_DIM
    hd = DN_HEAD_DIM

    def conv_silu(c0):
        cols = slice(c0, c0 + 2 * hd)
        x = _dot(hn, wqkv_ref[:, cols])
        prev = carry_ref[:, cols]
        carry_ref[:, cols] = x[tm - SUBLANES:, :]
        y = x * conv_ref[DN_CONV - 1:DN_CONV, cols]
        for s in range(1, DN_CONV):
            sh = pltpu.roll(x, s, 0)
            head = jnp.where(first < s, pltpu.roll(prev, s, 0), sh[:SUBLANES])
            sh = jnp.concatenate([head, sh[SUBLANES:]], axis=0)
            y = y + sh * conv_ref[DN_CONV - 1 - s:DN_CONV - s, cols]
        return _silu(y)

    for h in range(0, DN_HEADS, 2):
        q2 = conv_silu(h * hd)
        k2 = conv_silu(w + h * hd)
        v2 = conv_silu(2 * w + h * hd)
        z2 = _dot(hn, wz_ref[:, h * hd:(h + 2) * hd])
        for d in range(2):
            sl = slice(d * hd, (d + 1) * hd)
            qh, kh = q2[:, sl], k2[:, sl]
            q_ref[h + d] = (qh * (lax.rsqrt(jnp.sum(qh * qh, axis=-1, keepdims=True) + EPS)
                                  * (hd ** -0.5))).astype(BF16)
            k_ref[h + d] = (kh * lax.rsqrt(jnp.sum(kh * kh, axis=-1, keepdims=True) + EPS)).astype(BF16)
            v_ref[h + d] = v2[:, sl].astype(BF16)
            z_ref[h + d] = z2[:, sl].astype(BF16)


def _odd_in(hs, g, w_in, conv_w, a_log, dt_bias, tm):
    lp = hs.shape[0]
    w = DN_HEADS * DN_HEAD_DIM
    wqkv = w_in[:, :3 * w].astype(BF16)
    wz = w_in[:, 3 * w:4 * w].astype(BF16)
    wab = w_in[:, 4 * w:]
    wabc = jnp.pad(wab, ((0, 0), (0, LANES - 2 * DN_HEADS))).astype(BF16)
    pad_row = lambda a: jnp.pad(a.reshape(1, -1), ((0, 0), (0, LANES - DN_HEADS)))
    row_spec = lambda width: pl.BlockSpec((tm, width), lambda i: (i, 0))
    act = jax.ShapeDtypeStruct((DN_HEADS, lp, DN_HEAD_DIM), BF16)
    act_spec = pl.BlockSpec((DN_HEADS, tm, DN_HEAD_DIM), lambda i: (0, i, 0))
    return pl.pallas_call(
        functools.partial(_odd_in_kernel, tm=tm),
        grid=(lp // tm,),
        in_specs=[row_spec(D_MODEL), _const_spec((1, D_MODEL)), _const_spec(wqkv.shape), _const_spec(wz.shape),
                  _const_spec(wabc.shape), _const_spec(conv_w.shape),
                  _const_spec((1, LANES)), _const_spec((1, LANES))],
        out_specs=[act_spec, act_spec, act_spec, act_spec, row_spec(2 * DN_HEADS),
                   pl.BlockSpec((2 * DN_HEADS, tm), lambda i: (0, i))],
        out_shape=[act, act, act, act,
                   jax.ShapeDtypeStruct((lp, 2 * DN_HEADS), F32),
                   jax.ShapeDtypeStruct((2 * DN_HEADS, lp), F32)],
        scratch_shapes=[pltpu.VMEM((SUBLANES, 3 * w), F32)],
        compiler_params=_cparams("arbitrary"),
        name="odd_in",
    )(hs, g, wqkv, wz, wabc, conv_w, pad_row(a_log), pad_row(dt_bias))


def _bdot(a, b):
    return lax.dot_general(a, b, (((2,), (1,)), ((0,), (0,))), preferred_element_type=F32)


def _bdot_nt(a, b):
    return lax.dot_general(a, b, (((2,), (2,)), ((0,), (0,))), preferred_element_type=F32)


def _bdot_tn(a, b):
    return lax.dot_general(a, b, (((1,), (1,)), ((0,), (0,))), preferred_element_type=F32)


def _dn_kernel(q_ref, k_ref, v_ref, z_ref, gcol_ref, grow_ref, ng_ref, o_ref,
               s_ref, u_scr, wq_scr, a_scr, kt_scr, egl_scr, *, chunks):
    c64, nh, hd, grp = DN_CHUNK, DN_HEADS, DN_HEAD_DIM, DN_GROUP
    nb = nh * grp

    @pl.when(pl.program_id(0) == 0)
    def _():
        s_ref[...] = jnp.zeros_like(s_ref)

    ri = lax.broadcasted_iota(jnp.int32, (c64, c64), 0)
    ci = lax.broadcasted_iota(jnp.int32, (c64, c64), 1)
    incl = ri >= ci
    strict = ri > ci
    eye = jnp.where(ri == ci, 1.0, 0.0)
    pair_masks = []
    s = 1
    while s < c64:
        pair_masks.append(jnp.logical_and((ri // s) % 2 == 1, (ri // s) - 1 == ci // s))
        s *= 2
    ng = ng_ref[...]

    def prep(gi, carry):
        c0 = gi * grp
        rows = pl.ds(pl.multiple_of(c0 * c64, grp * c64), grp * c64)
        cs = pl.ds(c0, grp)
        qb = q_ref[:, rows, :].reshape(nb, c64, hd)
        kb = k_ref[:, rows, :].reshape(nb, c64, hd)
        vb = v_ref[:, rows, :].reshape(nb, c64, hd)
        kf, vf = kb.astype(F32), vb.astype(F32)
        gcb = gcol_ref[rows, :]
        gc = jnp.stack([gcb[:, h:h + 1] for h in range(nh)], axis=0).reshape(nb, c64, 1)
        be = jnp.stack([gcb[:, nh + h:nh + h + 1] for h in range(nh)], axis=0).reshape(nb, c64, 1)
        gr = grow_ref[0:nh, cs].reshape(nb, 1, c64)
        gl = gc[:, c64 - 1:c64, :]
        eg = jnp.exp(gc)
        decay = jnp.exp(jnp.where(incl, gc - gr, -jnp.inf))
        qkk = _bdot_nt(jnp.concatenate([qb, kb], axis=1), kb)
        n = jnp.where(strict, be * qkk[:, c64:] * decay, 0.0)
        x = eye - jnp.where(pair_masks[0], n, 0.0)
        for m in pair_masks[1:]:
            xb = x.astype(BF16)
            x = x - _bdot(xb, _bdot(jnp.where(m, n, 0.0).astype(BF16), xb).astype(BF16))
        rhs = jnp.concatenate([(vf * be).astype(BF16), (kf * (be * eg)).astype(BF16)], axis=2)
        uw = _bdot(x.astype(BF16), rhs)
        qg = (qb.astype(F32) * eg).astype(BF16)
        u_scr[:, cs] = uw[:, :, :hd].reshape(nh, grp, c64, hd)
        wq_scr[:, cs] = jnp.concatenate([uw[:, :, hd:].astype(BF16), qg], axis=1).reshape(nh, grp, 2 * c64, hd)
        a_scr[:, cs] = (qkk[:, :c64] * decay).astype(BF16).reshape(nh, grp, c64, c64)
        kt_scr[:, cs] = (kf * jnp.exp(gl - gc)).astype(BF16).reshape(nh, grp, c64, hd)
        egl_scr[:, cs] = jnp.broadcast_to(jnp.exp(gl), (nb, 1, hd)).reshape(nh, grp, 1, hd)
        return carry

    lax.fori_loop(0, chunks // grp, prep, 0)

    def step(c, carry):
        rows = pl.ds(pl.multiple_of(c * c64, c64), c64)
        s = s_ref[...]
        ws = _bdot(wq_scr[:, c], s.astype(BF16))
        vnb = (u_scr[:, c] - ws[:, :c64]).astype(BF16)
        o = ws[:, c64:] + _bdot(a_scr[:, c], vnb)
        s_ref[...] = s * egl_scr[:, c] + _bdot_tn(kt_scr[:, c], vnb)
        on = o * lax.rsqrt(jnp.mean(o * o, axis=-1, keepdims=True) + EPS) * ng
        o_ref[:, rows, :] = (on * _silu(z_ref[:, rows, :].astype(F32))).astype(BF16)
        return carry

    lax.fori_loop(0, chunks, step, 0)


def _dn(q, k, v, z, gcol, grow, norm_g, chunks):
    nh, lp, hd = q.shape
    rows = chunks * DN_CHUNK
    grow4 = grow.reshape(2 * nh, lp // DN_CHUNK, 1, DN_CHUNK)
    act_spec = pl.BlockSpec((nh, rows, hd), lambda i: (0, i, 0))
    return pl.pallas_call(
        functools.partial(_dn_kernel, chunks=chunks),
        grid=(lp // rows,),
        in_specs=[act_spec, act_spec, act_spec, act_spec,
                  pl.BlockSpec((rows, 2 * nh), lambda i: (i, 0)),
                  pl.BlockSpec((2 * nh, chunks, 1, DN_CHUNK), lambda i: (0, i, 0, 0)),
                  _const_spec((1, hd))],
        out_specs=act_spec,
        out_shape=jax.ShapeDtypeStruct((nh, lp, hd), BF16),
        scratch_shapes=[pltpu.VMEM((nh, hd, hd), F32),
                        pltpu.VMEM((nh, chunks, DN_CHUNK, hd), F32),
                        pltpu.VMEM((nh, chunks, 2 * DN_CHUNK, hd), BF16),
                        pltpu.VMEM((nh, chunks, DN_CHUNK, DN_CHUNK), BF16),
                        pltpu.VMEM((nh, chunks, DN_CHUNK, hd), BF16),
                        pltpu.VMEM((nh, chunks, 1, hd), F32)],
        compiler_params=_cparams("arbitrary"),
        name="dn",
    )(q, k, v, z, gcol, grow4, norm_g.reshape(1, -1))


def _odd_out_kernel(og_ref, hs_ref, w_ref, gpost_ref, gpre_mlp_ref, w1_ref, w2_ref, gpost_mlp_ref, o_ref):
    og = jnp.concatenate([og_ref[h] for h in range(DN_HEADS)], axis=1)
    hs = hs_ref[...] + _rms(_dot(og, w_ref[...]), gpost_ref[...])
    o_ref[...] = _mlp_block(hs, gpre_mlp_ref, w1_ref, w2_ref, gpost_mlp_ref)


def _odd_out(og, hs, w_out, g_post, g_pre_mlp, w1, w2, g_post_mlp, keep=None):
    lp = hs.shape[0]
    tm, steps, first = _row_tiles(lp, keep)
    return pl.pallas_call(
        _odd_out_kernel,
        grid=(steps,),
        in_specs=[_rows_spec(og.shape, tm, first, row_axis=1), _rows_spec(hs.shape, tm, first),
                  _resident(w_out.shape), _const_spec((1, D_MODEL)),
                  _const_spec((1, D_MODEL)), _resident(w1.shape), _resident(w2.shape), _const_spec((1, D_MODEL))],
        out_specs=pl.BlockSpec((tm, D_MODEL), lambda i: (i, 0)),
        out_shape=jax.ShapeDtypeStruct((steps * tm, D_MODEL), F32),
        compiler_params=_cparams("parallel"),
        name="odd_out_mlp",
    )(og, hs, w_out, g_post, g_pre_mlp, w1, w2, g_post_mlp)


def kernel(x, meta_tokens, pre_mix_norm, post_mix_norm, pre_mlp_norm, post_mlp_norm, mlp_w1, mlp_w2, w_in_even, w_out_even, sb_out_norm, s5_lambda_re, s5_lambda_im, s5_log_dt, s5_b_re, s5_b_im, s5_c_re, s5_c_im, s5_d, s5_w_glu, s5_b_glu, s5_out_norm, w_in_odd, dn_conv_w, dn_a_log, dn_dt_bias, dn_out_norm, w_out_odd):
    assert x.shape[0] == 1 and x.shape[2] == D_MODEL
    depth = pre_mix_norm.shape[0]
    l = N_META + x.shape[1]
    lp = -(-l // SB_TILE) * SB_TILE
    ls = lp // STREAMS
    tr_even = _largest_div(ls, 104, 16)
    t_s5 = _largest_div(ls, 128, SUBLANES)
    dn_chunks = _largest_div(lp // DN_CHUNK, 10, DN_GROUP)
    row = lambda a: a.reshape(1, -1)

    hs = (x[0], meta_tokens)
    for i in range(depth):
        j = i // 2
        mlp = (row(pre_mlp_norm[i]), mlp_w1[i].astype(BF16), mlp_w2[i].astype(BF16), row(post_mlp_norm[i]))
        keep = (N_META, l - N_META) if i == depth - 1 else None
        if i % 2 == 0:
            qkv, u_il = _even_in(hs, row(pre_mix_norm[i]), w_in_even[j].astype(BF16), tr_even, lp)
            o_sb = _sb_attention(qkv, SB_TILE)
            o_s5 = _s5(u_il, s5_lambda_re[j], s5_lambda_im[j], s5_log_dt[j],
                       s5_b_re[j], s5_b_im[j], s5_c_re[j], s5_c_im[j], s5_d[j], s5_w_glu[j], s5_b_glu[j],
                       s5_out_norm[j], t_s5)
            hs = _even_out(o_sb, o_s5.reshape(lp, S5_WIDTH), hs, row(sb_out_norm[j]), w_out_even[j].astype(BF16),
                           row(post_mix_norm[i]), *mlp, keep)
        else:
            q, k, v, z, gcol, grow = _odd_in(hs, row(pre_mix_norm[i]), w_in_odd[j], dn_conv_w[j], dn_a_log[j],
                                             dn_dt_bias[j], SB_TILE)
            og = _dn(q, k, v, z, gcol, grow, dn_out_norm[j], dn_chunks)
            hs = _odd_out(og, hs, w_out_odd[j].astype(BF16), row(post_mix_norm[i]), *mlp, keep)
    return hs[None]
```

```python
import functools

import jax
import jax.numpy as jnp
from jax import lax
from jax.experimental import pallas as pl
from jax.experimental.pallas import tpu as pltpu

F32 = jnp.float32
BF16 = jnp.bfloat16

D_MODEL = 1024
N_META = 16
SB_HEAD_DIM = 64
SB_WIDTH = 512
S5_WIDTH = 512
S5_GROUP = 16
S5_GROUPS = 32
S5_STATE = 64
DN_HEAD_DIM = 128
DN_HEADS = 8
DN_CONV = 4
D_FF = 4096
FF_TILE = 1024
EPS = 1e-6

LANES = 128
SUBLANES = 8
STREAMS = SUBLANES
OCTETS = S5_GROUPS // 8
OCT_STATES = 8 * S5_STATE
DN_CHUNK = 128
DN_GROUP = 2
SB_TILE = 256
SB_KEY_TILE = 128
SB_PAIR_GROUP = 4
UNDERFLOW_LOG = -87.5
VMEM_LIMIT = 56 * 1024 * 1024


def _largest_div(n, cap, mult):
    best = None
    for d in range(mult, cap + 1, mult):
        if n % d == 0:
            best = d
    assert best is not None, (n, cap, mult)
    return best


def _cparams(*sem):
    return pltpu.CompilerParams(dimension_semantics=sem, vmem_limit_bytes=VMEM_LIMIT)


def _rms(x, g):
    ms = jnp.mean(x * x, axis=-1, keepdims=True)
    return x * lax.rsqrt(ms + EPS) * g


def _softplus(x):
    return jnp.maximum(x, 0.0) + jnp.log1p(jnp.exp(-jnp.abs(x)))


def _sigmoid(x):
    return 1.0 / (1.0 + jnp.exp(-x))


def _silu(x):
    return x * _sigmoid(x)


def _split3(x):
    hi = x.astype(BF16)
    r1 = x - hi.astype(F32)
    mid = r1.astype(BF16)
    lo = (r1 - mid.astype(F32)).astype(BF16)
    return hi, mid, lo


def _dot(a, b):
    return jnp.dot(a, b, preferred_element_type=F32)


def _dot_nt(a, b):
    return lax.dot_general(a, b, (((1,), (1,)), ((), ())), preferred_element_type=F32)


def _dot_tn(a, b):
    return lax.dot_general(a, b, (((0,), (0,)), ((), ())), preferred_element_type=F32)


def _const_spec(shape):
    nd = len(shape)
    return pl.BlockSpec(shape, lambda *_: (0,) * nd)


def _x_window_spec(seq, a0, tile):
    assert a0 % 16 == 0 and tile % 16 == 0 and seq % 16 == 0 and N_META % 16 == 0
    return pl.BlockSpec(
        (pl.Element(tile), pl.Element(D_MODEL)),
        lambda i: (pl.multiple_of(jnp.clip(a0 + i * tile - N_META, 0, seq - tile), 16), 0))


def _seq_tile(xb, meta_ref, i, a0, tile, steps, seq):
    l = N_META + seq
    out = xb
    if a0 < N_META:
        assert a0 == 0 and tile > N_META
        head = jnp.concatenate([meta_ref[...].astype(xb.dtype), xb[:tile - N_META]], axis=0)
        out = jnp.where(i == 0, head, out)
    n_full = min(steps, max(0, (l - a0) // tile))
    rem = min(tile, max(0, l - a0 - n_full * tile))
    pad_from = n_full
    if n_full < steps and rem:
        assert n_full > 0 or a0 >= N_META
        cut = jnp.concatenate([xb[tile - rem:], jnp.zeros((tile - rem, D_MODEL), xb.dtype)], axis=0)
        out = jnp.where(i == n_full, cut, out)
        pad_from += 1
    if pad_from < steps:
        out = jnp.where(i >= pad_from, jnp.zeros_like(out), out)
    return out


def _even_in_kernel(*refs, tr, ls, seq):
    if seq is None:
        hs_ref, g_ref, w_ref, qkv_ref, *u_refs = refs
        x = hs_ref[...].reshape(STREAMS * tr, D_MODEL)
    else:
        x_refs, (meta_ref, g_ref, w_ref, qkv_ref, *u_refs) = refs[:STREAMS], refs[STREAMS:]
        x = jnp.concatenate([_seq_tile(x_refs[r][...], meta_ref, pl.program_id(0), r * ls, tr, ls // tr, seq)
                             for r in range(STREAMS)], axis=0)
    hn = _rms(x, g_ref[...]).astype(BF16)
    p = _dot(hn, w_ref[...])
    q = p[:, :SB_WIDTH] * (SB_HEAD_DIM ** -0.5)
    qkv_ref[:, :, :SB_WIDTH] = q.reshape(STREAMS, tr, SB_WIDTH).astype(BF16)
    qkv_ref[:, :, SB_WIDTH:] = p[:, SB_WIDTH:3 * SB_WIDTH].reshape(STREAMS, tr, 2 * SB_WIDTH).astype(BF16)
    for o, u_ref in enumerate(u_refs):
        c0 = 3 * SB_WIDTH + o * LANES
        for r in range(STREAMS):
            u_ref[pl.ds(r, tr, stride=STREAMS), :] = p[r * tr:(r + 1) * tr, c0:c0 + LANES]


def _even_in(hs, g, w, tr, lp):
    ls = lp // STREAMS
    if isinstance(hs, tuple):
        x, meta = hs
        seq = x.shape[0]
        acts = [x] * STREAMS + [meta]
        act_specs = [_x_window_spec(seq, r * ls, tr) for r in range(STREAMS)] + [_const_spec(meta.shape)]
    else:
        seq = None
        acts = [hs.reshape(STREAMS, ls, D_MODEL)]
        act_specs = [pl.BlockSpec((STREAMS, tr, D_MODEL), lambda i: (0, i, 0))]
    qkv, *us = pl.pallas_call(
        functools.partial(_even_in_kernel, tr=tr, ls=ls, seq=seq),
        grid=(ls // tr,),
        in_specs=act_specs + [_const_spec((1, D_MODEL)), _const_spec(w.shape)],
        out_specs=[pl.BlockSpec((STREAMS, tr, 3 * SB_WIDTH), lambda i: (0, i, 0))]
                  + [pl.BlockSpec((STREAMS * tr, LANES), lambda i: (i, 0))] * OCTETS,
        out_shape=[jax.ShapeDtypeStruct((STREAMS, ls, 3 * SB_WIDTH), BF16)]
                  + [jax.ShapeDtypeStruct((lp, LANES), F32)] * OCTETS,
        compiler_params=_cparams("parallel"),
        name="even_in",
    )(*acts, g, w)
    return qkv.reshape(lp, 3 * SB_WIDTH), us


def _sb_kernel(q_ref, k_ref, v_ref, o_ref, acc_ref, car_ref, *, tq):
    tk = SB_KEY_TILE
    npair = SB_WIDTH // LANES
    pairs = range(npair)
    i = pl.program_id(0)
    lane = lax.broadcasted_iota(jnp.int32, (1, LANES), 1)
    qs = []
    for p in pairs:
        q2 = q_ref[:, p * LANES:(p + 1) * LANES]
        zq = jnp.zeros_like(q2)
        qs.append((jnp.where(lane < SB_HEAD_DIM, q2, zq), jnp.where(lane >= SB_HEAD_DIM, q2, zq)))
    jr = lax.broadcasted_iota(jnp.int32, (tk, tk), 0)
    jc = lax.broadcasted_iota(jnp.int32, (tk, tk), 1)
    later = jnp.where(jr > jc, 1.0, 0.0).astype(BF16)
    acc_ref[...] = jnp.zeros_like(acc_ref)
    car_ref[...] = jnp.zeros_like(car_ref)

    def key_block(j, r0, r1, diagonal):
        rows = slice(r0, r1)
        nr = r1 - r0
        off = pl.multiple_of(j * tk, tk)
        if diagonal:
            col = j * tk + lax.broadcasted_iota(jnp.int32, (1, tk), 1)
            row = i * tq + r0 + lax.broadcasted_iota(jnp.int32, (nr, 1), 0)
            valid = col < row
            valid = jnp.concatenate([valid, valid], axis=0)
            keep = lambda a: jnp.where(valid, a, 0.0)
        else:
            keep = lambda a: a
        for g0 in range(0, npair, SB_PAIR_GROUP):
            grp = range(g0, g0 + SB_PAIR_GROUP)
            z = {p: _dot_nt(jnp.concatenate([qs[p][0][rows], qs[p][1][rows]], axis=0),
                            k_ref[pl.ds(off, tk), p * LANES:(p + 1) * LANES]) for p in grp}
            lk = {p: keep(-(jnp.maximum(z[p], 0.0) + jnp.log(1.0 + jnp.exp(-jnp.abs(z[p]))))) for p in grp}
            hi = {p: lk[p].astype(BF16) for p in grp}
            lo = {p: (lk[p] - hi[p].astype(F32)).astype(BF16) for p in grp}
            rc = {p: _dot(jnp.concatenate([hi[p], lo[p]], axis=0), later) for p in grp}
            car = {p: car_ref[p, :, rows].reshape(2 * nr, 1) for p in grp}
            w = {p: keep(jnp.exp(z[p] + lk[p] + car[p] + rc[p][:2 * nr] + rc[p][2 * nr:])).astype(BF16) for p in grp}
            for p in grp:
                acc_ref[p, :, rows] += _dot(w[p], v_ref[pl.ds(off, tk), p * LANES:(p + 1) * LANES]).reshape(
                    2, nr, LANES)
                car_ref[p, :, rows] = (car[p] + jnp.sum(lk[p], axis=1, keepdims=True)).reshape(2, nr, 1)

    def band_live(r1):
        return (jnp.max(car_ref[:, :, r1 - tk:r1]) > UNDERFLOW_LOG).astype(jnp.int32)

    nd = tq // tk
    for d in range(nd - 1, -1, -1):
        key_block(i * nd + d, d * tk, tq, diagonal=True)

    j = i * nd - 1
    for r1 in range(tq, 0, -tk):
        def body(c, r1=r1):
            key_block(c[0], 0, r1, diagonal=False)
            return c[0] - 1, band_live(r1)

        j, _ = lax.while_loop(lambda c: jnp.logical_and(c[0] >= 0, c[1] > 0), body, (j, band_live(r1)))
    for p in pairs:
        o_ref[:, p * LANES:(p + 1) * LANES] = jnp.where(lane < SB_HEAD_DIM, acc_ref[p, 0], acc_ref[p, 1])


def _sb_attention(qkv, tq):
    lp = qkv.shape[0]
    npair = SB_WIDTH // LANES
    resident = lambda col: pl.BlockSpec((lp, SB_WIDTH), lambda i: (0, col), pipeline_mode=pl.Buffered(1))
    return pl.pallas_call(
        functools.partial(_sb_kernel, tq=tq),
        grid=(lp // tq,),
        in_specs=[pl.BlockSpec((tq, SB_WIDTH), lambda i: (i, 0)), resident(1), resident(2)],
        out_specs=pl.BlockSpec((tq, SB_WIDTH), lambda i: (i, 0)),
        out_shape=jax.ShapeDtypeStruct((lp, SB_WIDTH), F32),
        scratch_shapes=[pltpu.VMEM((npair, 2, tq, LANES), F32), pltpu.VMEM((npair, 2, tq, 1), F32)],
        compiler_params=_cparams("parallel"),
        name="sb_attn",
    )(qkv, qkv, qkv)


def _cmul(ar, ai, br, bi):
    return ar * br - ai * bi, ar * bi + ai * br


def _s5_prep_kernel(lre_ref, lim_ref, ldt_ref, bre_ref, bim_ref, tab_ref, bb_ref, *, ls):
    lr = jnp.minimum(lre_ref[...], -1e-4)
    li = lim_ref[...]
    dt = jnp.exp(ldt_ref[...])
    mag = jnp.exp(lr * dt)
    ang = li * dt
    ar, ai = mag * jnp.cos(ang), mag * jnp.sin(ang)
    den = lr * lr + li * li
    nr, ni = ar - 1.0, ai
    cr = (nr * lr + ni * li) / den
    ci = (ni * lr - nr * li) / den
    pr, pi = jnp.ones_like(ar), jnp.zeros_like(ar)
    br, bi = ar, ai
    n = ls
    while n:
        if n & 1:
            pr, pi = _cmul(pr, pi, br, bi)
        n >>= 1
        if n:
            br, bi = _cmul(br, bi, br, bi)
    tab_ref[...] = jnp.concatenate(
        [ar, ai, cr, ci, pr, pi, jnp.zeros_like(ar), jnp.zeros_like(ar)], axis=0)
    for o in range(OCTETS):
        sl = slice(o * OCT_STATES, (o + 1) * OCT_STATES)
        bre, bim = bre_ref[o], bim_ref[o]
        bb_ref[o, :, :OCT_STATES] = (cr[:, sl] * bre - ci[:, sl] * bim).astype(BF16)
        bb_ref[o, :, OCT_STATES:] = (cr[:, sl] * bim + ci[:, sl] * bre).astype(BF16)


def _s5_prep(lam_re, lam_im, log_dt, bre, bim, ls):
    n = S5_GROUPS * S5_STATE
    row = lambda a: a.reshape(1, n)
    ldt = jnp.broadcast_to(log_dt[:, None], (S5_GROUPS, S5_STATE))
    return pl.pallas_call(
        functools.partial(_s5_prep_kernel, ls=ls),
        out_shape=[jax.ShapeDtypeStruct((SUBLANES, n), F32),
                   jax.ShapeDtypeStruct((OCTETS, 8 * S5_GROUP, 2 * OCT_STATES), BF16)],
        name="s5_prep",
    )(row(lam_re), row(lam_im), row(ldt), bre, bim)


def _s5_scan_octet(tab_ref, o, bur_ref, bui_ref, xr, xi, t_steps, hist=None):
    sl = slice(o * OCT_STATES, (o + 1) * OCT_STATES)
    ar = jnp.broadcast_to(tab_ref[0:1, sl], (STREAMS, OCT_STATES))
    ai = jnp.broadcast_to(tab_ref[1:2, sl], (STREAMS, OCT_STATES))

    for t in range(t_steps):
        rows = slice(t * STREAMS, (t + 1) * STREAMS)
        xr, xi = ar * xr - ai * xi + bur_ref[o, rows, :], ar * xi + ai * xr + bui_ref[o, rows, :]
        if hist is not None:
            hist[0][o, rows, :] = xr
            hist[1][o, rows, :] = xi
    return xr, xi


def _s5_pass_a_kernel(u0_ref, u1_ref, u2_ref, u3_ref, tab_ref, bb_ref, x0_ref,
                      xr_ref, xi_ref, bur_ref, bui_ref, *, t_steps):
    u_refs = (u0_ref, u1_ref, u2_ref, u3_ref)
    i = pl.program_id(0)

    @pl.when(i == 0)
    def _():
        xr_ref[...] = jnp.zeros_like(xr_ref)
        xi_ref[...] = jnp.zeros_like(xi_ref)

    for o in range(OCTETS):
        ub = u_refs[o][...].astype(BF16)
        bur_ref[o] = _dot(ub, bb_ref[o, :, :OCT_STATES])
        bui_ref[o] = _dot(ub, bb_ref[o, :, OCT_STATES:])
        xr, xi = _s5_scan_octet(tab_ref, o, bur_ref, bui_ref, xr_ref[o], xi_ref[o], t_steps)
        xr_ref[o] = xr
        xi_ref[o] = xi

    @pl.when(i == pl.num_programs(0) - 1)
    def _():
        sub = lax.broadcasted_iota(jnp.int32, (STREAMS, OCT_STATES), 0)
        for o in range(OCTETS):
            sl = slice(o * OCT_STATES, (o + 1) * OCT_STATES)
            pr, pi = tab_ref[4:5, sl], tab_ref[5:6, sl]
            er, ei = xr_ref[o], xi_ref[o]
            x0r = jnp.zeros((STREAMS, OCT_STATES), F32)
            x0i = jnp.zeros((STREAMS, OCT_STATES), F32)
            cr = jnp.zeros((1, OCT_STATES), F32)
            ci = jnp.zeros((1, OCT_STATES), F32)
            for r in range(1, STREAMS):
                mr, mi = _cmul(pr, pi, cr, ci)
                cr, ci = er[r - 1:r] + mr, ei[r - 1:r] + mi
                x0r = jnp.where(sub == r, cr, x0r)
                x0i = jnp.where(sub == r, ci, x0i)
            x0_ref[:, 2 * o * OCT_STATES:(2 * o + 1) * OCT_STATES] = x0r
            x0_ref[:, (2 * o + 1) * OCT_STATES:(2 * o + 2) * OCT_STATES] = x0i


def _gelu_tanh(x):
    return 0.5 * x * (1.0 + jnp.tanh(0.7978845608028654 * (x + 0.044715 * (x * x * x))))


def _s5_pass_b_kernel(u0_ref, u1_ref, u2_ref, u3_ref, tab_ref, bb_ref, x0_ref, cre_ref, cim_ref, d_ref,
                      wglu_ref, bglu_ref, g_ref, o_ref, xr_ref, xi_ref, bur_ref, bui_ref, hr_ref, hi_ref,
                      y0_ref, y1_ref, y2_ref, y3_ref, *, t_steps):
    u_refs = (u0_ref, u1_ref, u2_ref, u3_ref)
    y_refs = (y0_ref, y1_ref, y2_ref, y3_ref)
    i = pl.program_id(0)

    @pl.when(i == 0)
    def _():
        for o in range(OCTETS):
            xr_ref[o] = x0_ref[:, 2 * o * OCT_STATES:(2 * o + 1) * OCT_STATES]
            xi_ref[o] = x0_ref[:, (2 * o + 1) * OCT_STATES:(2 * o + 2) * OCT_STATES]

    ys = []
    for o in range(OCTETS):
        u = u_refs[o][...]
        bur_ref[o] = _dot(u.astype(BF16), bb_ref[o, :, :OCT_STATES])
        bui_ref[o] = _dot(u.astype(BF16), bb_ref[o, :, OCT_STATES:])
        xr, xi = _s5_scan_octet(tab_ref, o, bur_ref, bui_ref, xr_ref[o], xi_ref[o], t_steps,
                                hist=(hr_ref, hi_ref))
        xr_ref[o] = xr
        xi_ref[o] = xi
        ys.append(_dot(hr_ref[o].astype(BF16), cre_ref[o]) - _dot(hi_ref[o].astype(BF16), cim_ref[o])
                  + d_ref[:, o * LANES:(o + 1) * LANES] * u)

    hact = _gelu_tanh(jnp.concatenate(ys, axis=1))
    gate = _sigmoid(_dot(hact.astype(BF16), wglu_ref[...]) + bglu_ref[...])
    res = _rms(hact * gate, g_ref[...])
    for o in range(OCTETS):
        y_refs[o][...] = res[:, o * LANES:(o + 1) * LANES]
        for r in range(STREAMS):
            o_ref[r, :, o * LANES:(o + 1) * LANES] = y_refs[o][pl.ds(r, t_steps, stride=STREAMS), :]


def _s5_block_diag_b(b):
    bt = b.reshape(OCTETS, 8, S5_STATE, S5_GROUP).transpose(0, 1, 3, 2)
    eye = jnp.eye(8, dtype=b.dtype)
    return jnp.einsum('ogpn,gh->ogphn', bt, eye).reshape(OCTETS, 8 * S5_GROUP, OCT_STATES)


def _s5_block_diag_c(c):
    ct = c.reshape(OCTETS, 8, S5_GROUP, S5_STATE).transpose(0, 1, 3, 2)
    eye = jnp.eye(8, dtype=c.dtype)
    return jnp.einsum('ognp,gh->ognhp', ct, eye).reshape(OCTETS, OCT_STATES, 8 * S5_GROUP)


def _s5(us, lam_re, lam_im, log_dt, b_re, b_im, c_re, c_im, d_skip, w_glu, b_glu, norm_g, t_steps):
    rows = us[0].shape[0]
    ls = rows // STREAMS
    tile = STREAMS * t_steps
    nsteps = ls // t_steps
    tab, bb = _s5_prep(lam_re, lam_im, log_dt, _s5_block_diag_b(b_re), _s5_block_diag_b(b_im), ls)
    cre, cim = _s5_block_diag_c(c_re).astype(BF16), _s5_block_diag_c(c_im).astype(BF16)
    nstate = S5_GROUPS * S5_STATE
    state_scratch = [pltpu.VMEM((OCTETS, STREAMS, OCT_STATES), F32)] * 2
    bu_scratch = [pltpu.VMEM((OCTETS, tile, OCT_STATES), F32)] * 2
    u_specs = [pl.BlockSpec((tile, LANES), lambda i: (i, 0))] * OCTETS

    x0 = pl.pallas_call(
        functools.partial(_s5_pass_a_kernel, t_steps=t_steps),
        grid=(nsteps,),
        in_specs=u_specs + [_const_spec(tab.shape), _const_spec(bb.shape)],
        out_specs=_const_spec((STREAMS, 2 * nstate)),
        out_shape=jax.ShapeDtypeStruct((STREAMS, 2 * nstate), F32),
        scratch_shapes=state_scratch + bu_scratch,
        compiler_params=_cparams("arbitrary"),
        name="s5_pass_a",
    )(*us, tab, bb)

    return pl.pallas_call(
        functools.partial(_s5_pass_b_kernel, t_steps=t_steps),
        grid=(nsteps,),
        in_specs=u_specs + [_const_spec(tab.shape), _const_spec(bb.shape),
                            _const_spec(x0.shape), _const_spec(cre.shape), _const_spec(cim.shape),
                            _const_spec((1, S5_WIDTH)), _const_spec(w_glu.shape), _const_spec((1, S5_WIDTH)),
                            _const_spec((1, S5_WIDTH))],
        out_specs=pl.BlockSpec((STREAMS, t_steps, S5_WIDTH), lambda i: (0, i, 0)),
        out_shape=jax.ShapeDtypeStruct((STREAMS, ls, S5_WIDTH), F32),
        scratch_shapes=state_scratch + bu_scratch + bu_scratch + [pltpu.VMEM((tile, LANES), F32)] * OCTETS,
        compiler_params=_cparams("arbitrary"),
        name="s5_pass_b",
    )(*us, tab, bb, x0, cre, cim, d_skip.reshape(1, -1), w_glu.astype(BF16), b_glu.reshape(1, -1),
      norm_g.reshape(1, -1))


def _mlp_block(hs, gpre_ref, w1_ref, w2_ref, gpost_ref):
    hn = _rms(hs, gpre_ref[...]).astype(BF16)
    acc = jnp.zeros(hs.shape, F32)
    for f in range(D_FF // FF_TILE):
        a = jnp.maximum(_dot(hn, w1_ref[:, f * FF_TILE:(f + 1) * FF_TILE]), 0.0)
        acc = acc + _dot((a * a).astype(BF16), w2_ref[f * FF_TILE:(f + 1) * FF_TILE, :])
    return hs + _rms(acc, gpost_ref[...])


def _resident(shape):
    nd = len(shape)
    return pl.BlockSpec(shape, lambda *_: (0,) * nd, pipeline_mode=pl.Buffered(1))


def _even_out_kernel(osb_ref, os5_ref, *refs, seq_tile):
    if seq_tile is None:
        hs_ref, *refs = refs
        hs = hs_ref[...]
    else:
        x_ref, meta_ref, *refs = refs
        hs = _seq_tile(x_ref[...], meta_ref, pl.program_id(0), *seq_tile)
    gsb_ref, wsb_ref, ws5_ref, gpost_ref, gpre_mlp_ref, w1_ref, w2_ref, gpost_mlp_ref, o_ref = refs
    sb = _rms(osb_ref[...], gsb_ref[...]).astype(BF16)
    mix = _dot(sb, wsb_ref[...]) + _dot(os5_ref[...].astype(BF16), ws5_ref[...])
    hs = hs + _rms(mix, gpost_ref[...])
    o_ref[...] = _mlp_block(hs, gpre_mlp_ref, w1_ref, w2_ref, gpost_mlp_ref)


def _row_tiles(lp, keep):
    if keep is None:
        tm = _largest_div(lp, 640, 16)
        return tm, lp // tm, None
    first, count = keep
    tm = _largest_div(count, 640, 16)
    return tm, count // tm, first


def _rows_spec(shape, tm, first, row_axis=0):
    nd = len(shape)
    if first is None:
        block = tuple(tm if a == row_axis else s for a, s in enumerate(shape))
        return pl.BlockSpec(block, lambda i: tuple(i if a == row_axis else 0 for a in range(nd)))
    assert first % 16 == 0 and tm % 16 == 0
    block = tuple(pl.Element(tm if a == row_axis else s) for a, s in enumerate(shape))
    return pl.BlockSpec(block, lambda i: tuple(pl.multiple_of(first + i * tm, 16) if a == row_axis else 0
                                               for a in range(nd)))


def _even_out(o_sb, o_s5, hs, g_sb, w_out, g_post, g_pre_mlp, w1, w2, g_post_mlp, keep=None):
    lp = o_sb.shape[0]
    w_sb, w_s5 = w_out[:SB_WIDTH], w_out[SB_WIDTH:]
    tm, steps, first = _row_tiles(lp, keep)
    if isinstance(hs, tuple):
        x, meta = hs
        seq_tile = (first or 0, tm, steps, x.shape[0])
        res, res_specs = [x, meta], [_x_window_spec(x.shape[0], first or 0, tm), _const_spec(meta.shape)]
    else:
        seq_tile = None
        res, res_specs = [hs], [_rows_spec(hs.shape, tm, first)]
    return pl.pallas_call(
        functools.partial(_even_out_kernel, seq_tile=seq_tile),
        grid=(steps,),
        in_specs=[_rows_spec(o_sb.shape, tm, first), _rows_spec(o_s5.shape, tm, first), *res_specs,
                  _const_spec((1, SB_WIDTH)), _resident(w_sb.shape), _resident(w_s5.shape),
                  _const_spec((1, D_MODEL)),
                  _const_spec((1, D_MODEL)), _resident(w1.shape), _resident(w2.shape), _const_spec((1, D_MODEL))],
        out_specs=pl.BlockSpec((tm, D_MODEL), lambda i: (i, 0)),
        out_shape=jax.ShapeDtypeStruct((steps * tm, D_MODEL), F32),
        compiler_params=_cparams("parallel"),
        name="even_out_mlp",
    )(o_sb, o_s5, *res, g_sb, w_sb, w_s5, g_post, g_pre_mlp, w1, w2, g_post_mlp)


def _chunk_tri(n):
    r = lax.broadcasted_iota(jnp.int32, (n, n), 0)
    c = lax.broadcasted_iota(jnp.int32, (n, n), 1)
    same = (r // DN_CHUNK) == (c // DN_CHUNK)
    return jnp.where(jnp.logical_and(same, r >= c), 1.0, 0.0).astype(BF16)


def _odd_in_kernel(hs_ref, g_ref, wqkv_ref, wz_ref, wabc_ref, conv_ref, alc_ref, dtc_ref,
                   q_ref, k_ref, v_ref, z_ref, gcol_ref, grow_ref, carry_ref, *, tm):
    @pl.when(pl.program_id(0) == 0)
    def _():
        carry_ref[...] = jnp.zeros_like(carry_ref)

    hn = _rms(hs_ref[...], g_ref[...]).astype(BF16)

    ab = _dot(hn, wabc_ref[...])
    lane = lax.broadcasted_iota(jnp.int32, (1, LANES), 1)
    gb = jnp.where(lane < DN_HEADS, -jnp.exp(alc_ref[...]) * _softplus(ab + dtc_ref[...]), _sigmoid(ab))
    tri = _chunk_tri(tm)
    hi, mid, lo = _split3(gb)
    gsum = _dot(tri, hi) + _dot(tri, mid) + _dot(tri, lo)
    gcum = jnp.where(lane < DN_HEADS, gsum, gb)
    gcol_ref[...] = gcum[:, :2 * DN_HEADS]
    grow_ref[...] = gcum.T[:2 * DN_HEADS]

    first = lax.broadcasted_iota(jnp.int32, (SUBLANES, 1), 0)
    w = DN_HEADS * DN_HEAD_DIM
    hd = DN_HEAD_DIM

    def conv_silu(c0):
        cols = slice(c0, c0 + 2 * hd)
        x = _dot(hn, wqkv_ref[:, cols])
        prev = carry_ref[:, cols]
        carry_ref[:, cols] = x[tm - SUBLANES:, :]
        y = x * conv_ref[DN_CONV - 1:DN_CONV, cols]
        for s in range(1, DN_CONV):
            sh = pltpu.roll(x, s, 0)
            head = jnp.where(first < s, pltpu.roll(prev, s, 0), sh[:SUBLANES])
            sh = jnp.concatenate([head, sh[SUBLANES:]], axis=0)
            y = y + sh * conv_ref[DN_CONV - 1 - s:DN_CONV - s, cols]
        return _silu(y)

    for h in range(0, DN_HEADS, 2):
        q2 = conv_silu(h * hd)
        k2 = conv_silu(w + h * hd)
        v2 = conv_silu(2 * w + h * hd)
        z2 = _dot(hn, wz_ref[:, h * hd:(h + 2) * hd])
        for d in range(2):
            sl = slice(d * hd, (d + 1) * hd)
            qh, kh = q2[:, sl], k2[:, sl]
            q_ref[h + d] = (qh * (lax.rsqrt(jnp.sum(qh * qh, axis=-1, keepdims=True) + EPS)
                                  * (hd ** -0.5))).astype(BF16)
            k_ref[h + d] = (kh * lax.rsqrt(jnp.sum(kh * kh, axis=-1, keepdims=True) + EPS)).astype(BF16)
            v_ref[h + d] = v2[:, sl].astype(BF16)
            z_ref[h + d] = z2[:, sl].astype(BF16)


def _odd_in(hs, g, w_in, conv_w, a_log, dt_bias, tm):
    lp = hs.shape[0]
    w = DN_HEADS * DN_HEAD_DIM
    wqkv = w_in[:, :3 * w].astype(BF16)
    wz = w_in[:, 3 * w:4 * w].astype(BF16)
    wab = w_in[:, 4 * w:]
    wabc = jnp.pad(wab, ((0, 0), (0, LANES - 2 * DN_HEADS))).astype(BF16)
    pad_row = lambda a: jnp.pad(a.reshape(1, -1), ((0, 0), (0, LANES - DN_HEADS)))
    row_spec = lambda width: pl.BlockSpec((tm, width), lambda i: (i, 0))
    act = jax.ShapeDtypeStruct((DN_HEADS, lp, DN_HEAD_DIM), BF16)
    act_spec = pl.BlockSpec((DN_HEADS, tm, DN_HEAD_DIM), lambda i: (0, i, 0))
    return pl.pallas_call(
        functools.partial(_odd_in_kernel, tm=tm),
        grid=(lp // tm,),
        in_specs=[row_spec(D_MODEL), _const_spec((1, D_MODEL)), _resident(wqkv.shape), _resident(wz.shape),
                  _const_spec(wabc.shape), _const_spec(conv_w.shape),
                  _const_spec((1, LANES)), _const_spec((1, LANES))],
        out_specs=[act_spec, act_spec, act_spec, act_spec, row_spec(2 * DN_HEADS),
                   pl.BlockSpec((2 * DN_HEADS, tm), lambda i: (0, i))],
        out_shape=[act, act, act, act,
                   jax.ShapeDtypeStruct((lp, 2 * DN_HEADS), F32),
                   jax.ShapeDtypeStruct((2 * DN_HEADS, lp), F32)],
        scratch_shapes=[pltpu.VMEM((SUBLANES, 3 * w), F32)],
        compiler_params=_cparams("arbitrary"),
        name="odd_in",
    )(hs, g, wqkv, wz, wabc, conv_w, pad_row(a_log), pad_row(dt_bias))


def _bdot(a, b):
    return lax.dot_general(a, b, (((2,), (1,)), ((0,), (0,))), preferred_element_type=F32)


def _bdot_nt(a, b):
    return lax.dot_general(a, b, (((2,), (2,)), ((0,), (0,))), preferred_element_type=F32)


def _bdot_tn(a, b):
    return lax.dot_general(a, b, (((1,), (1,)), ((0,), (0,))), preferred_element_type=F32)


def _dn_kernel(q_ref, k_ref, v_ref, z_ref, gcol_ref, grow_ref, ng_ref, o_ref,
               s_ref, u_scr, wq_scr, a_scr, kt_scr, egl_scr, *, chunks):
    c64, nh, hd, grp = DN_CHUNK, DN_HEADS, DN_HEAD_DIM, DN_GROUP
    nb = nh * grp

    @pl.when(pl.program_id(0) == 0)
    def _():
        s_ref[...] = jnp.zeros_like(s_ref)

    ri = lax.broadcasted_iota(jnp.int32, (c64, c64), 0)
    ci = lax.broadcasted_iota(jnp.int32, (c64, c64), 1)
    incl = ri >= ci
    strict = ri > ci
    eye = jnp.where(ri == ci, 1.0, 0.0)
    pair_masks = []
    s = 1
    while s < c64:
        pair_masks.append(jnp.logical_and((ri // s) % 2 == 1, (ri // s) - 1 == ci // s))
        s *= 2
    ng = ng_ref[...]

    def prep(gi, carry):
        c0 = gi * grp
        rows = pl.ds(pl.multiple_of(c0 * c64, grp * c64), grp * c64)
        cs = pl.ds(c0, grp)
        qb = q_ref[:, rows, :].reshape(nb, c64, hd)
        kb = k_ref[:, rows, :].reshape(nb, c64, hd)
        vb = v_ref[:, rows, :].reshape(nb, c64, hd)
        kf, vf = kb.astype(F32), vb.astype(F32)
        gcb = gcol_ref[rows, :]
        gc = jnp.stack([gcb[:, h:h + 1] for h in range(nh)], axis=0).reshape(nb, c64, 1)
        be = jnp.stack([gcb[:, nh + h:nh + h + 1] for h in range(nh)], axis=0).reshape(nb, c64, 1)
        gr = grow_ref[0:nh, cs].reshape(nb, 1, c64)
        gl = gc[:, c64 - 1:c64, :]
        eg = jnp.exp(gc)
        decay = jnp.exp(jnp.where(incl, gc - gr, -jnp.inf))
        qkk = _bdot_nt(jnp.concatenate([qb, kb], axis=1), kb)
        n = jnp.where(strict, be * qkk[:, c64:] * decay, 0.0)
        x = eye - jnp.where(pair_masks[0], n, 0.0)
        for m in pair_masks[1:]:
            xb = x.astype(BF16)
            x = x - _bdot(xb, _bdot(jnp.where(m, n, 0.0).astype(BF16), xb).astype(BF16))
        rhs = jnp.concatenate([(vf * be).astype(BF16), (kf * (be * eg)).astype(BF16)], axis=2)
        uw = _bdot(x.astype(BF16), rhs)
        qg = (qb.astype(F32) * eg).astype(BF16)
        u_scr[:, cs] = uw[:, :, :hd].reshape(nh, grp, c64, hd)
        wq_scr[:, cs] = jnp.concatenate([uw[:, :, hd:].astype(BF16), qg], axis=1).reshape(nh, grp, 2 * c64, hd)
        a_scr[:, cs] = (qkk[:, :c64] * decay).astype(BF16).reshape(nh, grp, c64, c64)
        kt_scr[:, cs] = (kf * jnp.exp(gl - gc)).astype(BF16).reshape(nh, grp, c64, hd)
        egl_scr[:, cs] = jnp.broadcast_to(jnp.exp(gl), (nb, 1, hd)).reshape(nh, grp, 1, hd)
        return carry

    lax.fori_loop(0, chunks // grp, prep, 0)

    def step(c, carry):
        rows = pl.ds(pl.multiple_of(c * c64, c64), c64)
        s = s_ref[...]
        ws = _bdot(wq_scr[:, c], s.astype(BF16))
        vnb = (u_scr[:, c] - ws[:, :c64]).astype(BF16)
        o = ws[:, c64:] + _bdot(a_scr[:, c], vnb)
        s_ref[...] = s * egl_scr[:, c] + _bdot_tn(kt_scr[:, c], vnb)
        on = o * lax.rsqrt(jnp.mean(o * o, axis=-1, keepdims=True) + EPS) * ng
        o_ref[:, rows, :] = (on * _silu(z_ref[:, rows, :].astype(F32))).astype(BF16)
        return carry

    lax.fori_loop(0, chunks, step, 0)


def _dn(q, k, v, z, gcol, grow, norm_g, chunks):
    nh, lp, hd = q.shape
    rows = chunks * DN_CHUNK
    grow4 = grow.reshape(2 * nh, lp // DN_CHUNK, 1, DN_CHUNK)
    act_spec = pl.BlockSpec((nh, rows, hd), lambda i: (0, i, 0))
    return pl.pallas_call(
        functools.partial(_dn_kernel, chunks=chunks),
        grid=(lp // rows,),
        in_specs=[act_spec, act_spec, act_spec, act_spec,
                  pl.BlockSpec((rows, 2 * nh), lambda i: (i, 0)),
                  pl.BlockSpec((2 * nh, chunks, 1, DN_CHUNK), lambda i: (0, i, 0, 0)),
                  _const_spec((1, hd))],
        out_specs=act_spec,
        out_shape=jax.ShapeDtypeStruct((nh, lp, hd), BF16),
        scratch_shapes=[pltpu.VMEM((nh, hd, hd), F32),
                        pltpu.VMEM((nh, chunks, DN_CHUNK, hd), F32),
                        pltpu.VMEM((nh, chunks, 2 * DN_CHUNK, hd), BF16),
                        pltpu.VMEM((nh, chunks, DN_CHUNK, DN_CHUNK), BF16),
                        pltpu.VMEM((nh, chunks, DN_CHUNK, hd), BF16),
                        pltpu.VMEM((nh, chunks, 1, hd), F32)],
        compiler_params=_cparams("arbitrary"),
        name="dn",
    )(q, k, v, z, gcol, grow4, norm_g.reshape(1, -1))


def _odd_out_kernel(og_ref, hs_ref, w_ref, gpost_ref, gpre_mlp_ref, w1_ref, w2_ref, gpost_mlp_ref, o_ref):
    og = jnp.concatenate([og_ref[h] for h in range(DN_HEADS)], axis=1)
    hs = hs_ref[...] + _rms(_dot(og, w_ref[...]), gpost_ref[...])
    o_ref[...] = _mlp_block(hs, gpre_mlp_ref, w1_ref, w2_ref, gpost_mlp_ref)


def _odd_out(og, hs, w_out, g_post, g_pre_mlp, w1, w2, g_post_mlp, keep=None):
    lp = hs.shape[0]
    tm, steps, first = _row_tiles(lp, keep)
    return pl.pallas_call(
        _odd_out_kernel,
        grid=(steps,),
        in_specs=[_rows_spec(og.shape, tm, first, row_axis=1), _rows_spec(hs.shape, tm, first),
                  _resident(w_out.shape), _const_spec((1, D_MODEL)),
                  _const_spec((1, D_MODEL)), _resident(w1.shape), _resident(w2.shape), _const_spec((1, D_MODEL))],
        out_specs=pl.BlockSpec((tm, D_MODEL), lambda i: (i, 0)),
        out_shape=jax.ShapeDtypeStruct((steps * tm, D_MODEL), F32),
        compiler_params=_cparams("parallel"),
        name="odd_out_mlp",
    )(og, hs, w_out, g_post, g_pre_mlp, w1, w2, g_post_mlp)


def kernel(x, meta_tokens, pre_mix_norm, post_mix_norm, pre_mlp_norm, post_mlp_norm, mlp_w1, mlp_w2, w_in_even, w_out_even, sb_out_norm, s5_lambda_re, s5_lambda_im, s5_log_dt, s5_b_re, s5_b_im, s5_c_re, s5_c_im, s5_d, s5_w_glu, s5_b_glu, s5_out_norm, w_in_odd, dn_conv_w, dn_a_log, dn_dt_bias, dn_out_norm, w_out_odd):
    assert x.shape[0] == 1 and x.shape[2] == D_MODEL
    depth = pre_mix_norm.shape[0]
    l = N_META + x.shape[1]
    lp = -(-l // SB_TILE) * SB_TILE
    ls = lp // STREAMS
    tr_even = _largest_div(ls, 104, 16)
    t_s5 = _largest_div(ls, 128, SUBLANES)
    tm_odd = _largest_div(lp, 640, max(LANES, DN_CHUNK))
    dn_chunks = _largest_div(lp // DN_CHUNK, 10, DN_GROUP)
    row = lambda a: a.reshape(1, -1)

    hs = (x[0], meta_tokens)
    for i in range(depth):
        j = i // 2
        mlp = (row(pre_mlp_norm[i]), mlp_w1[i].astype(BF16), mlp_w2[i].astype(BF16), row(post_mlp_norm[i]))
        keep = (N_META, l - N_META) if i == depth - 1 else None
        if i % 2 == 0:
            qkv, u_il = _even_in(hs, row(pre_mix_norm[i]), w_in_even[j].astype(BF16), tr_even, lp)
            o_sb = _sb_attention(qkv, SB_TILE)
            o_s5 = _s5(u_il, s5_lambda_re[j], s5_lambda_im[j], s5_log_dt[j],
                       s5_b_re[j], s5_b_im[j], s5_c_re[j], s5_c_im[j], s5_d[j], s5_w_glu[j], s5_b_glu[j],
                       s5_out_norm[j], t_s5)
            hs = _even_out(o_sb, o_s5.reshape(lp, S5_WIDTH), hs, row(sb_out_norm[j]), w_out_even[j].astype(BF16),
                           row(post_mix_norm[i]), *mlp, keep)
        else:
            q, k, v, z, gcol, grow = _odd_in(hs, row(pre_mix_norm[i]), w_in_odd[j], dn_conv_w[j], dn_a_log[j],
                                             dn_dt_bias[j], tm_odd)
            og = _dn(q, k, v, z, gcol, grow, dn_out_norm[j], dn_chunks)
            hs = _odd_out(og, hs, w_out_odd[j].astype(BF16), row(post_mix_norm[i]), *mlp, keep)
    return hs[None]
```

```python
import functools

import jax
import jax.numpy as jnp
from jax import lax
from jax.experimental import pallas as pl
from jax.experimental.pallas import tpu as pltpu

F32 = jnp.float32
BF16 = jnp.bfloat16

D_MODEL = 1024
N_META = 16
SB_HEAD_DIM = 64
SB_WIDTH = 512
S5_WIDTH = 512
S5_GROUP = 16
S5_GROUPS = 32
S5_STATE = 64
DN_HEAD_DIM = 128
DN_HEADS = 8
DN_CONV = 4
D_FF = 4096
FF_TILE = 1024
EPS = 1e-6

LANES = 128
SUBLANES = 8
STREAMS = SUBLANES
OCTETS = S5_GROUPS // 8
OCT_STATES = 8 * S5_STATE
DN_CHUNK = 128
DN_GROUP = 2
SB_TILE = 256
SB_KEY_TILE = 128
SB_PAIR_GROUP = 4
UNDERFLOW_LOG = -87.5
VMEM_LIMIT = 56 * 1024 * 1024


def _largest_div(n, cap, mult):
    best = None
    for d in range(mult, cap + 1, mult):
        if n % d == 0:
            best = d
    assert best is not None, (n, cap, mult)
    return best


def _cparams(*sem):
    return pltpu.CompilerParams(dimension_semantics=sem, vmem_limit_bytes=VMEM_LIMIT)


def _rms(x, g):
    ms = jnp.mean(x * x, axis=-1, keepdims=True)
    return x * lax.rsqrt(ms + EPS) * g


def _softplus(x):
    return jnp.maximum(x, 0.0) + jnp.log1p(jnp.exp(-jnp.abs(x)))


def _sigmoid(x):
    return 1.0 / (1.0 + jnp.exp(-x))


def _silu(x):
    return x * _sigmoid(x)


def _split3(x):
    hi = x.astype(BF16)
    r1 = x - hi.astype(F32)
    mid = r1.astype(BF16)
    lo = (r1 - mid.astype(F32)).astype(BF16)
    return hi, mid, lo


def _dot(a, b):
    return jnp.dot(a, b, preferred_element_type=F32)


def _dot_nt(a, b):
    return lax.dot_general(a, b, (((1,), (1,)), ((), ())), preferred_element_type=F32)


def _dot_tn(a, b):
    return lax.dot_general(a, b, (((0,), (0,)), ((), ())), preferred_element_type=F32)


def _const_spec(shape):
    nd = len(shape)
    return pl.BlockSpec(shape, lambda *_: (0,) * nd)


def _x_window_spec(seq, a0, tile):
    assert a0 % 16 == 0 and tile % 16 == 0 and seq % 16 == 0 and N_META % 16 == 0
    return pl.BlockSpec(
        (pl.Element(tile), pl.Element(D_MODEL)),
        lambda i: (pl.multiple_of(jnp.clip(a0 + i * tile - N_META, 0, seq - tile), 16), 0))


def _seq_tile(xb, meta_ref, i, a0, tile, steps, seq):
    l = N_META + seq
    out = xb
    if a0 < N_META:
        assert a0 == 0 and tile > N_META
        head = jnp.concatenate([meta_ref[...].astype(xb.dtype), xb[:tile - N_META]], axis=0)
        out = jnp.where(i == 0, head, out)
    n_full = min(steps, max(0, (l - a0) // tile))
    rem = min(tile, max(0, l - a0 - n_full * tile))
    pad_from = n_full
    if n_full < steps and rem:
        assert n_full > 0 or a0 >= N_META
        cut = jnp.concatenate([xb[tile - rem:], jnp.zeros((tile - rem, D_MODEL), xb.dtype)], axis=0)
        out = jnp.where(i == n_full, cut, out)
        pad_from += 1
    if pad_from < steps:
        out = jnp.where(i >= pad_from, jnp.zeros_like(out), out)
    return out


def _even_in_kernel(*refs, tr, ls, seq):
    if seq is None:
        hs_ref, g_ref, w_ref, qkv_ref, *u_refs = refs
        x = hs_ref[...].reshape(STREAMS * tr, D_MODEL)
    else:
        x_refs, (meta_ref, g_ref, w_ref, qkv_ref, *u_refs) = refs[:STREAMS], refs[STREAMS:]
        x = jnp.concatenate([_seq_tile(x_refs[r][...], meta_ref, pl.program_id(0), r * ls, tr, ls // tr, seq)
                             for r in range(STREAMS)], axis=0)
    hn = _rms(x, g_ref[...]).astype(BF16)
    p = _dot(hn, w_ref[...])
    q = p[:, :SB_WIDTH] * (SB_HEAD_DIM ** -0.5)
    qkv_ref[:, :, :SB_WIDTH] = q.reshape(STREAMS, tr, SB_WIDTH).astype(BF16)
    qkv_ref[:, :, SB_WIDTH:] = p[:, SB_WIDTH:3 * SB_WIDTH].reshape(STREAMS, tr, 2 * SB_WIDTH).astype(BF16)
    for o, u_ref in enumerate(u_refs):
        c0 = 3 * SB_WIDTH + o * LANES
        for r in range(STREAMS):
            u_ref[pl.ds(r, tr, stride=STREAMS), :] = p[r * tr:(r + 1) * tr, c0:c0 + LANES]


def _even_in(hs, g, w, tr, lp):
    ls = lp // STREAMS
    if isinstance(hs, tuple):
        x, meta = hs
        seq = x.shape[0]
        acts = [x] * STREAMS + [meta]
        act_specs = [_x_window_spec(seq, r * ls, tr) for r in range(STREAMS)] + [_const_spec(meta.shape)]
    else:
        seq = None
        acts = [hs.reshape(STREAMS, ls, D_MODEL)]
        act_specs = [pl.BlockSpec((STREAMS, tr, D_MODEL), lambda i: (0, i, 0))]
    qkv, *us = pl.pallas_call(
        functools.partial(_even_in_kernel, tr=tr, ls=ls, seq=seq),
        grid=(ls // tr,),
        in_specs=act_specs + [_const_spec((1, D_MODEL)), _const_spec(w.shape)],
        out_specs=[pl.BlockSpec((STREAMS, tr, 3 * SB_WIDTH), lambda i: (0, i, 0))]
                  + [pl.BlockSpec((STREAMS * tr, LANES), lambda i: (i, 0))] * OCTETS,
        out_shape=[jax.ShapeDtypeStruct((STREAMS, ls, 3 * SB_WIDTH), BF16)]
                  + [jax.ShapeDtypeStruct((lp, LANES), F32)] * OCTETS,
        compiler_params=_cparams("parallel"),
        name="even_in",
    )(*acts, g, w)
    return qkv.reshape(lp, 3 * SB_WIDTH), us


def _sb_kernel(q_ref, k_ref, v_ref, o_ref, acc_ref, car_ref, *, tq):
    tk = SB_KEY_TILE
    npair = SB_WIDTH // LANES
    pairs = range(npair)
    i = pl.program_id(0)
    lane = lax.broadcasted_iota(jnp.int32, (1, LANES), 1)
    qs = []
    for p in pairs:
        q2 = q_ref[:, p * LANES:(p + 1) * LANES]
        zq = jnp.zeros_like(q2)
        qs.append((jnp.where(lane < SB_HEAD_DIM, q2, zq), jnp.where(lane >= SB_HEAD_DIM, q2, zq)))
    jr = lax.broadcasted_iota(jnp.int32, (tk, tk), 0)
    jc = lax.broadcasted_iota(jnp.int32, (tk, tk), 1)
    later = jnp.where(jr > jc, 1.0, 0.0).astype(BF16)
    acc_ref[...] = jnp.zeros_like(acc_ref)
    car_ref[...] = jnp.zeros_like(car_ref)

    def key_block(j, r0, r1, diagonal):
        rows = slice(r0, r1)
        nr = r1 - r0
        off = pl.multiple_of(j * tk, tk)
        if diagonal:
            col = j * tk + lax.broadcasted_iota(jnp.int32, (1, tk), 1)
            row = i * tq + r0 + lax.broadcasted_iota(jnp.int32, (nr, 1), 0)
            valid = col < row
            valid = jnp.concatenate([valid, valid], axis=0)
            keep = lambda a: jnp.where(valid, a, 0.0)
        else:
            keep = lambda a: a
        for g0 in range(0, npair, SB_PAIR_GROUP):
            grp = range(g0, g0 + SB_PAIR_GROUP)
            z = {p: _dot_nt(jnp.concatenate([qs[p][0][rows], qs[p][1][rows]], axis=0),
                            k_ref[pl.ds(off, tk), p * LANES:(p + 1) * LANES]) for p in grp}
            lk = {p: keep(-(jnp.maximum(z[p], 0.0) + jnp.log(1.0 + jnp.exp(-jnp.abs(z[p]))))) for p in grp}
            hi = {p: lk[p].astype(BF16) for p in grp}
            lo = {p: (lk[p] - hi[p].astype(F32)).astype(BF16) for p in grp}
            rc = {p: _dot(jnp.concatenate([hi[p], lo[p]], axis=0), later) for p in grp}
            car = {p: car_ref[p, :, rows].reshape(2 * nr, 1) for p in grp}
            w = {p: keep(jnp.exp(z[p] + lk[p] + car[p] + rc[p][:2 * nr] + rc[p][2 * nr:])).astype(BF16) for p in grp}
            for p in grp:
                acc_ref[p, :, rows] += _dot(w[p], v_ref[pl.ds(off, tk), p * LANES:(p + 1) * LANES]).reshape(
                    2, nr, LANES)
                car_ref[p, :, rows] = (car[p] + jnp.sum(lk[p], axis=1, keepdims=True)).reshape(2, nr, 1)

    def band_live(r1):
        return (jnp.max(car_ref[:, :, r1 - tk:r1]) > UNDERFLOW_LOG).astype(jnp.int32)

    nd = tq // tk
    for d in range(nd - 1, -1, -1):
        key_block(i * nd + d, d * tk, tq, diagonal=True)

    j = i * nd - 1
    for r1 in range(tq, 0, -tk):
        def body(c, r1=r1):
            key_block(c[0], 0, r1, diagonal=False)
            return c[0] - 1, band_live(r1)

        j, _ = lax.while_loop(lambda c: jnp.logical_and(c[0] >= 0, c[1] > 0), body, (j, band_live(r1)))
    for p in pairs:
        o_ref[:, p * LANES:(p + 1) * LANES] = jnp.where(lane < SB_HEAD_DIM, acc_ref[p, 0], acc_ref[p, 1])


def _sb_attention(qkv, tq):
    lp = qkv.shape[0]
    npair = SB_WIDTH // LANES
    resident = lambda col: pl.BlockSpec((lp, SB_WIDTH), lambda i: (0, col), pipeline_mode=pl.Buffered(1))
    return pl.pallas_call(
        functools.partial(_sb_kernel, tq=tq),
        grid=(lp // tq,),
        in_specs=[pl.BlockSpec((tq, SB_WIDTH), lambda i: (i, 0)), resident(1), resident(2)],
        out_specs=pl.BlockSpec((tq, SB_WIDTH), lambda i: (i, 0)),
        out_shape=jax.ShapeDtypeStruct((lp, SB_WIDTH), F32),
        scratch_shapes=[pltpu.VMEM((npair, 2, tq, LANES), F32), pltpu.VMEM((npair, 2, tq, 1), F32)],
        compiler_params=_cparams("parallel"),
        name="sb_attn",
    )(qkv, qkv, qkv)


def _cmul(ar, ai, br, bi):
    return ar * br - ai * bi, ar * bi + ai * br


def _s5_prep_kernel(lre_ref, lim_ref, ldt_ref, bre_ref, bim_ref, tab_ref, bb_ref, *, ls):
    lr = jnp.minimum(lre_ref[...], -1e-4)
    li = lim_ref[...]
    dt = jnp.exp(ldt_ref[...])
    mag = jnp.exp(lr * dt)
    ang = li * dt
    ar, ai = mag * jnp.cos(ang), mag * jnp.sin(ang)
    den = lr * lr + li * li
    nr, ni = ar - 1.0, ai
    cr = (nr * lr + ni * li) / den
    ci = (ni * lr - nr * li) / den
    pr, pi = jnp.ones_like(ar), jnp.zeros_like(ar)
    br, bi = ar, ai
    n = ls
    while n:
        if n & 1:
            pr, pi = _cmul(pr, pi, br, bi)
        n >>= 1
        if n:
            br, bi = _cmul(br, bi, br, bi)
    tab_ref[...] = jnp.concatenate(
        [ar, ai, cr, ci, pr, pi, jnp.zeros_like(ar), jnp.zeros_like(ar)], axis=0)
    for o in range(OCTETS):
        sl = slice(o * OCT_STATES, (o + 1) * OCT_STATES)
        bre, bim = bre_ref[o], bim_ref[o]
        bb_ref[o, :, :OCT_STATES] = (cr[:, sl] * bre - ci[:, sl] * bim).astype(BF16)
        bb_ref[o, :, OCT_STATES:] = (cr[:, sl] * bim + ci[:, sl] * bre).astype(BF16)


def _s5_prep(lam_re, lam_im, log_dt, bre, bim, ls):
    n = S5_GROUPS * S5_STATE
    row = lambda a: a.reshape(1, n)
    ldt = jnp.broadcast_to(log_dt[:, None], (S5_GROUPS, S5_STATE))
    return pl.pallas_call(
        functools.partial(_s5_prep_kernel, ls=ls),
        out_shape=[jax.ShapeDtypeStruct((SUBLANES, n), F32),
                   jax.ShapeDtypeStruct((OCTETS, 8 * S5_GROUP, 2 * OCT_STATES), BF16)],
        name="s5_prep",
    )(row(lam_re), row(lam_im), row(ldt), bre, bim)


def _s5_scan_octet(tab_ref, o, bur_ref, bui_ref, xr, xi, t_steps, hist=None):
    sl = slice(o * OCT_STATES, (o + 1) * OCT_STATES)
    ar = jnp.broadcast_to(tab_ref[0:1, sl], (STREAMS, OCT_STATES))
    ai = jnp.broadcast_to(tab_ref[1:2, sl], (STREAMS, OCT_STATES))

    for t in range(t_steps):
        rows = slice(t * STREAMS, (t + 1) * STREAMS)
        xr, xi = ar * xr - ai * xi + bur_ref[o, rows, :], ar * xi + ai * xr + bui_ref[o, rows, :]
        if hist is not None:
            hist[0][o, rows, :] = xr
            hist[1][o, rows, :] = xi
    return xr, xi


def _s5_pass_a_kernel(u0_ref, u1_ref, u2_ref, u3_ref, tab_ref, bb_ref, x0_ref,
                      xr_ref, xi_ref, bur_ref, bui_ref, *, t_steps):
    u_refs = (u0_ref, u1_ref, u2_ref, u3_ref)
    i = pl.program_id(0)

    @pl.when(i == 0)
    def _():
        xr_ref[...] = jnp.zeros_like(xr_ref)
        xi_ref[...] = jnp.zeros_like(xi_ref)

    for o in range(OCTETS):
        ub = u_refs[o][...].astype(BF16)
        bur_ref[o] = _dot(ub, bb_ref[o, :, :OCT_STATES])
        bui_ref[o] = _dot(ub, bb_ref[o, :, OCT_STATES:])
        xr, xi = _s5_scan_octet(tab_ref, o, bur_ref, bui_ref, xr_ref[o], xi_ref[o], t_steps)
        xr_ref[o] = xr
        xi_ref[o] = xi

    @pl.when(i == pl.num_programs(0) - 1)
    def _():
        sub = lax.broadcasted_iota(jnp.int32, (STREAMS, OCT_STATES), 0)
        for o in range(OCTETS):
            sl = slice(o * OCT_STATES, (o + 1) * OCT_STATES)
            pr, pi = tab_ref[4:5, sl], tab_ref[5:6, sl]
            er, ei = xr_ref[o], xi_ref[o]
            x0r = jnp.zeros((STREAMS, OCT_STATES), F32)
            x0i = jnp.zeros((STREAMS, OCT_STATES), F32)
            cr = jnp.zeros((1, OCT_STATES), F32)
            ci = jnp.zeros((1, OCT_STATES), F32)
            for r in range(1, STREAMS):
                mr, mi = _cmul(pr, pi, cr, ci)
                cr, ci = er[r - 1:r] + mr, ei[r - 1:r] + mi
                x0r = jnp.where(sub == r, cr, x0r)
                x0i = jnp.where(sub == r, ci, x0i)
            x0_ref[:, 2 * o * OCT_STATES:(2 * o + 1) * OCT_STATES] = x0r
            x0_ref[:, (2 * o + 1) * OCT_STATES:(2 * o + 2) * OCT_STATES] = x0i


def _gelu_tanh(x):
    return 0.5 * x * (1.0 + jnp.tanh(0.7978845608028654 * (x + 0.044715 * (x * x * x))))


def _s5_pass_b_kernel(u0_ref, u1_ref, u2_ref, u3_ref, tab_ref, bb_ref, x0_ref, cre_ref, cim_ref, d_ref,
                      wglu_ref, bglu_ref, g_ref, o_ref, xr_ref, xi_ref, bur_ref, bui_ref, hr_ref, hi_ref,
                      y0_ref, y1_ref, y2_ref, y3_ref, *, t_steps):
    u_refs = (u0_ref, u1_ref, u2_ref, u3_ref)
    y_refs = (y0_ref, y1_ref, y2_ref, y3_ref)
    i = pl.program_id(0)

    @pl.when(i == 0)
    def _():
        for o in range(OCTETS):
            xr_ref[o] = x0_ref[:, 2 * o * OCT_STATES:(2 * o + 1) * OCT_STATES]
            xi_ref[o] = x0_ref[:, (2 * o + 1) * OCT_STATES:(2 * o + 2) * OCT_STATES]

    ys = []
    for o in range(OCTETS):
        u = u_refs[o][...]
        bur_ref[o] = _dot(u.astype(BF16), bb_ref[o, :, :OCT_STATES])
        bui_ref[o] = _dot(u.astype(BF16), bb_ref[o, :, OCT_STATES:])
        xr, xi = _s5_scan_octet(tab_ref, o, bur_ref, bui_ref, xr_ref[o], xi_ref[o], t_steps,
                                hist=(hr_ref, hi_ref))
        xr_ref[o] = xr
        xi_ref[o] = xi
        ys.append(_dot(hr_ref[o].astype(BF16), cre_ref[o]) - _dot(hi_ref[o].astype(BF16), cim_ref[o])
                  + d_ref[:, o * LANES:(o + 1) * LANES] * u)

    hact = _gelu_tanh(jnp.concatenate(ys, axis=1))
    gate = _sigmoid(_dot(hact.astype(BF16), wglu_ref[...]) + bglu_ref[...])
    res = _rms(hact * gate, g_ref[...])
    for o in range(OCTETS):
        y_refs[o][...] = res[:, o * LANES:(o + 1) * LANES]
        for r in range(STREAMS):
            o_ref[r, :, o * LANES:(o + 1) * LANES] = y_refs[o][pl.ds(r, t_steps, stride=STREAMS), :]


def _s5_block_diag_b(b):
    bt = b.reshape(OCTETS, 8, S5_STATE, S5_GROUP).transpose(0, 1, 3, 2)
    eye = jnp.eye(8, dtype=b.dtype)
    return jnp.einsum('ogpn,gh->ogphn', bt, eye).reshape(OCTETS, 8 * S5_GROUP, OCT_STATES)


def _s5_block_diag_c(c):
    ct = c.reshape(OCTETS, 8, S5_GROUP, S5_STATE).transpose(0, 1, 3, 2)
    eye = jnp.eye(8, dtype=c.dtype)
    return jnp.einsum('ognp,gh->ognhp', ct, eye).reshape(OCTETS, OCT_STATES, 8 * S5_GROUP)


def _s5(us, lam_re, lam_im, log_dt, b_re, b_im, c_re, c_im, d_skip, w_glu, b_glu, norm_g, t_steps):
    rows = us[0].shape[0]
    ls = rows // STREAMS
    tile = STREAMS * t_steps
    nsteps = ls // t_steps
    tab, bb = _s5_prep(lam_re, lam_im, log_dt, _s5_block_diag_b(b_re), _s5_block_diag_b(b_im), ls)
    cre, cim = _s5_block_diag_c(c_re).astype(BF16), _s5_block_diag_c(c_im).astype(BF16)
    nstate = S5_GROUPS * S5_STATE
    state_scratch = [pltpu.VMEM((OCTETS, STREAMS, OCT_STATES), F32)] * 2
    bu_scratch = [pltpu.VMEM((OCTETS, tile, OCT_STATES), F32)] * 2
    u_specs = [pl.BlockSpec((tile, LANES), lambda i: (i, 0))] * OCTETS

    x0 = pl.pallas_call(
        functools.partial(_s5_pass_a_kernel, t_steps=t_steps),
        grid=(nsteps,),
        in_specs=u_specs + [_const_spec(tab.shape), _const_spec(bb.shape)],
        out_specs=_const_spec((STREAMS, 2 * nstate)),
        out_shape=jax.ShapeDtypeStruct((STREAMS, 2 * nstate), F32),
        scratch_shapes=state_scratch + bu_scratch,
        compiler_params=_cparams("arbitrary"),
        name="s5_pass_a",
    )(*us, tab, bb)

    return pl.pallas_call(
        functools.partial(_s5_pass_b_kernel, t_steps=t_steps),
        grid=(nsteps,),
        in_specs=u_specs + [_const_spec(tab.shape), _const_spec(bb.shape),
                            _const_spec(x0.shape), _const_spec(cre.shape), _const_spec(cim.shape),
                            _const_spec((1, S5_WIDTH)), _const_spec(w_glu.shape), _const_spec((1, S5_WIDTH)),
                            _const_spec((1, S5_WIDTH))],
        out_specs=pl.BlockSpec((STREAMS, t_steps, S5_WIDTH), lambda i: (0, i, 0)),
        out_shape=jax.ShapeDtypeStruct((STREAMS, ls, S5_WIDTH), F32),
        scratch_shapes=state_scratch + bu_scratch + bu_scratch + [pltpu.VMEM((tile, LANES), F32)] * OCTETS,
        compiler_params=_cparams("arbitrary"),
        name="s5_pass_b",
    )(*us, tab, bb, x0, cre, cim, d_skip.reshape(1, -1), w_glu.astype(BF16), b_glu.reshape(1, -1),
      norm_g.reshape(1, -1))


def _mlp_block(hs, gpre_ref, w1_ref, w2_ref, gpost_ref):
    hn = _rms(hs, gpre_ref[...]).astype(BF16)
    acc = jnp.zeros(hs.shape, F32)
    for f in range(D_FF // FF_TILE):
        a = jnp.maximum(_dot(hn, w1_ref[:, f * FF_TILE:(f + 1) * FF_TILE]), 0.0)
        acc = acc + _dot((a * a).astype(BF16), w2_ref[f * FF_TILE:(f + 1) * FF_TILE, :])
    return hs + _rms(acc, gpost_ref[...])


def _resident(shape):
    nd = len(shape)
    return pl.BlockSpec(shape, lambda *_: (0,) * nd, pipeline_mode=pl.Buffered(1))


def _even_out_kernel(osb_ref, os5_ref, *refs, seq_tile):
    if seq_tile is None:
        hs_ref, *refs = refs
        hs = hs_ref[...]
    else:
        x_ref, meta_ref, *refs = refs
        hs = _seq_tile(x_ref[...], meta_ref, pl.program_id(0), *seq_tile)
    gsb_ref, wsb_ref, ws5_ref, gpost_ref, gpre_mlp_ref, w1_ref, w2_ref, gpost_mlp_ref, o_ref = refs
    sb = _rms(osb_ref[...], gsb_ref[...]).astype(BF16)
    mix = _dot(sb, wsb_ref[...]) + _dot(os5_ref[...].astype(BF16), ws5_ref[...])
    hs = hs + _rms(mix, gpost_ref[...])
    o_ref[...] = _mlp_block(hs, gpre_mlp_ref, w1_ref, w2_ref, gpost_mlp_ref)


def _row_tiles(lp, keep):
    if keep is None:
        tm = _largest_div(lp, 640, 16)
        return tm, lp // tm, None
    first, count = keep
    tm = _largest_div(count, 640, 16)
    return tm, count // tm, first


def _rows_spec(shape, tm, first, row_axis=0):
    nd = len(shape)
    if first is None:
        block = tuple(tm if a == row_axis else s for a, s in enumerate(shape))
        return pl.BlockSpec(block, lambda i: tuple(i if a == row_axis else 0 for a in range(nd)))
    assert first % 16 == 0 and tm % 16 == 0
    block = tuple(pl.Element(tm if a == row_axis else s) for a, s in enumerate(shape))
    return pl.BlockSpec(block, lambda i: tuple(pl.multiple_of(first + i * tm, 16) if a == row_axis else 0
                                               for a in range(nd)))


def _even_out(o_sb, o_s5, hs, g_sb, w_out, g_post, g_pre_mlp, w1, w2, g_post_mlp, keep=None):
    lp = o_sb.shape[0]
    w_sb, w_s5 = w_out[:SB_WIDTH], w_out[SB_WIDTH:]
    tm, steps, first = _row_tiles(lp, keep)
    if isinstance(hs, tuple):
        x, meta = hs
        seq_tile = (first or 0, tm, steps, x.shape[0])
        res, res_specs = [x, meta], [_x_window_spec(x.shape[0], first or 0, tm), _const_spec(meta.shape)]
    else:
        seq_tile = None
        res, res_specs = [hs], [_rows_spec(hs.shape, tm, first)]
    return pl.pallas_call(
        functools.partial(_even_out_kernel, seq_tile=seq_tile),
        grid=(steps,),
        in_specs=[_rows_spec(o_sb.shape, tm, first), _rows_spec(o_s5.shape, tm, first), *res_specs,
                  _const_spec((1, SB_WIDTH)), _resident(w_sb.shape), _resident(w_s5.shape),
                  _const_spec((1, D_MODEL)),
                  _const_spec((1, D_MODEL)), _resident(w1.shape), _resident(w2.shape), _const_spec((1, D_MODEL))],
        out_specs=pl.BlockSpec((tm, D_MODEL), lambda i: (i, 0)),
        out_shape=jax.ShapeDtypeStruct((steps * tm, D_MODEL), F32),
        compiler_params=_cparams("parallel"),
        name="even_out_mlp",
    )(o_sb, o_s5, *res, g_sb, w_sb, w_s5, g_post, g_pre_mlp, w1, w2, g_post_mlp)


def _odd_in_kernel(hs_ref, g_ref, wqkv_ref, wz_ref, wabc_ref, conv_ref, alc_ref, dtc_ref,
                   q_ref, k_ref, v_ref, z_ref, gcol_ref, grow_ref, carry_ref, *, tm):
    @pl.when(pl.program_id(0) == 0)
    def _():
        carry_ref[...] = jnp.zeros_like(carry_ref)

    hn = _rms(hs_ref[...], g_ref[...]).astype(BF16)

    ab = _dot(hn, wabc_ref[...])
    lane = lax.broadcasted_iota(jnp.int32, (1, LANES), 1)
    gb = jnp.where(lane < DN_HEADS, -jnp.exp(alc_ref[...]) * _softplus(ab + dtc_ref[...]), _sigmoid(ab))
    tr = lax.broadcasted_iota(jnp.int32, (DN_CHUNK, DN_CHUNK), 0)
    tc = lax.broadcasted_iota(jnp.int32, (DN_CHUNK, DN_CHUNK), 1)
    tri = jnp.where(tr >= tc, 1.0, 0.0).astype(BF16)
    hi, mid, lo = _split3(gb)
    gsum = jnp.concatenate(
        [_dot(tri, hi[c:c + DN_CHUNK]) + _dot(tri, mid[c:c + DN_CHUNK]) + _dot(tri, lo[c:c + DN_CHUNK])
         for c in range(0, tm, DN_CHUNK)], axis=0)
    gcum = jnp.where(lane < DN_HEADS, gsum, gb)
    gcol_ref[...] = gcum[:, :2 * DN_HEADS]
    grow_ref[...] = gcum.T[:2 * DN_HEADS]

    first = lax.broadcasted_iota(jnp.int32, (SUBLANES, 1), 0)
    w = DN_HEADS * DN_HEAD_DIM
    hd = DN_HEAD_DIM

    def conv_silu(c0):
        cols = slice(c0, c0 + 2 * hd)
        x = _dot(hn, wqkv_ref[:, cols])
        prev = carry_ref[:, cols]
        carry_ref[:, cols] = x[tm - SUBLANES:, :]
        y = x * conv_ref[DN_CONV - 1:DN_CONV, cols]
        for s in range(1, DN_CONV):
            sh = pltpu.roll(x, s, 0)
            head = jnp.where(first < s, pltpu.roll(prev, s, 0), sh[:SUBLANES])
            sh = jnp.concatenate([head, sh[SUBLANES:]], axis=0)
            y = y + sh * conv_ref[DN_CONV - 1 - s:DN_CONV - s, cols]
        return _silu(y)

    for h in range(0, DN_HEADS, 2):
        q2 = conv_silu(h * hd)
        k2 = conv_silu(w + h * hd)
        v2 = conv_silu(2 * w + h * hd)
        z2 = _dot(hn, wz_ref[:, h * hd:(h + 2) * hd])
        for d in range(2):
            sl = slice(d * hd, (d + 1) * hd)
            qh, kh = q2[:, sl], k2[:, sl]
            q_ref[h + d] = (qh * (lax.rsqrt(jnp.sum(qh * qh, axis=-1, keepdims=True) + EPS)
                                  * (hd ** -0.5))).astype(BF16)
            k_ref[h + d] = (kh * lax.rsqrt(jnp.sum(kh * kh, axis=-1, keepdims=True) + EPS)).astype(BF16)
            v_ref[h + d] = v2[:, sl].astype(BF16)
            z_ref[h + d] = z2[:, sl].astype(BF16)


def _odd_in(hs, g, w_in, conv_w, a_log, dt_bias, tm):
    lp = hs.shape[0]
    w = DN_HEADS * DN_HEAD_DIM
    wqkv = w_in[:, :3 * w].astype(BF16)
    wz = w_in[:, 3 * w:4 * w].astype(BF16)
    wab = w_in[:, 4 * w:]
    wabc = jnp.pad(wab, ((0, 0), (0, LANES - 2 * DN_HEADS))).astype(BF16)
    pad_row = lambda a: jnp.pad(a.reshape(1, -1), ((0, 0), (0, LANES - DN_HEADS)))
    row_spec = lambda width: pl.BlockSpec((tm, width), lambda i: (i, 0))
    act = jax.ShapeDtypeStruct((DN_HEADS, lp, DN_HEAD_DIM), BF16)
    act_spec = pl.BlockSpec((DN_HEADS, tm, DN_HEAD_DIM), lambda i: (0, i, 0))
    return pl.pallas_call(
        functools.partial(_odd_in_kernel, tm=tm),
        grid=(lp // tm,),
        in_specs=[row_spec(D_MODEL), _const_spec((1, D_MODEL)), _resident(wqkv.shape), _resident(wz.shape),
                  _const_spec(wabc.shape), _const_spec(conv_w.shape),
                  _const_spec((1, LANES)), _const_spec((1, LANES))],
        out_specs=[act_spec, act_spec, act_spec, act_spec, row_spec(2 * DN_HEADS),
                   pl.BlockSpec((2 * DN_HEADS, tm), lambda i: (0, i))],
        out_shape=[act, act, act, act,
                   jax.ShapeDtypeStruct((lp, 2 * DN_HEADS), F32),
                   jax.ShapeDtypeStruct((2 * DN_HEADS, lp), F32)],
        scratch_shapes=[pltpu.VMEM((SUBLANES, 3 * w), F32)],
        compiler_params=_cparams("arbitrary"),
        name="odd_in",
    )(hs, g, wqkv, wz, wabc, conv_w, pad_row(a_log), pad_row(dt_bias))


def _bdot(a, b):
    return lax.dot_general(a, b, (((2,), (1,)), ((0,), (0,))), preferred_element_type=F32)


def _bdot_nt(a, b):
    return lax.dot_general(a, b, (((2,), (2,)), ((0,), (0,))), preferred_element_type=F32)


def _bdot_tn(a, b):
    return lax.dot_general(a, b, (((1,), (1,)), ((0,), (0,))), preferred_element_type=F32)


def _dn_kernel(q_ref, k_ref, v_ref, z_ref, gcol_ref, grow_ref, ng_ref, o_ref,
               s_ref, u_scr, wq_scr, a_scr, kt_scr, egl_scr, *, chunks):
    c64, nh, hd, grp = DN_CHUNK, DN_HEADS, DN_HEAD_DIM, DN_GROUP
    nb = nh * grp

    @pl.when(pl.program_id(0) == 0)
    def _():
        s_ref[...] = jnp.zeros_like(s_ref)

    ri = lax.broadcasted_iota(jnp.int32, (c64, c64), 0)
    ci = lax.broadcasted_iota(jnp.int32, (c64, c64), 1)
    incl = ri >= ci
    strict = ri > ci
    eye = jnp.where(ri == ci, 1.0, 0.0)
    pair_masks = []
    s = 1
    while s < c64:
        pair_masks.append(jnp.logical_and((ri // s) % 2 == 1, (ri // s) - 1 == ci // s))
        s *= 2
    ng = ng_ref[...]

    def prep(gi, carry):
        c0 = gi * grp
        rows = pl.ds(pl.multiple_of(c0 * c64, grp * c64), grp * c64)
        cs = pl.ds(c0, grp)
        qb = q_ref[:, rows, :].reshape(nb, c64, hd)
        kb = k_ref[:, rows, :].reshape(nb, c64, hd)
        vb = v_ref[:, rows, :].reshape(nb, c64, hd)
        kf, vf = kb.astype(F32), vb.astype(F32)
        gcb = gcol_ref[rows, :]
        gc = jnp.stack([gcb[:, h:h + 1] for h in range(nh)], axis=0).reshape(nb, c64, 1)
        be = jnp.stack([gcb[:, nh + h:nh + h + 1] for h in range(nh)], axis=0).reshape(nb, c64, 1)
        gr = grow_ref[0:nh, cs].reshape(nb, 1, c64)
        gl = gc[:, c64 - 1:c64, :]
        eg = jnp.exp(gc)
        decay = jnp.exp(jnp.where(incl, gc - gr, -jnp.inf))
        qkk = _bdot_nt(jnp.concatenate([qb, kb], axis=1), kb)
        n = jnp.where(strict, be * qkk[:, c64:] * decay, 0.0)
        x = eye - jnp.where(pair_masks[0], n, 0.0)
        for m in pair_masks[1:]:
            xb = x.astype(BF16)
            x = x - _bdot(xb, _bdot(jnp.where(m, n, 0.0).astype(BF16), xb).astype(BF16))
        rhs = jnp.concatenate([(vf * be).astype(BF16), (kf * (be * eg)).astype(BF16)], axis=2)
        uw = _bdot(x.astype(BF16), rhs)
        qg = (qb.astype(F32) * eg).astype(BF16)
        u_scr[:, cs] = uw[:, :, :hd].reshape(nh, grp, c64, hd)
        wq_scr[:, cs] = jnp.concatenate([uw[:, :, hd:].astype(BF16), qg], axis=1).reshape(nh, grp, 2 * c64, hd)
        a_scr[:, cs] = (qkk[:, :c64] * decay).astype(BF16).reshape(nh, grp, c64, c64)
        kt_scr[:, cs] = (kf * jnp.exp(gl - gc)).astype(BF16).reshape(nh, grp, c64, hd)
        egl_scr[:, cs] = jnp.broadcast_to(jnp.exp(gl), (nb, 1, hd)).reshape(nh, grp, 1, hd)
        return carry

    lax.fori_loop(0, chunks // grp, prep, 0)

    def step(c, carry):
        rows = pl.ds(pl.multiple_of(c * c64, c64), c64)
        s = s_ref[...]
        ws = _bdot(wq_scr[:, c], s.astype(BF16))
        vnb = (u_scr[:, c] - ws[:, :c64]).astype(BF16)
        o = ws[:, c64:] + _bdot(a_scr[:, c], vnb)
        s_ref[...] = s * egl_scr[:, c] + _bdot_tn(kt_scr[:, c], vnb)
        on = o * lax.rsqrt(jnp.mean(o * o, axis=-1, keepdims=True) + EPS) * ng
        o_ref[:, rows, :] = (on * _silu(z_ref[:, rows, :].astype(F32))).astype(BF16)
        return carry

    lax.fori_loop(0, chunks, step, 0)


def _dn(q, k, v, z, gcol, grow, norm_g, chunks):
    nh, lp, hd = q.shape
    rows = chunks * DN_CHUNK
    grow4 = grow.reshape(2 * nh, lp // DN_CHUNK, 1, DN_CHUNK)
    act_spec = pl.BlockSpec((nh, rows, hd), lambda i: (0, i, 0))
    return pl.pallas_call(
        functools.partial(_dn_kernel, chunks=chunks),
        grid=(lp // rows,),
        in_specs=[act_spec, act_spec, act_spec, act_spec,
                  pl.BlockSpec((rows, 2 * nh), lambda i: (i, 0)),
                  pl.BlockSpec((2 * nh, chunks, 1, DN_CHUNK), lambda i: (0, i, 0, 0)),
                  _const_spec((1, hd))],
        out_specs=act_spec,
        out_shape=jax.ShapeDtypeStruct((nh, lp, hd), BF16),
        scratch_shapes=[pltpu.VMEM((nh, hd, hd), F32),
                        pltpu.VMEM((nh, chunks, DN_CHUNK, hd), F32),
                        pltpu.VMEM((nh, chunks, 2 * DN_CHUNK, hd), BF16),
                        pltpu.VMEM((nh, chunks, DN_CHUNK, DN_CHUNK), BF16),
                        pltpu.VMEM((nh, chunks, DN_CHUNK, hd), BF16),
                        pltpu.VMEM((nh, chunks, 1, hd), F32)],
        compiler_params=_cparams("arbitrary"),
        name="dn",
    )(q, k, v, z, gcol, grow4, norm_g.reshape(1, -1))


def _odd_out_kernel(og_ref, hs_ref, w_ref, gpost_ref, gpre_mlp_ref, w1_ref, w2_ref, gpost_mlp_ref, o_ref):
    og = jnp.concatenate([og_ref[h] for h in range(DN_HEADS)], axis=1)
    hs = hs_ref[...] + _rms(_dot(og, w_ref[...]), gpost_ref[...])
    o_ref[...] = _mlp_block(hs, gpre_mlp_ref, w1_ref, w2_ref, gpost_mlp_ref)


def _odd_out(og, hs, w_out, g_post, g_pre_mlp, w1, w2, g_post_mlp, keep=None):
    lp = hs.shape[0]
    tm, steps, first = _row_tiles(lp, keep)
    return pl.pallas_call(
        _odd_out_kernel,
        grid=(steps,),
        in_specs=[_rows_spec(og.shape, tm, first, row_axis=1), _rows_spec(hs.shape, tm, first),
                  _resident(w_out.shape), _const_spec((1, D_MODEL)),
                  _const_spec((1, D_MODEL)), _resident(w1.shape), _resident(w2.shape), _const_spec((1, D_MODEL))],
        out_specs=pl.BlockSpec((tm, D_MODEL), lambda i: (i, 0)),
        out_shape=jax.ShapeDtypeStruct((steps * tm, D_MODEL), F32),
        compiler_params=_cparams("parallel"),
        name="odd_out_mlp",
    )(og, hs, w_out, g_post, g_pre_mlp, w1, w2, g_post_mlp)


def kernel(x, meta_tokens, pre_mix_norm, post_mix_norm, pre_mlp_norm, post_mlp_norm, mlp_w1, mlp_w2, w_in_even, w_out_even, sb_out_norm, s5_lambda_re, s5_lambda_im, s5_log_dt, s5_b_re, s5_b_im, s5_c_re, s5_c_im, s5_d, s5_w_glu, s5_b_glu, s5_out_norm, w_in_odd, dn_conv_w, dn_a_log, dn_dt_bias, dn_out_norm, w_out_odd):
    assert x.shape[0] == 1 and x.shape[2] == D_MODEL
    depth = pre_mix_norm.shape[0]
    l = N_META + x.shape[1]
    lp = -(-l // SB_TILE) * SB_TILE
    ls = lp // STREAMS
    tr_even = _largest_div(ls, 104, 16)
    t_s5 = _largest_div(ls, 128, SUBLANES)
    tm_odd = _largest_div(lp, 640, max(LANES, DN_CHUNK))
    dn_chunks = _largest_div(lp // DN_CHUNK, 10, DN_GROUP)
    row = lambda a: a.reshape(1, -1)

    hs = (x[0], meta_tokens)
    for i in range(depth):
        j = i // 2
        mlp = (row(pre_mlp_norm[i]), mlp_w1[i].astype(BF16), mlp_w2[i].astype(BF16), row(post_mlp_norm[i]))
        keep = (N_META, l - N_META) if i == depth - 1 else None
        if i % 2 == 0:
            qkv, u_il = _even_in(hs, row(pre_mix_norm[i]), w_in_even[j].astype(BF16), tr_even, lp)
            o_sb = _sb_attention(qkv, SB_TILE)
            o_s5 = _s5(u_il, s5_lambda_re[j], s5_lambda_im[j], s5_log_dt[j],
                       s5_b_re[j], s5_b_im[j], s5_c_re[j], s5_c_im[j], s5_d[j], s5_w_glu[j], s5_b_glu[j],
                       s5_out_norm[j], t_s5)
            hs = _even_out(o_sb, o_s5.reshape(lp, S5_WIDTH), hs, row(sb_out_norm[j]), w_out_even[j].astype(BF16),
                           row(post_mix_norm[i]), *mlp, keep)
        else:
            q, k, v, z, gcol, grow = _odd_in(hs, row(pre_mix_norm[i]), w_in_odd[j], dn_conv_w[j], dn_a_log[j],
                                             dn_dt_bias[j], tm_odd)
            og = _dn(q, k, v, z, gcol, grow, dn_out_norm[j], dn_chunks)
            hs = _odd_out(og, hs, w_out_odd[j].astype(BF16), row(post_mix_norm[i]), *mlp, keep)
    return hs[None]
```

```python
import functools

import jax
import jax.numpy as jnp
from jax import lax
from jax.experimental import pallas as pl
from jax.experimental.pallas import tpu as pltpu

F32 = jnp.float32
BF16 = jnp.bfloat16

D_MODEL = 1024
N_META = 16
SB_HEAD_DIM = 64
SB_WIDTH = 512
S5_WIDTH = 512
S5_GROUP = 16
S5_GROUPS = 32
S5_STATE = 64
DN_HEAD_DIM = 128
DN_HEADS = 8
DN_CONV = 4
D_FF = 4096
FF_TILE = 1024
EPS = 1e-6

LANES = 128
SUBLANES = 8
STREAMS = SUBLANES
OCTETS = S5_GROUPS // 8
OCT_STATES = 8 * S5_STATE
DN_CHUNK = 128
DN_GROUP = 2
SB_TILE = 256
SB_KEY_TILE = 128
SB_PAIR_GROUP = 4
UNDERFLOW_LOG = -87.5
VMEM_LIMIT = 56 * 1024 * 1024


def _largest_div(n, cap, mult):
    best = None
    for d in range(mult, cap + 1, mult):
        if n % d == 0:
            best = d
    assert best is not None, (n, cap, mult)
    return best


def _cparams(*sem):
    return pltpu.CompilerParams(dimension_semantics=sem, vmem_limit_bytes=VMEM_LIMIT)


def _rms(x, g):
    ms = jnp.mean(x * x, axis=-1, keepdims=True)
    return x * lax.rsqrt(ms + EPS) * g


def _softplus(x):
    return jnp.maximum(x, 0.0) + jnp.log1p(jnp.exp(-jnp.abs(x)))


def _sigmoid(x):
    return 1.0 / (1.0 + jnp.exp(-x))


def _silu(x):
    return x * _sigmoid(x)


def _split3(x):
    hi = x.astype(BF16)
    r1 = x - hi.astype(F32)
    mid = r1.astype(BF16)
    lo = (r1 - mid.astype(F32)).astype(BF16)
    return hi, mid, lo


def _dot(a, b):
    return jnp.dot(a, b, preferred_element_type=F32)


def _dot_nt(a, b):
    return lax.dot_general(a, b, (((1,), (1,)), ((), ())), preferred_element_type=F32)


def _dot_tn(a, b):
    return lax.dot_general(a, b, (((0,), (0,)), ((), ())), preferred_element_type=F32)


def _const_spec(shape):
    nd = len(shape)
    return pl.BlockSpec(shape, lambda *_: (0,) * nd)


def _x_window_spec(seq, a0, tile):
    assert a0 % 16 == 0 and tile % 16 == 0 and seq % 16 == 0 and N_META % 16 == 0
    return pl.BlockSpec(
        (pl.Element(tile), pl.Element(D_MODEL)),
        lambda i: (pl.multiple_of(jnp.clip(a0 + i * tile - N_META, 0, seq - tile), 16), 0))


def _seq_tile(xb, meta_ref, i, a0, tile, steps, seq):
    l = N_META + seq
    out = xb
    if a0 < N_META:
        assert a0 == 0 and tile > N_META
        head = jnp.concatenate([meta_ref[...].astype(xb.dtype), xb[:tile - N_META]], axis=0)
        out = jnp.where(i == 0, head, out)
    n_full = min(steps, max(0, (l - a0) // tile))
    rem = min(tile, max(0, l - a0 - n_full * tile))
    pad_from = n_full
    if n_full < steps and rem:
        assert n_full > 0 or a0 >= N_META
        cut = jnp.concatenate([xb[tile - rem:], jnp.zeros((tile - rem, D_MODEL), xb.dtype)], axis=0)
        out = jnp.where(i == n_full, cut, out)
        pad_from += 1
    if pad_from < steps:
        out = jnp.where(i >= pad_from, jnp.zeros_like(out), out)
    return out


def _even_in_kernel(*refs, tr, ls, seq):
    if seq is None:
        hs_ref, g_ref, w_ref, qkv_ref, *u_refs = refs
        x = hs_ref[...].reshape(STREAMS * tr, D_MODEL)
    else:
        x_refs, (meta_ref, g_ref, w_ref, qkv_ref, *u_refs) = refs[:STREAMS], refs[STREAMS:]
        x = jnp.concatenate([_seq_tile(x_refs[r][...], meta_ref, pl.program_id(0), r * ls, tr, ls // tr, seq)
                             for r in range(STREAMS)], axis=0)
    hn = _rms(x, g_ref[...]).astype(BF16)
    p = _dot(hn, w_ref[...])
    q = p[:, :SB_WIDTH] * (SB_HEAD_DIM ** -0.5)
    qkv_ref[:, :, :SB_WIDTH] = q.reshape(STREAMS, tr, SB_WIDTH).astype(BF16)
    qkv_ref[:, :, SB_WIDTH:] = p[:, SB_WIDTH:3 * SB_WIDTH].reshape(STREAMS, tr, 2 * SB_WIDTH).astype(BF16)
    for o, u_ref in enumerate(u_refs):
        c0 = 3 * SB_WIDTH + o * LANES
        for r in range(STREAMS):
            u_ref[pl.ds(r, tr, stride=STREAMS), :] = p[r * tr:(r + 1) * tr, c0:c0 + LANES]


def _even_in(hs, g, w, tr, lp):
    ls = lp // STREAMS
    if isinstance(hs, tuple):
        x, meta = hs
        seq = x.shape[0]
        acts = [x] * STREAMS + [meta]
        act_specs = [_x_window_spec(seq, r * ls, tr) for r in range(STREAMS)] + [_const_spec(meta.shape)]
    else:
        seq = None
        acts = [hs.reshape(STREAMS, ls, D_MODEL)]
        act_specs = [pl.BlockSpec((STREAMS, tr, D_MODEL), lambda i: (0, i, 0))]
    qkv, *us = pl.pallas_call(
        functools.partial(_even_in_kernel, tr=tr, ls=ls, seq=seq),
        grid=(ls // tr,),
        in_specs=act_specs + [_const_spec((1, D_MODEL)), _const_spec(w.shape)],
        out_specs=[pl.BlockSpec((STREAMS, tr, 3 * SB_WIDTH), lambda i: (0, i, 0))]
                  + [pl.BlockSpec((STREAMS * tr, LANES), lambda i: (i, 0))] * OCTETS,
        out_shape=[jax.ShapeDtypeStruct((STREAMS, ls, 3 * SB_WIDTH), BF16)]
                  + [jax.ShapeDtypeStruct((lp, LANES), F32)] * OCTETS,
        compiler_params=_cparams("parallel"),
        name="even_in",
    )(*acts, g, w)
    return qkv.reshape(lp, 3 * SB_WIDTH), us


def _sb_kernel(q_ref, k_ref, v_ref, o_ref, acc_ref, car_ref, *, tq):
    tk = SB_KEY_TILE
    npair = SB_WIDTH // LANES
    pairs = range(npair)
    i = pl.program_id(0)
    lane = lax.broadcasted_iota(jnp.int32, (1, LANES), 1)
    qs = []
    for p in pairs:
        q2 = q_ref[:, p * LANES:(p + 1) * LANES]
        zq = jnp.zeros_like(q2)
        qs.append((jnp.where(lane < SB_HEAD_DIM, q2, zq), jnp.where(lane >= SB_HEAD_DIM, q2, zq)))
    jr = lax.broadcasted_iota(jnp.int32, (tk, tk), 0)
    jc = lax.broadcasted_iota(jnp.int32, (tk, tk), 1)
    later = jnp.where(jr > jc, 1.0, 0.0).astype(BF16)
    acc_ref[...] = jnp.zeros_like(acc_ref)
    car_ref[...] = jnp.zeros_like(car_ref)

    def key_block(j, r0, r1, diagonal):
        rows = slice(r0, r1)
        nr = r1 - r0
        off = pl.multiple_of(j * tk, tk)
        if diagonal:
            col = j * tk + lax.broadcasted_iota(jnp.int32, (1, tk), 1)
            row = i * tq + r0 + lax.broadcasted_iota(jnp.int32, (nr, 1), 0)
            valid = col < row
            valid = jnp.concatenate([valid, valid], axis=0)
            keep = lambda a: jnp.where(valid, a, 0.0)
        else:
            keep = lambda a: a
        for g0 in range(0, npair, SB_PAIR_GROUP):
            grp = range(g0, g0 + SB_PAIR_GROUP)
            z = {p: _dot_nt(jnp.concatenate([qs[p][0][rows], qs[p][1][rows]], axis=0),
                            k_ref[pl.ds(off, tk), p * LANES:(p + 1) * LANES]) for p in grp}
            lk = {p: keep(-(jnp.maximum(z[p], 0.0) + jnp.log(1.0 + jnp.exp(-jnp.abs(z[p]))))) for p in grp}
            hi = {p: lk[p].astype(BF16) for p in grp}
            lo = {p: (lk[p] - hi[p].astype(F32)).astype(BF16) for p in grp}
            rc = {p: _dot(jnp.concatenate([hi[p], lo[p]], axis=0), later) for p in grp}
            car = {p: car_ref[p, :, rows].reshape(2 * nr, 1) for p in grp}
            w = {p: keep(jnp.exp(z[p] + lk[p] + car[p] + rc[p][:2 * nr] + rc[p][2 * nr:])).astype(BF16) for p in grp}
            for p in grp:
                acc_ref[p, :, rows] += _dot(w[p], v_ref[pl.ds(off, tk), p * LANES:(p + 1) * LANES]).reshape(
                    2, nr, LANES)
                car_ref[p, :, rows] = (car[p] + jnp.sum(lk[p], axis=1, keepdims=True)).reshape(2, nr, 1)

    def band_live(r1):
        return (jnp.max(car_ref[:, :, r1 - tk:r1]) > UNDERFLOW_LOG).astype(jnp.int32)

    nd = tq // tk
    for d in range(nd - 1, -1, -1):
        key_block(i * nd + d, d * tk, tq, diagonal=True)

    j = i * nd - 1
    for r1 in range(tq, 0, -tk):
        def body(c, r1=r1):
            key_block(c[0], 0, r1, diagonal=False)
            return c[0] - 1, band_live(r1)

        j, _ = lax.while_loop(lambda c: jnp.logical_and(c[0] >= 0, c[1] > 0), body, (j, band_live(r1)))
    for p in pairs:
        o_ref[:, p * LANES:(p + 1) * LANES] = jnp.where(lane < SB_HEAD_DIM, acc_ref[p, 0], acc_ref[p, 1])


def _sb_attention(qkv, tq):
    lp = qkv.shape[0]
    npair = SB_WIDTH // LANES
    resident = lambda col: pl.BlockSpec((lp, SB_WIDTH), lambda i: (0, col), pipeline_mode=pl.Buffered(1))
    return pl.pallas_call(
        functools.partial(_sb_kernel, tq=tq),
        grid=(lp // tq,),
        in_specs=[pl.BlockSpec((tq, SB_WIDTH), lambda i: (i, 0)), resident(1), resident(2)],
        out_specs=pl.BlockSpec((tq, SB_WIDTH), lambda i: (i, 0)),
        out_shape=jax.ShapeDtypeStruct((lp, SB_WIDTH), F32),
        scratch_shapes=[pltpu.VMEM((npair, 2, tq, LANES), F32), pltpu.VMEM((npair, 2, tq, 1), F32)],
        compiler_params=_cparams("parallel"),
        name="sb_attn",
    )(qkv, qkv, qkv)


def _cmul(ar, ai, br, bi):
    return ar * br - ai * bi, ar * bi + ai * br


def _s5_prep_kernel(lre_ref, lim_ref, ldt_ref, bre_ref, bim_ref, tab_ref, bb_ref, *, ls):
    lr = jnp.minimum(lre_ref[...], -1e-4)
    li = lim_ref[...]
    dt = jnp.exp(ldt_ref[...])
    mag = jnp.exp(lr * dt)
    ang = li * dt
    ar, ai = mag * jnp.cos(ang), mag * jnp.sin(ang)
    den = lr * lr + li * li
    nr, ni = ar - 1.0, ai
    cr = (nr * lr + ni * li) / den
    ci = (ni * lr - nr * li) / den
    pr, pi = jnp.ones_like(ar), jnp.zeros_like(ar)
    br, bi = ar, ai
    n = ls
    while n:
        if n & 1:
            pr, pi = _cmul(pr, pi, br, bi)
        n >>= 1
        if n:
            br, bi = _cmul(br, bi, br, bi)
    tab_ref[...] = jnp.concatenate(
        [ar, ai, cr, ci, pr, pi, jnp.zeros_like(ar), jnp.zeros_like(ar)], axis=0)
    for o in range(OCTETS):
        sl = slice(o * OCT_STATES, (o + 1) * OCT_STATES)
        bre, bim = bre_ref[o], bim_ref[o]
        bb_ref[o, :, :OCT_STATES] = (cr[:, sl] * bre - ci[:, sl] * bim).astype(BF16)
        bb_ref[o, :, OCT_STATES:] = (cr[:, sl] * bim + ci[:, sl] * bre).astype(BF16)


def _s5_prep(lam_re, lam_im, log_dt, bre, bim, ls):
    n = S5_GROUPS * S5_STATE
    row = lambda a: a.reshape(1, n)
    ldt = jnp.broadcast_to(log_dt[:, None], (S5_GROUPS, S5_STATE))
    return pl.pallas_call(
        functools.partial(_s5_prep_kernel, ls=ls),
        out_shape=[jax.ShapeDtypeStruct((SUBLANES, n), F32),
                   jax.ShapeDtypeStruct((OCTETS, 8 * S5_GROUP, 2 * OCT_STATES), BF16)],
        name="s5_prep",
    )(row(lam_re), row(lam_im), row(ldt), bre, bim)


def _s5_scan_octet(tab_ref, o, bur_ref, bui_ref, xr, xi, t_steps, hist=None):
    sl = slice(o * OCT_STATES, (o + 1) * OCT_STATES)
    ar = jnp.broadcast_to(tab_ref[0:1, sl], (STREAMS, OCT_STATES))
    ai = jnp.broadcast_to(tab_ref[1:2, sl], (STREAMS, OCT_STATES))

    for t in range(t_steps):
        rows = slice(t * STREAMS, (t + 1) * STREAMS)
        xr, xi = ar * xr - ai * xi + bur_ref[o, rows, :], ar * xi + ai * xr + bui_ref[o, rows, :]
        if hist is not None:
            hist[0][o, rows, :] = xr
            hist[1][o, rows, :] = xi
    return xr, xi


def _s5_pass_a_kernel(u0_ref, u1_ref, u2_ref, u3_ref, tab_ref, bb_ref, x0_ref,
                      xr_ref, xi_ref, bur_ref, bui_ref, *, t_steps):
    u_refs = (u0_ref, u1_ref, u2_ref, u3_ref)
    i = pl.program_id(0)

    @pl.when(i == 0)
    def _():
        xr_ref[...] = jnp.zeros_like(xr_ref)
        xi_ref[...] = jnp.zeros_like(xi_ref)

    for o in range(OCTETS):
        ub = u_refs[o][...].astype(BF16)
        bur_ref[o] = _dot(ub, bb_ref[o, :, :OCT_STATES])
        bui_ref[o] = _dot(ub, bb_ref[o, :, OCT_STATES:])
        xr, xi = _s5_scan_octet(tab_ref, o, bur_ref, bui_ref, xr_ref[o], xi_ref[o], t_steps)
        xr_ref[o] = xr
        xi_ref[o] = xi

    @pl.when(i == pl.num_programs(0) - 1)
    def _():
        sub = lax.broadcasted_iota(jnp.int32, (STREAMS, OCT_STATES), 0)
        for o in range(OCTETS):
            sl = slice(o * OCT_STATES, (o + 1) * OCT_STATES)
            pr, pi = tab_ref[4:5, sl], tab_ref[5:6, sl]
            er, ei = xr_ref[o], xi_ref[o]
            x0r = jnp.zeros((STREAMS, OCT_STATES), F32)
            x0i = jnp.zeros((STREAMS, OCT_STATES), F32)
            cr = jnp.zeros((1, OCT_STATES), F32)
            ci = jnp.zeros((1, OCT_STATES), F32)
            for r in range(1, STREAMS):
                mr, mi = _cmul(pr, pi, cr, ci)
                cr, ci = er[r - 1:r] + mr, ei[r - 1:r] + mi
                x0r = jnp.where(sub == r, cr, x0r)
                x0i = jnp.where(sub == r, ci, x0i)
            x0_ref[:, 2 * o * OCT_STATES:(2 * o + 1) * OCT_STATES] = x0r
            x0_ref[:, (2 * o + 1) * OCT_STATES:(2 * o + 2) * OCT_STATES] = x0i


def _gelu_tanh(x):
    return 0.5 * x * (1.0 + jnp.tanh(0.7978845608028654 * (x + 0.044715 * (x * x * x))))


def _s5_pass_b_kernel(u0_ref, u1_ref, u2_ref, u3_ref, tab_ref, bb_ref, x0_ref, cre_ref, cim_ref, d_ref,
                      wglu_ref, bglu_ref, g_ref, o_ref, xr_ref, xi_ref, bur_ref, bui_ref, hr_ref, hi_ref,
                      y0_ref, y1_ref, y2_ref, y3_ref, *, t_steps):
    u_refs = (u0_ref, u1_ref, u2_ref, u3_ref)
    y_refs = (y0_ref, y1_ref, y2_ref, y3_ref)
    i = pl.program_id(0)

    @pl.when(i == 0)
    def _():
        for o in range(OCTETS):
            xr_ref[o] = x0_ref[:, 2 * o * OCT_STATES:(2 * o + 1) * OCT_STATES]
            xi_ref[o] = x0_ref[:, (2 * o + 1) * OCT_STATES:(2 * o + 2) * OCT_STATES]

    ys = []
    for o in range(OCTETS):
        u = u_refs[o][...]
        bur_ref[o] = _dot(u.astype(BF16), bb_ref[o, :, :OCT_STATES])
        bui_ref[o] = _dot(u.astype(BF16), bb_ref[o, :, OCT_STATES:])
        xr, xi = _s5_scan_octet(tab_ref, o, bur_ref, bui_ref, xr_ref[o], xi_ref[o], t_steps,
                                hist=(hr_ref, hi_ref))
        xr_ref[o] = xr
        xi_ref[o] = xi
        ys.append(_dot(hr_ref[o].astype(BF16), cre_ref[o]) - _dot(hi_ref[o].astype(BF16), cim_ref[o])
                  + d_ref[:, o * LANES:(o + 1) * LANES] * u)

    hact = _gelu_tanh(jnp.concatenate(ys, axis=1))
    gate = _sigmoid(_dot(hact.astype(BF16), wglu_ref[...]) + bglu_ref[...])
    res = _rms(hact * gate, g_ref[...])
    for o in range(OCTETS):
        y_refs[o][...] = res[:, o * LANES:(o + 1) * LANES]
        for r in range(STREAMS):
            o_ref[r, :, o * LANES:(o + 1) * LANES] = y_refs[o][pl.ds(r, t_steps, stride=STREAMS), :]


def _s5_block_diag_b(b):
    bt = b.reshape(OCTETS, 8, S5_STATE, S5_GROUP).transpose(0, 1, 3, 2)
    eye = jnp.eye(8, dtype=b.dtype)
    return jnp.einsum('ogpn,gh->ogphn', bt, eye).reshape(OCTETS, 8 * S5_GROUP, OCT_STATES)


def _s5_block_diag_c(c):
    ct = c.reshape(OCTETS, 8, S5_GROUP, S5_STATE).transpose(0, 1, 3, 2)
    eye = jnp.eye(8, dtype=c.dtype)
    return jnp.einsum('ognp,gh->ognhp', ct, eye).reshape(OCTETS, OCT_STATES, 8 * S5_GROUP)


def _s5(us, lam_re, lam_im, log_dt, b_re, b_im, c_re, c_im, d_skip, w_glu, b_glu, norm_g, t_steps):
    rows = us[0].shape[0]
    ls = rows // STREAMS
    tile = STREAMS * t_steps
    nsteps = ls // t_steps
    tab, bb = _s5_prep(lam_re, lam_im, log_dt, _s5_block_diag_b(b_re), _s5_block_diag_b(b_im), ls)
    cre, cim = _s5_block_diag_c(c_re).astype(BF16), _s5_block_diag_c(c_im).astype(BF16)
    nstate = S5_GROUPS * S5_STATE
    state_scratch = [pltpu.VMEM((OCTETS, STREAMS, OCT_STATES), F32)] * 2
    bu_scratch = [pltpu.VMEM((OCTETS, tile, OCT_STATES), F32)] * 2
    u_specs = [pl.BlockSpec((tile, LANES), lambda i: (i, 0))] * OCTETS

    x0 = pl.pallas_call(
        functools.partial(_s5_pass_a_kernel, t_steps=t_steps),
        grid=(nsteps,),
        in_specs=u_specs + [_const_spec(tab.shape), _const_spec(bb.shape)],
        out_specs=_const_spec((STREAMS, 2 * nstate)),
        out_shape=jax.ShapeDtypeStruct((STREAMS, 2 * nstate), F32),
        scratch_shapes=state_scratch + bu_scratch,
        compiler_params=_cparams("arbitrary"),
        name="s5_pass_a",
    )(*us, tab, bb)

    return pl.pallas_call(
        functools.partial(_s5_pass_b_kernel, t_steps=t_steps),
        grid=(nsteps,),
        in_specs=u_specs + [_const_spec(tab.shape), _const_spec(bb.shape),
                            _const_spec(x0.shape), _const_spec(cre.shape), _const_spec(cim.shape),
                            _const_spec((1, S5_WIDTH)), _const_spec(w_glu.shape), _const_spec((1, S5_WIDTH)),
                            _const_spec((1, S5_WIDTH))],
        out_specs=pl.BlockSpec((STREAMS, t_steps, S5_WIDTH), lambda i: (0, i, 0)),
        out_shape=jax.ShapeDtypeStruct((STREAMS, ls, S5_WIDTH), F32),
        scratch_shapes=state_scratch + bu_scratch + bu_scratch + [pltpu.VMEM((tile, LANES), F32)] * OCTETS,
        compiler_params=_cparams("arbitrary"),
        name="s5_pass_b",
    )(*us, tab, bb, x0, cre, cim, d_skip.reshape(1, -1), w_glu.astype(BF16), b_glu.reshape(1, -1),
      norm_g.reshape(1, -1))


def _mlp_block(hs, gpre_ref, w1_ref, w2_ref, gpost_ref):
    hn = _rms(hs, gpre_ref[...]).astype(BF16)
    acc = jnp.zeros(hs.shape, F32)
    for f in range(D_FF // FF_TILE):
        a = jnp.maximum(_dot(hn, w1_ref[:, f * FF_TILE:(f + 1) * FF_TILE]), 0.0)
        acc = acc + _dot((a * a).astype(BF16), w2_ref[f * FF_TILE:(f + 1) * FF_TILE, :])
    return hs + _rms(acc, gpost_ref[...])


def _resident(shape):
    nd = len(shape)
    return pl.BlockSpec(shape, lambda *_: (0,) * nd, pipeline_mode=pl.Buffered(1))


def _even_out_kernel(osb_ref, os5_ref, *refs, seq_tile):
    if seq_tile is None:
        hs_ref, *refs = refs
        hs = hs_ref[...]
    else:
        x_ref, meta_ref, *refs = refs
        hs = _seq_tile(x_ref[...], meta_ref, pl.program_id(0), *seq_tile)
    gsb_ref, wsb_ref, ws5_ref, gpost_ref, gpre_mlp_ref, w1_ref, w2_ref, gpost_mlp_ref, o_ref = refs
    sb = _rms(osb_ref[...], gsb_ref[...]).astype(BF16)
    mix = _dot(sb, wsb_ref[...]) + _dot(os5_ref[...].astype(BF16), ws5_ref[...])
    hs = hs + _rms(mix, gpost_ref[...])
    o_ref[...] = _mlp_block(hs, gpre_mlp_ref, w1_ref, w2_ref, gpost_mlp_ref)


def _row_tiles(lp, keep):
    if keep is None:
        tm = _largest_div(lp, 640, 16)
        return tm, lp // tm, None
    first, count = keep
    tm = _largest_div(count, 640, 16)
    return tm, count // tm, first


def _rows_spec(shape, tm, first, row_axis=0):
    nd = len(shape)
    if first is None:
        block = tuple(tm if a == row_axis else s for a, s in enumerate(shape))
        return pl.BlockSpec(block, lambda i: tuple(i if a == row_axis else 0 for a in range(nd)))
    assert first % 16 == 0 and tm % 16 == 0
    block = tuple(pl.Element(tm if a == row_axis else s) for a, s in enumerate(shape))
    return pl.BlockSpec(block, lambda i: tuple(pl.multiple_of(first + i * tm, 16) if a == row_axis else 0
                                               for a in range(nd)))


def _even_out(o_sb, o_s5, hs, g_sb, w_out, g_post, g_pre_mlp, w1, w2, g_post_mlp, keep=None):
    lp = o_sb.shape[0]
    w_sb, w_s5 = w_out[:SB_WIDTH], w_out[SB_WIDTH:]
    tm, steps, first = _row_tiles(lp, keep)
    if isinstance(hs, tuple):
        x, meta = hs
        seq_tile = (first or 0, tm, steps, x.shape[0])
        res, res_specs = [x, meta], [_x_window_spec(x.shape[0], first or 0, tm), _const_spec(meta.shape)]
    else:
        seq_tile = None
        res, res_specs = [hs], [_rows_spec(hs.shape, tm, first)]
    return pl.pallas_call(
        functools.partial(_even_out_kernel, seq_tile=seq_tile),
        grid=(steps,),
        in_specs=[_rows_spec(o_sb.shape, tm, first), _rows_spec(o_s5.shape, tm, first), *res_specs,
                  _const_spec((1, SB_WIDTH)), _resident(w_sb.shape), _resident(w_s5.shape),
                  _const_spec((1, D_MODEL)),
                  _const_spec((1, D_MODEL)), _resident(w1.shape), _resident(w2.shape), _const_spec((1, D_MODEL))],
        out_specs=pl.BlockSpec((tm, D_MODEL), lambda i: (i, 0)),
        out_shape=jax.ShapeDtypeStruct((steps * tm, D_MODEL), F32),
        compiler_params=_cparams("parallel"),
        name="even_out_mlp",
    )(o_sb, o_s5, *res, g_sb, w_sb, w_s5, g_post, g_pre_mlp, w1, w2, g_post_mlp)


def _odd_in_kernel(hs_ref, g_ref, wqkv_ref, wz_ref, wabc_ref, conv_ref, alc_ref, dtc_ref,
                   q_ref, k_ref, v_ref, z_ref, gcol_ref, grow_ref, carry_ref, *, tm):
    @pl.when(pl.program_id(0) == 0)
    def _():
        carry_ref[...] = jnp.zeros_like(carry_ref)

    hn = _rms(hs_ref[...], g_ref[...]).astype(BF16)

    ab = _dot(hn, wabc_ref[...])
    lane = lax.broadcasted_iota(jnp.int32, (1, LANES), 1)
    gb = jnp.where(lane < DN_HEADS, -jnp.exp(alc_ref[...]) * _softplus(ab + dtc_ref[...]), _sigmoid(ab))
    tr = lax.broadcasted_iota(jnp.int32, (DN_CHUNK, DN_CHUNK), 0)
    tc = lax.broadcasted_iota(jnp.int32, (DN_CHUNK, DN_CHUNK), 1)
    tri = jnp.where(tr >= tc, 1.0, 0.0).astype(BF16)
    hi, mid, lo = _split3(gb)
    gsum = jnp.concatenate(
        [_dot(tri, hi[c:c + DN_CHUNK]) + _dot(tri, mid[c:c + DN_CHUNK]) + _dot(tri, lo[c:c + DN_CHUNK])
         for c in range(0, tm, DN_CHUNK)], axis=0)
    gcum = jnp.where(lane < DN_HEADS, gsum, gb)
    gcol_ref[...] = gcum[:, :2 * DN_HEADS]
    grow_ref[...] = gcum.T[:2 * DN_HEADS]

    first = lax.broadcasted_iota(jnp.int32, (SUBLANES, 1), 0)
    w = DN_HEADS * DN_HEAD_DIM
    hd = DN_HEAD_DIM

    def conv_silu(c0):
        cols = slice(c0, c0 + 2 * hd)
        x = _dot(hn, wqkv_ref[:, cols])
        prev = carry_ref[:, cols]
        carry_ref[:, cols] = x[tm - SUBLANES:, :]
        y = x * conv_ref[DN_CONV - 1:DN_CONV, cols]
        for s in range(1, DN_CONV):
            sh = pltpu.roll(x, s, 0)
            head = jnp.where(first < s, pltpu.roll(prev, s, 0), sh[:SUBLANES])
            sh = jnp.concatenate([head, sh[SUBLANES:]], axis=0)
            y = y + sh * conv_ref[DN_CONV - 1 - s:DN_CONV - s, cols]
        return _silu(y)

    for h in range(0, DN_HEADS, 2):
        q2 = conv_silu(h * hd)
        k2 = conv_silu(w + h * hd)
        v2 = conv_silu(2 * w + h * hd)
        z2 = _dot(hn, wz_ref[:, h * hd:(h + 2) * hd])
        for d in range(2):
            sl = slice(d * hd, (d + 1) * hd)
            qh, kh = q2[:, sl], k2[:, sl]
            q_ref[h + d] = (qh * (lax.rsqrt(jnp.sum(qh * qh, axis=-1, keepdims=True) + EPS)
                                  * (hd ** -0.5))).astype(BF16)
            k_ref[h + d] = (kh * lax.rsqrt(jnp.sum(kh * kh, axis=-1, keepdims=True) + EPS)).astype(BF16)
            v_ref[h + d] = v2[:, sl].astype(BF16)
            z_ref[h + d] = z2[:, sl].astype(BF16)


def _odd_in(hs, g, w_in, conv_w, a_log, dt_bias, tm):
    lp = hs.shape[0]
    w = DN_HEADS * DN_HEAD_DIM
    wqkv = w_in[:, :3 * w].astype(BF16)
    wz = w_in[:, 3 * w:4 * w].astype(BF16)
    wab = w_in[:, 4 * w:]
    wabc = jnp.pad(wab, ((0, 0), (0, LANES - 2 * DN_HEADS))).astype(BF16)
    pad_row = lambda a: jnp.pad(a.reshape(1, -1), ((0, 0), (0, LANES - DN_HEADS)))
    row_spec = lambda width: pl.BlockSpec((tm, width), lambda i: (i, 0))
    act = jax.ShapeDtypeStruct((DN_HEADS, lp, DN_HEAD_DIM), BF16)
    act_spec = pl.BlockSpec((DN_HEADS, tm, DN_HEAD_DIM), lambda i: (0, i, 0))
    return pl.pallas_call(
        functools.partial(_odd_in_kernel, tm=tm),
        grid=(lp // tm,),
        in_specs=[row_spec(D_MODEL), _const_spec((1, D_MODEL)), _resident(wqkv.shape), _resident(wz.shape),
                  _const_spec(wabc.shape), _const_spec(conv_w.shape),
                  _const_spec((1, LANES)), _const_spec((1, LANES))],
        out_specs=[act_spec, act_spec, act_spec, act_spec, row_spec(2 * DN_HEADS),
                   pl.BlockSpec((2 * DN_HEADS, tm), lambda i: (0, i))],
        out_shape=[act, act, act, act,
                   jax.ShapeDtypeStruct((lp, 2 * DN_HEADS), F32),
                   jax.ShapeDtypeStruct((2 * DN_HEADS, lp), F32)],
        scratch_shapes=[pltpu.VMEM((SUBLANES, 3 * w), F32)],
        compiler_params=_cparams("arbitrary"),
        name="odd_in",
    )(hs, g, wqkv, wz, wabc, conv_w, pad_row(a_log), pad_row(dt_bias))


def _bdot(a, b):
    return lax.dot_general(a, b, (((2,), (1,)), ((0,), (0,))), preferred_element_type=F32)


def _bdot_nt(a, b):
    return lax.dot_general(a, b, (((2,), (2,)), ((0,), (0,))), preferred_element_type=F32)


def _bdot_tn(a, b):
    return lax.dot_general(a, b, (((1,), (1,)), ((0,), (0,))), preferred_element_type=F32)


def _dn_kernel(q_ref, k_ref, v_ref, z_ref, gcol_ref, grow_ref, ng_ref, o_ref, s_ref, *set_refs, chunks):
    c64, nh, hd, grp = DN_CHUNK, DN_HEADS, DN_HEAD_DIM, DN_GROUP
    nb = nh * grp
    sets = (set_refs[:5], set_refs[5:])

    @pl.when(pl.program_id(0) == 0)
    def _():
        s_ref[...] = jnp.zeros_like(s_ref)

    ri = lax.broadcasted_iota(jnp.int32, (c64, c64), 0)
    ci = lax.broadcasted_iota(jnp.int32, (c64, c64), 1)
    incl = ri >= ci
    strict = ri > ci
    eye = jnp.where(ri == ci, 1.0, 0.0)
    pair_masks = []
    s = 1
    while s < c64:
        pair_masks.append(jnp.logical_and((ri // s) % 2 == 1, (ri // s) - 1 == ci // s))
        s *= 2
    ng = ng_ref[...]

    def prep(gi, dst):
        u_scr, wq_scr, a_scr, kt_scr, egl_scr = dst
        c0 = gi * grp
        rows = pl.ds(pl.multiple_of(c0 * c64, grp * c64), grp * c64)
        cs = pl.ds(c0, grp)
        qb = q_ref[:, rows, :].reshape(nb, c64, hd)
        kb = k_ref[:, rows, :].reshape(nb, c64, hd)
        vb = v_ref[:, rows, :].reshape(nb, c64, hd)
        kf, vf = kb.astype(F32), vb.astype(F32)
        gcb = gcol_ref[rows, :]
        gc = jnp.stack([gcb[:, h:h + 1] for h in range(nh)], axis=0).reshape(nb, c64, 1)
        be = jnp.stack([gcb[:, nh + h:nh + h + 1] for h in range(nh)], axis=0).reshape(nb, c64, 1)
        gr = grow_ref[0:nh, cs].reshape(nb, 1, c64)
        gl = gc[:, c64 - 1:c64, :]
        eg = jnp.exp(gc)
        decay = jnp.exp(jnp.where(incl, gc - gr, -jnp.inf))
        qkk = _bdot_nt(jnp.concatenate([qb, kb], axis=1), kb)
        n = jnp.where(strict, be * qkk[:, c64:] * decay, 0.0)
        x = eye - jnp.where(pair_masks[0], n, 0.0)
        for m in pair_masks[1:]:
            xb = x.astype(BF16)
            x = x - _bdot(xb, _bdot(jnp.where(m, n, 0.0).astype(BF16), xb).astype(BF16))
        rhs = jnp.concatenate([(vf * be).astype(BF16), (kf * (be * eg)).astype(BF16)], axis=2)
        uw = _bdot(x.astype(BF16), rhs)
        qg = (qb.astype(F32) * eg).astype(BF16)
        u_scr[...] = uw[:, :, :hd].reshape(nh, grp, c64, hd)
        wq_scr[...] = jnp.concatenate([uw[:, :, hd:].astype(BF16), qg], axis=1).reshape(nh, grp, 2 * c64, hd)
        a_scr[...] = (qkk[:, :c64] * decay).astype(BF16).reshape(nh, grp, c64, c64)
        kt_scr[...] = (kf * jnp.exp(gl - gc)).astype(BF16).reshape(nh, grp, c64, hd)
        egl_scr[...] = jnp.broadcast_to(jnp.exp(gl), (nb, 1, hd)).reshape(nh, grp, 1, hd)

    def recur(gi, src):
        u_scr, wq_scr, a_scr, kt_scr, egl_scr = src
        for cl in range(grp):
            rows = pl.ds(pl.multiple_of((gi * grp + cl) * c64, c64), c64)
            s = s_ref[...]
            ws = _bdot(wq_scr[:, cl], s.astype(BF16))
            vnb = (u_scr[:, cl] - ws[:, :c64]).astype(BF16)
            o = ws[:, c64:] + _bdot(a_scr[:, cl], vnb)
            s_ref[...] = s * egl_scr[:, cl] + _bdot_tn(kt_scr[:, cl], vnb)
            on = o * lax.rsqrt(jnp.mean(o * o, axis=-1, keepdims=True) + EPS) * ng
            o_ref[:, rows, :] = (on * _silu(z_ref[:, rows, :].astype(F32))).astype(BF16)

    ngroups = chunks // grp
    prep(0, sets[0])

    def two_groups(m, carry):
        prep(2 * m + 1, sets[1])
        recur(2 * m, sets[0])
        prep(2 * m + 2, sets[0])
        recur(2 * m + 1, sets[1])
        return carry

    lax.fori_loop(0, (ngroups - 1) // 2, two_groups, 0)
    if ngroups % 2 == 0:
        prep(ngroups - 1, sets[1])
        recur(ngroups - 2, sets[0])
    recur(ngroups - 1, sets[(ngroups - 1) % 2])


def _dn(q, k, v, z, gcol, grow, norm_g, chunks):
    nh, lp, hd = q.shape
    rows = chunks * DN_CHUNK
    grow4 = grow.reshape(2 * nh, lp // DN_CHUNK, 1, DN_CHUNK)
    act_spec = pl.BlockSpec((nh, rows, hd), lambda i: (0, i, 0))
    return pl.pallas_call(
        functools.partial(_dn_kernel, chunks=chunks),
        grid=(lp // rows,),
        in_specs=[act_spec, act_spec, act_spec, act_spec,
                  pl.BlockSpec((rows, 2 * nh), lambda i: (i, 0)),
                  pl.BlockSpec((2 * nh, chunks, 1, DN_CHUNK), lambda i: (0, i, 0, 0)),
                  _const_spec((1, hd))],
        out_specs=act_spec,
        out_shape=jax.ShapeDtypeStruct((nh, lp, hd), BF16),
        scratch_shapes=[pltpu.VMEM((nh, hd, hd), F32)] + 2 * [
            pltpu.VMEM((nh, DN_GROUP, DN_CHUNK, hd), F32),
            pltpu.VMEM((nh, DN_GROUP, 2 * DN_CHUNK, hd), BF16),
            pltpu.VMEM((nh, DN_GROUP, DN_CHUNK, DN_CHUNK), BF16),
            pltpu.VMEM((nh, DN_GROUP, DN_CHUNK, hd), BF16),
            pltpu.VMEM((nh, DN_GROUP, 1, hd), F32)],
        compiler_params=_cparams("arbitrary"),
        name="dn",
    )(q, k, v, z, gcol, grow4, norm_g.reshape(1, -1))


def _odd_out_kernel(og_ref, hs_ref, w_ref, gpost_ref, gpre_mlp_ref, w1_ref, w2_ref, gpost_mlp_ref, o_ref):
    og = jnp.concatenate([og_ref[h] for h in range(DN_HEADS)], axis=1)
    hs = hs_ref[...] + _rms(_dot(og, w_ref[...]), gpost_ref[...])
    o_ref[...] = _mlp_block(hs, gpre_mlp_ref, w1_ref, w2_ref, gpost_mlp_ref)


def _odd_out(og, hs, w_out, g_post, g_pre_mlp, w1, w2, g_post_mlp, keep=None):
    lp = hs.shape[0]
    tm, steps, first = _row_tiles(lp, keep)
    return pl.pallas_call(
        _odd_out_kernel,
        grid=(steps,),
        in_specs=[_rows_spec(og.shape, tm, first, row_axis=1), _rows_spec(hs.shape, tm, first),
                  _resident(w_out.shape), _const_spec((1, D_MODEL)),
                  _const_spec((1, D_MODEL)), _resident(w1.shape), _resident(w2.shape), _const_spec((1, D_MODEL))],
        out_specs=pl.BlockSpec((tm, D_MODEL), lambda i: (i, 0)),
        out_shape=jax.ShapeDtypeStruct((steps * tm, D_MODEL), F32),
        compiler_params=_cparams("parallel"),
        name="odd_out_mlp",
    )(og, hs, w_out, g_post, g_pre_mlp, w1, w2, g_post_mlp)


def kernel(x, meta_tokens, pre_mix_norm, post_mix_norm, pre_mlp_norm, post_mlp_norm, mlp_w1, mlp_w2, w_in_even, w_out_even, sb_out_norm, s5_lambda_re, s5_lambda_im, s5_log_dt, s5_b_re, s5_b_im, s5_c_re, s5_c_im, s5_d, s5_w_glu, s5_b_glu, s5_out_norm, w_in_odd, dn_conv_w, dn_a_log, dn_dt_bias, dn_out_norm, w_out_odd):
    assert x.shape[0] == 1 and x.shape[2] == D_MODEL
    depth = pre_mix_norm.shape[0]
    l = N_META + x.shape[1]
    lp = -(-l // SB_TILE) * SB_TILE
    ls = lp // STREAMS
    tr_even = _largest_div(ls, 104, 16)
    t_s5 = _largest_div(ls, 128, SUBLANES)
    tm_odd = _largest_div(lp, 640, max(LANES, DN_CHUNK))
    dn_chunks = _largest_div(lp // DN_CHUNK, 10, DN_GROUP)
    row = lambda a: a.reshape(1, -1)

    hs = (x[0], meta_tokens)
    for i in range(depth):
        j = i // 2
        mlp = (row(pre_mlp_norm[i]), mlp_w1[i].astype(BF16), mlp_w2[i].astype(BF16), row(post_mlp_norm[i]))
        keep = (N_META, l - N_META) if i == depth - 1 else None
        if i % 2 == 0:
            qkv, u_il = _even_in(hs, row(pre_mix_norm[i]), w_in_even[j].astype(BF16), tr_even, lp)
            o_sb = _sb_attention(qkv, SB_TILE)
            o_s5 = _s5(u_il, s5_lambda_re[j], s5_lambda_im[j], s5_log_dt[j],
                       s5_b_re[j], s5_b_im[j], s5_c_re[j], s5_c_im[j], s5_d[j], s5_w_glu[j], s5_b_glu[j],
                       s5_out_norm[j], t_s5)
            hs = _even_out(o_sb, o_s5.reshape(lp, S5_WIDTH), hs, row(sb_out_norm[j]), w_out_even[j].astype(BF16),
                           row(post_mix_norm[i]), *mlp, keep)
        else:
            q, k, v, z, gcol, grow = _odd_in(hs, row(pre_mix_norm[i]), w_in_odd[j], dn_conv_w[j], dn_a_log[j],
                                             dn_dt_bias[j], tm_odd)
            og = _dn(q, k, v, z, gcol, grow, dn_out_norm[j], dn_chunks)
            hs = _odd_out(og, hs, w_out_odd[j].astype(BF16), row(post_mix_norm[i]), *mlp, keep)
    return hs[None]
```
